```python
import math
import jax
import jax.numpy as jnp
from jax import lax
import numpy as np

D_MODEL = 1024
BATCH = 8
SEQ = 2048
DEPTH = 2

EPS = 1e-6
CONV_K = 4

SSD_D_INNER = 2 * D_MODEL
SSD_HEAD_DIM = 64
SSD_HEADS = SSD_D_INNER // SSD_HEAD_DIM
SSD_GROUPS = 4
SSD_STATE = 128
SSD_CHUNK = 128
SSD_XBC = SSD_D_INNER + 2 * SSD_GROUPS * SSD_STATE

GDN_HEADS = 8
GDN_HEAD_DIM = 128
GDN_DIM = GDN_HEADS * GDN_HEAD_DIM
GDN_CHUNK = 64

MOE_GROUPS = 4
MOE_EXPERTS_PER_GROUP = 4
MOE_EXPERTS = MOE_GROUPS * MOE_EXPERTS_PER_GROUP
MOE_TOP_K = 2
MOE_D_FF = 512

SPLIT_SIZES = (SSD_D_INNER, SSD_XBC, SSD_HEADS, 3 * GDN_DIM, GDN_HEADS, GDN_HEADS, GDN_DIM, 2 * D_MODEL)
IN_PROJ_DIM = SSD_D_INNER + SSD_XBC + SSD_HEADS + 3 * GDN_DIM + 2 * GDN_HEADS + GDN_DIM + 2 * D_MODEL

kernel_name = 'hybrid_ssd_gdn_hier_moe'


def split_points(sizes):
    return [int(v) for v in np.cumsum(sizes)[:-1]]


def rms_norm(x, w):
    xf = x.astype(jnp.float32)
    y = xf * lax.rsqrt(jnp.mean(xf * xf, axis=-1, keepdims=True) + EPS)
    return (y * w.astype(jnp.float32)).astype(x.dtype)


def l2_normalize(x):
    return x * lax.rsqrt(jnp.sum(x * x, axis=-1, keepdims=True) + EPS)


def causal_depthwise_conv(u, w):
    return lax.conv_general_dilated(
        u, w[:, None, :].astype(u.dtype), window_strides=(1,), padding=[(CONV_K - 1, 0)],
        dimension_numbers=('NWC', 'WIO', 'NWC'), feature_group_count=u.shape[-1])


def ssd_chunked(xh, dt, a, bm, cm):
    bsz, seqlen, nh, hp = xh.shape
    ng, ns = bm.shape[2], bm.shape[3]
    r = nh // ng
    nc = seqlen // SSD_CHUNK
    q = SSD_CHUNK
    x = (xh * dt[..., None]).reshape(bsz, nc, q, ng, r, hp)
    a_cs = jnp.cumsum((dt * a).reshape(bsz, nc, q, ng, r), axis=2)
    bc = bm.reshape(bsz, nc, q, ng, ns)
    cc = cm.reshape(bsz, nc, q, ng, ns)
    at = jnp.moveaxis(a_cs, 2, -1)
    causal = jnp.tril(jnp.ones((q, q), dtype=bool))
    seg = at[..., :, None] - at[..., None, :]
    lmat = jnp.exp(jnp.where(causal, seg, -jnp.inf))
    scores = jnp.einsum('bclgn,bcsgn->bcgls', cc, bc)
    y_diag = jnp.einsum('bcgrls,bcsgrp->bclgrp', scores[:, :, :, None] * lmat, x)
    decay_to_end = jnp.moveaxis(jnp.exp(at[..., -1:] - at), -1, 2)
    states = jnp.einsum('bcsgn,bcsgrp->bcgrpn', bc, x * decay_to_end[..., None])
    chunk_decay = jnp.exp(at[..., -1])

    def step(hstate, inp):
        st, dec = inp
        return hstate * dec[..., None, None] + st, hstate

    h0 = jnp.zeros((bsz, ng, r, hp, ns), xh.dtype)
    _, prev = lax.scan(step, h0, (jnp.swapaxes(states, 0, 1), jnp.swapaxes(chunk_decay, 0, 1)))
    prev = jnp.swapaxes(prev, 0, 1)
    y_off = jnp.einsum('bclgn,bcgrpn->bclgrp', cc, prev) * jnp.exp(a_cs)[..., None]
    return (y_diag + y_off).reshape(bsz, seqlen, nh, hp)


def gated_delta_chunked(q, k, v, log_decay, beta):
    bsz, seqlen, nh, hd = q.shape
    nc = seqlen // GDN_CHUNK
    c = GDN_CHUNK

    def to_chunks(t):
        return t.reshape(bsz, nc, c, nh, -1).transpose(0, 1, 3, 2, 4)

    qc, kc, vc = to_chunks(q), to_chunks(k), to_chunks(v)
    gc = jnp.cumsum(log_decay.reshape(bsz, nc, c, nh).transpose(0, 1, 3, 2), axis=-1)
    bc = beta.reshape(bsz, nc, c, nh).transpose(0, 1, 3, 2)[..., None]
    incl = jnp.tril(jnp.ones((c, c), dtype=bool))
    strict = jnp.tril(jnp.ones((c, c), dtype=bool), -1)
    decay = jnp.exp(jnp.where(incl, gc[..., :, None] - gc[..., None, :], -jnp.inf))
    kb = kc * bc
    m = jnp.where(strict, jnp.einsum('bchid,bchjd->bchij', kb, kc) * decay, 0.0)
    eye = jnp.eye(c, dtype=q.dtype)
    t_inv = lax.linalg.triangular_solve(eye + m, jnp.broadcast_to(eye, m.shape),
                                        left_side=True, lower=True, unit_diagonal=True)
    u = t_inv @ (vc * bc)
    w = t_inv @ (kb * jnp.exp(gc)[..., None])
    a_qk = jnp.einsum('bchid,bchjd->bchij', qc, kc) * decay
    g_last = gc[..., -1]
    k_dec = kc * jnp.exp(g_last[..., None] - gc)[..., None]
    q_dec = qc * jnp.exp(gc)[..., None]

    def step(s, inp):
        qd, kd, uu, ww, aqk, gl = inp
        v_new = uu - ww @ s
        o = qd @ s + aqk @ v_new
        s = s * jnp.exp(gl)[..., None, None] + jnp.swapaxes(kd, -1, -2) @ v_new
        return s, o

    s0 = jnp.zeros((bsz, nh, hd, vc.shape[-1]), q.dtype)
    xs = (jnp.swapaxes(q_dec, 0, 1), jnp.swapaxes(k_dec, 0, 1), jnp.swapaxes(u, 0, 1),
          jnp.swapaxes(w, 0, 1), jnp.swapaxes(a_qk, 0, 1), jnp.swapaxes(g_last, 0, 1))
    _, o = lax.scan(step, s0, xs)
    return o.transpose(1, 0, 3, 2, 4).reshape(bsz, seqlen, nh, -1)


def mixer_block(u, w_in, ssd_conv_w, ssd_conv_b, ssd_dt_bias, ssd_a_log, ssd_d, ssd_norm_w, w_ssd_proj,
                gdn_conv_w, gdn_dt_bias, gdn_a_log, gdn_norm_w, w_gdn_proj, b_merge, w_out):
    bsz, seqlen, _ = u.shape
    f32 = jnp.float32
    proj = u @ w_in
    z_s, xbc_s, dt_s, qkv_g, a_g, beta_g, z_g, gate_pre = jnp.split(proj, split_points(SPLIT_SIZES), axis=-1)

    xbc = jax.nn.silu(causal_depthwise_conv(xbc_s, ssd_conv_w) + ssd_conv_b.astype(u.dtype)).astype(f32)
    xs, bs, cs = jnp.split(xbc, [SSD_D_INNER, SSD_D_INNER + SSD_GROUPS * SSD_STATE], axis=-1)
    xh = xs.reshape(bsz, seqlen, SSD_HEADS, SSD_HEAD_DIM)
    dt = jax.nn.softplus(dt_s.astype(f32) + ssd_dt_bias.astype(f32))
    a = -jnp.exp(ssd_a_log.astype(f32))
    y = ssd_chunked(xh, dt, a, bs.reshape(bsz, seqlen, SSD_GROUPS, SSD_STATE),
                    cs.reshape(bsz, seqlen, SSD_GROUPS, SSD_STATE))
    y = y + ssd_d.astype(f32)[:, None] * xh
    y = rms_norm(y.reshape(bsz, seqlen, SSD_D_INNER) * jax.nn.silu(z_s.astype(f32)), ssd_norm_w)
    y_ssd = y.astype(u.dtype) @ w_ssd_proj

    qkv = jax.nn.silu(causal_depthwise_conv(qkv_g, gdn_conv_w)).astype(f32)
    q, k, v = [t.reshape(bsz, seqlen, GDN_HEADS, GDN_HEAD_DIM) for t in jnp.split(qkv, 3, axis=-1)]
    q = l2_normalize(q) * GDN_HEAD_DIM ** -0.5
    k = l2_normalize(k)
    log_decay = -jnp.exp(gdn_a_log.astype(f32)) * jax.nn.softplus(a_g.astype(f32) + gdn_dt_bias.astype(f32))
    beta = jax.nn.sigmoid(beta_g.astype(f32))
    o = gated_delta_chunked(q, k, v, log_decay, beta)
    o = rms_norm(o, gdn_norm_w) * jax.nn.silu(z_g.astype(f32).reshape(bsz, seqlen, GDN_HEADS, GDN_HEAD_DIM))
    y_gdn = o.reshape(bsz, seqlen, GDN_DIM).astype(u.dtype) @ w_gdn_proj

    gates = jax.nn.sigmoid(gate_pre.astype(f32) + b_merge.astype(f32))
    g_ssd, g_gdn = jnp.split(gates, 2, axis=-1)
    merged = (g_ssd * y_ssd.astype(f32) + g_gdn * y_gdn.astype(f32)).astype(u.dtype)
    return merged @ w_out


def hierarchical_moe(u, w_group, b_group, w_expert, b_expert, w_gate, w_up, w_down):
    bsz, seqlen, d = u.shape
    f32 = jnp.float32
    t = u.reshape(-1, d)
    n = t.shape[0]
    group_logits = (t @ w_group).astype(f32) + b_group.astype(f32)
    group_prob = jax.nn.softmax(group_logits, axis=-1)
    _, gsel = lax.top_k(group_logits, 1)
    group_w = jnp.take_along_axis(group_prob, gsel, axis=-1)
    exp_logits = ((t @ w_expert).astype(f32) + b_expert.astype(f32)).reshape(n, MOE_GROUPS, MOE_EXPERTS_PER_GROUP)
    sel_logits = jnp.take_along_axis(exp_logits, gsel[:, :, None], axis=1)[:, 0]
    exp_prob = jax.nn.softmax(sel_logits, axis=-1)
    top_p, top_i = lax.top_k(exp_prob, MOE_TOP_K)
    top_p = top_p / jnp.sum(top_p, axis=-1, keepdims=True)
    expert_idx = gsel * MOE_EXPERTS_PER_GROUP + top_i
    combine = jnp.sum(jax.nn.one_hot(expert_idx, MOE_EXPERTS, dtype=f32) * (group_w * top_p)[..., None], axis=1)
    h_gate = jnp.einsum('td,edf->tef', t, w_gate)
    h_up = jnp.einsum('td,edf->tef', t, w_up)
    h = jax.nn.silu(h_gate) * h_up * combine.astype(t.dtype)[..., None]
    y = jnp.einsum('tef,efd->td', h, w_down)
    return y.reshape(bsz, seqlen, d)


def setup_inputs(seed: int = 0) -> dict:
    key = jax.random.key(seed)
    ks = jax.random.split(key, 32)
    f32 = jnp.float32

    def dense(k, shape, fan_in):
        return jax.random.normal(k, shape, f32) * fan_in ** -0.5

    def gain(k, shape):
        return 1.0 + 0.02 * jax.random.normal(k, shape, f32)

    def small(k, shape, scale=0.02):
        return scale * jax.random.normal(k, shape, f32)

    def dt_bias(k, shape):
        dt = jnp.exp(jax.random.uniform(k, shape, f32, math.log(1e-3), math.log(1e-1)))
        return dt + jnp.log(-jnp.expm1(-dt))

    def a_log(k, shape):
        return jnp.log(jax.random.uniform(k, shape, f32, 1.0, 16.0))

    return {
        'x': jax.random.normal(ks[0], (BATCH, SEQ, D_MODEL), f32),
        'ln1_w': gain(ks[1], (DEPTH, D_MODEL)),
        'w_in': dense(ks[2], (DEPTH, D_MODEL, IN_PROJ_DIM), D_MODEL),
        'ssd_conv_w': dense(ks[3], (DEPTH, CONV_K, SSD_XBC), CONV_K),
        'ssd_conv_b': small(ks[4], (DEPTH, SSD_XBC)),
        'ssd_dt_bias': dt_bias(ks[5], (DEPTH, SSD_HEADS)),
        'ssd_a_log': a_log(ks[6], (DEPTH, SSD_HEADS)),
        'ssd_d': gain(ks[7], (DEPTH, SSD_HEADS)),
        'ssd_norm_w': gain(ks[8], (DEPTH, SSD_D_INNER)),
        'w_ssd_proj': dense(ks[9], (DEPTH, SSD_D_INNER, D_MODEL), SSD_D_INNER),
        'gdn_conv_w': dense(ks[10], (DEPTH, CONV_K, 3 * GDN_DIM), CONV_K),
        'gdn_dt_bias': dt_bias(ks[11], (DEPTH, GDN_HEADS)),
        'gdn_a_log': a_log(ks[12], (DEPTH, GDN_HEADS)),
        'gdn_norm_w': gain(ks[13], (DEPTH, GDN_HEAD_DIM)),
        'w_gdn_proj': dense(ks[14], (DEPTH, GDN_DIM, D_MODEL), GDN_DIM),
        'b_merge': small(ks[15], (DEPTH, 2 * D_MODEL)),
        'w_out': dense(ks[16], (DEPTH, D_MODEL, D_MODEL), D_MODEL),
        'ln2_w': gain(ks[17], (DEPTH, D_MODEL)),
        'moe_w_group': dense(ks[18], (DEPTH, D_MODEL, MOE_GROUPS), D_MODEL),
        'moe_b_group': small(ks[19], (DEPTH, MOE_GROUPS), 0.01),
        'moe_w_expert': dense(ks[20], (DEPTH, D_MODEL, MOE_EXPERTS), D_MODEL),
        'moe_b_expert': small(ks[21], (DEPTH, MOE_EXPERTS), 0.01),
        'moe_w_gate': dense(ks[22], (DEPTH, MOE_EXPERTS, D_MODEL, MOE_D_FF), D_MODEL),
        'moe_w_up': dense(ks[23], (DEPTH, MOE_EXPERTS, D_MODEL, MOE_D_FF), D_MODEL),
        'moe_w_down': dense(ks[24], (DEPTH, MOE_EXPERTS, MOE_D_FF, D_MODEL), MOE_D_FF),
        'final_norm_w': gain(ks[25], (D_MODEL,)),
    }


def reference(x, ln1_w, w_in, ssd_conv_w, ssd_conv_b, ssd_dt_bias, ssd_a_log, ssd_d, ssd_norm_w, w_ssd_proj,
              gdn_conv_w, gdn_dt_bias, gdn_a_log, gdn_norm_w, w_gdn_proj, b_merge, w_out, ln2_w,
              moe_w_group, moe_b_group, moe_w_expert, moe_b_expert, moe_w_gate, moe_w_up, moe_w_down,
              final_norm_w):
    h = x
    for i in range(DEPTH):
        h = h + mixer_block(rms_norm(h, ln1_w[i]), w_in[i], ssd_conv_w[i], ssd_conv_b[i], ssd_dt_bias[i],
                            ssd_a_log[i], ssd_d[i], ssd_norm_w[i], w_ssd_proj[i], gdn_conv_w[i],
                            gdn_dt_bias[i], gdn_a_log[i], gdn_norm_w[i], w_gdn_proj[i], b_merge[i], w_out[i])
        h = h + hierarchical_moe(rms_norm(h, ln2_w[i]), moe_w_group[i], moe_b_group[i], moe_w_expert[i],
                                 moe_b_expert[i], moe_w_gate[i], moe_w_up[i], moe_w_down[i])
    return rms_norm(h, final_norm_w)
```

```python
import functools

import jax
import jax.numpy as jnp
from jax import lax
from jax.experimental import pallas as pl
from jax.experimental.pallas import tpu as pltpu

F32 = jnp.float32
BF16 = jnp.bfloat16

EPS = 1e-6
CONV_K = 4
LANES = 128
HALO = 8

D_MODEL = 1024
SSD_D_INNER = 2 * D_MODEL
SSD_HEAD_DIM = 64
SSD_HEADS = SSD_D_INNER // SSD_HEAD_DIM
SSD_GROUPS = 4
SSD_STATE = 128
SSD_CHUNK = 128
SSD_XBC = SSD_D_INNER + 2 * SSD_GROUPS * SSD_STATE
GDN_HEADS = 8
GDN_HEAD_DIM = 128
GDN_DIM = GDN_HEADS * GDN_HEAD_DIM
GDN_CHUNK = 64
MOE_GROUPS = 4
MOE_EPG = 4
MOE_EXPERTS = MOE_GROUPS * MOE_EPG
MOE_D_FF = 512

MAIN_COLS = SSD_XBC + 3 * GDN_DIM + SSD_D_INNER + 2 * D_MODEL + GDN_DIM
XBC_BLK, QKV_BLK = 0, 1
ZS_BLK, GATE_BLK = 3, 4
ZG_BLK = 10
SM_DT, SM_A, SM_BETA = 0, SSD_HEADS, SSD_HEADS + GDN_HEADS
RT_G, RT_E = 0, MOE_GROUPS

VMEM_LIMIT = 56 * 1024 * 1024


def _dot(a, b):
    return jnp.dot(a, b, preferred_element_type=F32)


def _dot_nt(a, b):
    return lax.dot_general(a, b, (((1,), (1,)), ((), ())), preferred_element_type=F32)


def _split(x):
    hi = x.astype(BF16)
    lo = (x - hi.astype(F32)).astype(BF16)
    return hi, lo


def _dot_split_lhs(x, w):
    hi, lo = _split(x)
    return _dot(hi, w) + _dot(lo, w)


def _dot_split_rhs(w, x):
    hi, lo = _split(x)
    return _dot(w, hi) + _dot(w, lo)


def _sigmoid(x):
    return 1.0 / (1.0 + jnp.exp(-x))


def _silu(x):
    return x * _sigmoid(x)


def _softplus(x):
    return jnp.maximum(x, 0.0) + jnp.log1p(jnp.exp(-jnp.abs(x)))


def _causal_conv(cur, halo_ref, w, first):
    t = cur.shape[0]

    @pl.when(first)
    def _():
        halo_ref[...] = jnp.zeros_like(halo_ref)

    ext = jnp.concatenate([halo_ref[...], cur], axis=0)
    acc = None
    for k in range(CONV_K):
        off = HALO - (CONV_K - 1) + k
        term = ext[off:off + t, :] * w[k:k + 1, :]
        acc = term if acc is None else acc + term
    halo_ref[...] = cur[t - HALO:, :]
    return acc


def _inproj_kernel(x_ref, lnw_ref, w_ref, ws_ref, o_ref, os_ref, xn_ref):
    @pl.when(pl.program_id(1) == 0)
    def _():
        x = x_ref[...]
        xn = x * lax.rsqrt(jnp.mean(x * x, axis=-1, keepdims=True) + EPS) * lnw_ref[...]
        xn_ref[...] = xn.astype(BF16)
        os_ref[...] = _dot(xn_ref[...], ws_ref[...])

    o_ref[...] = _dot(xn_ref[...], w_ref[...])


def _in_proj(x, lnw, w_main, w_small, tm, tn):
    n, d = x.shape
    nm = w_main.shape[1]
    return pl.pallas_call(
        _inproj_kernel,
        grid=(n // tm, nm // tn),
        in_specs=[
            pl.BlockSpec((tm, d), lambda i, j: (i, 0)),
            pl.BlockSpec((1, d), lambda i, j: (0, 0)),
            pl.BlockSpec((d, tn), lambda i, j: (0, j)),
            pl.BlockSpec((d, LANES), lambda i, j: (0, 0)),
        ],
        out_specs=[
            pl.BlockSpec((tm, tn), lambda i, j: (i, j)),
            pl.BlockSpec((tm, LANES), lambda i, j: (i, 0)),
        ],
        out_shape=[jax.ShapeDtypeStruct((n, nm), F32), jax.ShapeDtypeStruct((n, LANES), F32)],
        scratch_shapes=[pltpu.VMEM((tm, d), BF16)],
        compiler_params=pltpu.CompilerParams(
            dimension_semantics=("arbitrary", "arbitrary"), vmem_limit_bytes=VMEM_LIMIT),
        name="in_proj",
    )(x, lnw, w_main, w_small)


def _ssd_kernel(xbc_ref, z_ref, sm_ref, convw_ref, convb_ref, dtb_ref, alog_ref, dexp_ref, normw_ref, e_ref,
                y_ref, halo_ref, state_ref, yacc_ref):
    q = SSD_CHUNK
    p = SSD_HEAD_DIM
    ns = SSD_STATE
    gw = (SSD_HEADS // SSD_GROUPS) * p
    first = pl.program_id(1) == 0

    @pl.when(first)
    def _():
        state_ref[...] = jnp.zeros_like(state_ref)

    conv = _causal_conv(xbc_ref[...], halo_ref, convw_ref[...], first) + convb_ref[...]
    xbc = _silu(conv)
    xs = xbc[:, :SSD_D_INNER]
    bm = xbc[:, SSD_D_INNER:SSD_D_INNER + SSD_GROUPS * ns]
    cm = xbc[:, SSD_D_INNER + SSD_GROUPS * ns:]

    dt = _softplus(sm_ref[...] + dtb_ref[...])
    d_a = dt * (-jnp.exp(alog_ref[...]))
    row = lax.broadcasted_iota(jnp.int32, (q, q), 0)
    col = lax.broadcasted_iota(jnp.int32, (q, q), 1)
    causal = row >= col
    tri = jnp.where(causal, 1.0, 0.0).astype(BF16)
    a_cs = _dot_split_rhs(tri, d_a)
    a_cs_t = a_cs.T
    e_acs = jnp.exp(a_cs)
    dec_end = jnp.exp(a_cs[q - 1:q, :] - a_cs)
    ex = _dot_split_lhs(jnp.concatenate([dt, dt * dec_end, e_acs], axis=0), e_ref[...])
    dt_x, dtdec_x, eacs_x = ex[:q], ex[q:2 * q], ex[2 * q:]

    xdt_b = (xs * dt_x).astype(BF16)
    xdd_b = (xs * dtdec_x).astype(BF16)
    lane = lax.broadcasted_iota(jnp.int32, (q, LANES), 1)
    for g in range(SSD_GROUPS):
        bg = bm[:, g * ns:(g + 1) * ns]
        cg_b = cm[:, g * ns:(g + 1) * ns].astype(BF16)
        scores = _dot_nt(cg_b, bg.astype(BF16))
        s_g = state_ref[:, g * gw:(g + 1) * gw]
        yacc_ref[:, g * gw:(g + 1) * gw] = _dot(cg_b, s_g.astype(BF16)) * eacs_x[:, g * gw:(g + 1) * gw]
        new_g = _dot(bg.T.astype(BF16), xdd_b[:, g * gw:(g + 1) * gw])
        state_ref[:, g * gw:(g + 1) * gw] = s_g * eacs_x[q - 1:q, g * gw:(g + 1) * gw] + new_g
        for j in range(gw // LANES):
            h0 = (g * gw + j * LANES) // p
            c0 = g * gw + j * LANES
            ms = []
            for h in (h0, h0 + 1):
                seg = a_cs[:, h:h + 1] - a_cs_t[h:h + 1, :]
                ms.append((scores * jnp.exp(jnp.where(causal, seg, -jnp.inf))).astype(BF16))
            xp = xdt_b[:, c0:c0 + LANES]
            zero = jnp.zeros_like(xp)
            rhs = jnp.concatenate([jnp.where(lane < p, xp, zero), jnp.where(lane >= p, xp, zero)], axis=0)
            yacc_ref[:, c0:c0 + LANES] += _dot(jnp.concatenate(ms, axis=1), rhs)

    y = yacc_ref[...] + dexp_ref[...] * xs
    y = y * _silu(z_ref[...])
    y = y * lax.rsqrt(jnp.mean(y * y, axis=-1, keepdims=True) + EPS) * normw_ref[...]
    y_ref[...] = y.astype(BF16)


def _ssd(proj, small, convw, convb, dtb, alog, dexp, normw, expand, bsz, seqlen):
    q = SSD_CHUNK
    nc = seqlen // q
    n = bsz * seqlen
    rowmap = lambda b, c: b * nc + c
    const = lambda b, c: (0, 0)
    return pl.pallas_call(
        _ssd_kernel,
        grid=(bsz, nc),
        in_specs=[
            pl.BlockSpec((q, SSD_XBC), lambda b, c: (rowmap(b, c), XBC_BLK)),
            pl.BlockSpec((q, SSD_D_INNER), lambda b, c: (rowmap(b, c), ZS_BLK)),
            pl.BlockSpec((q, LANES), lambda b, c: (rowmap(b, c), 0)),
            pl.BlockSpec((CONV_K, SSD_XBC), const),
            pl.BlockSpec((1, SSD_XBC), const),
            pl.BlockSpec((1, LANES), const),
            pl.BlockSpec((1, LANES), const),
            pl.BlockSpec((1, SSD_D_INNER), const),
            pl.BlockSpec((1, SSD_D_INNER), const),
            pl.BlockSpec((LANES, SSD_D_INNER), const),
        ],
        out_specs=pl.BlockSpec((q, SSD_D_INNER), lambda b, c: (rowmap(b, c), 0)),
        out_shape=jax.ShapeDtypeStruct((n, SSD_D_INNER), BF16),
        scratch_shapes=[
            pltpu.VMEM((HALO, SSD_XBC), F32),
            pltpu.VMEM((SSD_STATE, SSD_D_INNER), F32),
            pltpu.VMEM((q, SSD_D_INNER), F32),
        ],
        compiler_params=pltpu.CompilerParams(
            dimension_semantics=("arbitrary", "arbitrary"), vmem_limit_bytes=VMEM_LIMIT),
        name="ssd",
    )(proj, proj, small, convw, convb, dtb, alog, dexp, normw, expand)


def _unit_lower_inverse(m, row, col):
    c = m.shape[0]
    diff = row ^ col
    t = jnp.where(row == col, 1.0, 0.0) - jnp.where((diff >> 1) == 0, m, 0.0)
    shift = 1
    while (1 << shift) < c:
        o = jnp.where((diff >> shift) == 1, m, 0.0)
        t_b = t.astype(BF16)
        x = _dot(t_b, o.astype(BF16))
        t = t - _dot(x.astype(BF16), t_b)
        shift += 1
    return t


def _gdn_kernel(qkv_ref, zg_ref, sm_ref, convw_ref, dtb_ref, alog_ref, normw_ref, eg_ref, eb_ref,
                o_ref, halo_ref, state_ref):
    c = GDN_CHUNK
    dh = GDN_HEAD_DIM
    first = pl.program_id(1) == 0

    @pl.when(first)
    def _():
        state_ref[...] = jnp.zeros_like(state_ref)

    qkv = _silu(_causal_conv(qkv_ref[...], halo_ref, convw_ref[...], first))

    sm = sm_ref[...]
    lane1 = lax.broadcasted_iota(jnp.int32, (c, LANES), 1)
    is_a = (lane1 >= SM_A) & (lane1 < SM_A + GDN_HEADS)
    log_decay = jnp.where(is_a, -jnp.exp(alog_ref[...]) * _softplus(sm + dtb_ref[...]), 0.0)
    beta = _sigmoid(sm)
    row = lax.broadcasted_iota(jnp.int32, (c, c), 0)
    col = lax.broadcasted_iota(jnp.int32, (c, c), 1)
    incl = row >= col
    strict = row > col
    tri = jnp.where(incl, 1.0, 0.0).astype(BF16)
    gc = _dot_split_rhs(tri, log_decay)
    gc_t = gc.T
    egc = jnp.exp(gc)
    erev = jnp.exp(gc[c - 1:c, :] - gc)
    ex = _dot_split_lhs(jnp.concatenate([egc, erev], axis=0), eg_ref[...])
    egc_x, erev_x = ex[:c], ex[c:]
    beta_x = _dot_split_lhs(beta, eb_ref[...])

    for h in range(GDN_HEADS):
        hs = slice(h * dh, (h + 1) * dh)
        qh = qkv[:, h * dh:(h + 1) * dh]
        kh = qkv[:, GDN_DIM + h * dh:GDN_DIM + (h + 1) * dh]
        vh = qkv[:, 2 * GDN_DIM + h * dh:2 * GDN_DIM + (h + 1) * dh]
        qh = qh * lax.rsqrt(jnp.sum(qh * qh, axis=-1, keepdims=True) + EPS) * (dh ** -0.5)
        kh = kh * lax.rsqrt(jnp.sum(kh * kh, axis=-1, keepdims=True) + EPS)
        bx, gx, rx = beta_x[:, hs], egc_x[:, hs], erev_x[:, hs]
        la = SM_A + h
        decay = jnp.exp(jnp.where(incl, gc[:, la:la + 1] - gc_t[la:la + 1, :], -jnp.inf))
        kb = kh * bx
        k_b = kh.astype(BF16)
        m = jnp.where(strict, _dot_nt(kb.astype(BF16), k_b) * decay, 0.0)
        t_inv = _unit_lower_inverse(m, row, col)
        uw = _dot(t_inv.astype(BF16), jnp.concatenate([vh * bx, kb * gx], axis=1).astype(BF16))
        u, w = uw[:, :dh], uw[:, dh:]
        a_qk = _dot_nt(qh.astype(BF16), k_b) * decay
        s = state_ref[h]
        s_b = s.astype(BF16)
        v_new = u - _dot(w.astype(BF16), s_b)
        v_new_b = v_new.astype(BF16)
        o = _dot((qh * gx).astype(BF16), s_b) + _dot(a_qk.astype(BF16), v_new_b)
        state_ref[h] = s * gx[c - 1:c, :] + _dot((kh * rx).T.astype(BF16), v_new_b)
        o = o * lax.rsqrt(jnp.mean(o * o, axis=-1, keepdims=True) + EPS) * normw_ref[...]
        o_ref[:, hs] = (o * _silu(zg_ref[:, hs])).astype(BF16)


def _gdn(proj, small, convw, dtb, alog, normw, expand_a, expand_b, bsz, seqlen):
    c = GDN_CHUNK
    nc = seqlen // c
    n = bsz * seqlen
    rowmap = lambda b, i: b * nc + i
    const = lambda b, i: (0, 0)
    return pl.pallas_call(
        _gdn_kernel,
        grid=(bsz, nc),
        in_specs=[
            pl.BlockSpec((c, 3 * GDN_DIM), lambda b, i: (rowmap(b, i), QKV_BLK)),
            pl.BlockSpec((c, GDN_DIM), lambda b, i: (rowmap(b, i), ZG_BLK)),
            pl.BlockSpec((c, LANES), lambda b, i: (rowmap(b, i), 0)),
            pl.BlockSpec((CONV_K, 3 * GDN_DIM), const),
            pl.BlockSpec((1, LANES), const),
            pl.BlockSpec((1, LANES), const),
            pl.BlockSpec((1, GDN_HEAD_DIM), const),
            pl.BlockSpec((LANES, GDN_DIM), const),
            pl.BlockSpec((LANES, GDN_DIM), const),
        ],
        out_specs=pl.BlockSpec((c, GDN_DIM), lambda b, i: (rowmap(b, i), 0)),
        out_shape=jax.ShapeDtypeStruct((n, GDN_DIM), BF16),
        scratch_shapes=[
            pltpu.VMEM((HALO, 3 * GDN_DIM), F32),
            pltpu.VMEM((GDN_HEADS, GDN_HEAD_DIM, GDN_HEAD_DIM), F32),
        ],
        compiler_params=pltpu.CompilerParams(
            dimension_semantics=("arbitrary", "arbitrary"), vmem_limit_bytes=VMEM_LIMIT),
        name="gdn",
    )(proj, proj, small, convw, dtb, alog, normw, expand_a, expand_b)


def _merge_kernel(y_ref, o_ref, gate_ref, h_ref, bm_ref, wsp_ref, wgp_ref, wout_ref, out_ref):
    d = h_ref.shape[1]
    y_ssd = _dot(y_ref[...], wsp_ref[...])
    y_gdn = _dot(o_ref[...], wgp_ref[...])
    gates = _sigmoid(gate_ref[...] + bm_ref[...])
    merged = gates[:, :d] * y_ssd + gates[:, d:] * y_gdn
    out_ref[...] = h_ref[...] + _dot(merged.astype(BF16), wout_ref[...])


def _merge(y, o, proj, h, b_merge, w_sp, w_gp, w_out, tm):
    n, d = h.shape
    const = lambda i: (0, 0)
    return pl.pallas_call(
        _merge_kernel,
        grid=(n // tm,),
        in_specs=[
            pl.BlockSpec((tm, SSD_D_INNER), lambda i: (i, 0)),
            pl.BlockSpec((tm, GDN_DIM), lambda i: (i, 0)),
            pl.BlockSpec((tm, 2 * d), lambda i: (i, GATE_BLK)),
            pl.BlockSpec((tm, d), lambda i: (i, 0)),
            pl.BlockSpec((1, 2 * d), const),
            pl.BlockSpec((SSD_D_INNER, d), const),
            pl.BlockSpec((GDN_DIM, d), const),
            pl.BlockSpec((d, d), const),
        ],
        out_specs=pl.BlockSpec((tm, d), lambda i: (i, 0)),
        out_shape=jax.ShapeDtypeStruct((n, d), F32),
        compiler_params=pltpu.CompilerParams(
            dimension_semantics=("arbitrary",), vmem_limit_bytes=VMEM_LIMIT),
        name="merge",
    )(y, o, proj, h, b_merge, w_sp, w_gp, w_out)


def _route(logits):
    lane = lax.broadcasted_iota(jnp.int32, logits.shape, 1)
    neg = -jnp.inf
    big = jnp.int32(LANES)
    is_g = (lane >= RT_G) & (lane < RT_G + MOE_GROUPS)
    gl = jnp.where(is_g, logits, neg)
    gmax = jnp.max(gl, axis=-1, keepdims=True)
    gsel = jnp.min(jnp.where(gl == gmax, lane, big), axis=-1, keepdims=True) - RT_G
    group_w = 1.0 / jnp.sum(jnp.exp(gl - gmax), axis=-1, keepdims=True)
    e_lo = RT_E + gsel * MOE_EPG
    in_grp = (lane >= e_lo) & (lane < e_lo + MOE_EPG)
    el = jnp.where(in_grp, logits, neg)
    m1 = jnp.max(el, axis=-1, keepdims=True)
    i1 = jnp.min(jnp.where(el == m1, lane, big), axis=-1, keepdims=True)
    el2 = jnp.where(lane == i1, neg, el)
    m2 = jnp.max(el2, axis=-1, keepdims=True)
    i2 = jnp.min(jnp.where(el2 == m2, lane, big), axis=-1, keepdims=True)
    z = jnp.sum(jnp.exp(el - m1), axis=-1, keepdims=True)
    p1 = 1.0 / z
    p2 = jnp.exp(m2 - m1) / z
    tot = p1 + p2
    return group_w * (jnp.where(lane == i1, p1 / tot, 0.0) + jnp.where(lane == i2, p2 / tot, 0.0))


def _moe_kernel(h_ref, lnw_ref, wr_ref, br_ref, wg_ref, wu_ref, wd_ref, fnw_ref, out_ref,
                xn_ref, comb_ref, acc_ref, *, final_norm):
    e = pl.program_id(1)

    @pl.when(e == 0)
    def _():
        x = h_ref[...]
        xn = x * lax.rsqrt(jnp.mean(x * x, axis=-1, keepdims=True) + EPS) * lnw_ref[...]
        xn_ref[...] = xn.astype(BF16)
        hi, lo = _split(xn)
        whi, wlo = wr_ref[0], wr_ref[1]
        logits = _dot(hi, whi) + _dot(lo, whi) + _dot(hi, wlo) + br_ref[...]
        comb_ref[...] = _route(logits)
        acc_ref[...] = jnp.zeros_like(acc_ref)

    xn = xn_ref[...]
    lane = lax.broadcasted_iota(jnp.int32, comb_ref.shape, 1)
    c_e = jnp.sum(jnp.where(lane == RT_E + e, comb_ref[...], 0.0), axis=-1, keepdims=True)
    hid = _silu(_dot(xn, wg_ref[0])) * _dot(xn, wu_ref[0]) * c_e
    acc_ref[...] += _dot(hid.astype(BF16), wd_ref[0])

    @pl.when(e == pl.num_programs(1) - 1)
    def _():
        y = h_ref[...] + acc_ref[...]
        if final_norm:
            y = y * lax.rsqrt(jnp.mean(y * y, axis=-1, keepdims=True) + EPS) * fnw_ref[...]
        out_ref[...] = y


def _moe(h, lnw, w_router, b_router, w_gate, w_up, w_down, fnw, tm, final_norm):
    n, d = h.shape
    ne, _, dff = w_gate.shape
    const2 = lambda i, e: (0, 0)
    return pl.pallas_call(
        functools.partial(_moe_kernel, final_norm=final_norm),
        grid=(n // tm, ne),
        in_specs=[
            pl.BlockSpec((tm, d), lambda i, e: (i, 0)),
            pl.BlockSpec((1, d), const2),
            pl.BlockSpec((2, d, LANES), lambda i, e: (0, 0, 0)),
            pl.BlockSpec((1, LANES), const2),
            pl.BlockSpec((1, d, dff), lambda i, e: (e, 0, 0)),
            pl.BlockSpec((1, d, dff), lambda i, e: (e, 0, 0)),
            pl.BlockSpec((1, dff, d), lambda i, e: (e, 0, 0)),
            pl.BlockSpec((1, d), const2),
        ],
        out_specs=pl.BlockSpec((tm, d), lambda i, e: (i, 0)),
        out_shape=jax.ShapeDtypeStruct((n, d), F32),
        scratch_shapes=[
            pltpu.VMEM((tm, d), BF16),
            pltpu.VMEM((tm, LANES), F32),
            pltpu.VMEM((tm, d), F32),
        ],
        compiler_params=pltpu.CompilerParams(
            dimension_semantics=("arbitrary", "arbitrary"), vmem_limit_bytes=VMEM_LIMIT),
        name="moe",
    )(h, lnw, w_router, b_router, w_gate, w_up, w_down, fnw)


def _lane_row(v, offset):
    return jnp.zeros((1, LANES), F32).at[0, offset:offset + v.shape[0]].set(v.astype(F32))


def _expansion(offset, heads, width):
    r = jnp.arange(LANES)[:, None]
    c = jnp.arange(heads * width)[None, :]
    return (r == offset + c // width).astype(BF16)


def _row_tile(n, want):
    t = min(n, want)
    while n % t:
        t //= 2
    return t


def kernel(x, ln1_w, w_in, ssd_conv_w, ssd_conv_b, ssd_dt_bias, ssd_a_log, ssd_d, ssd_norm_w, w_ssd_proj,
           gdn_conv_w, gdn_dt_bias, gdn_a_log, gdn_norm_w, w_gdn_proj, b_merge, w_out, ln2_w,
           moe_w_group, moe_b_group, moe_w_expert, moe_b_expert, moe_w_gate, moe_w_up, moe_w_down,
           final_norm_w):
    bsz, seqlen, d = x.shape
    n = bsz * seqlen
    depth = w_in.shape[0]
    h = x.reshape(n, d)

    o1 = SSD_D_INNER
    o2 = o1 + SSD_XBC
    o3 = o2 + SSD_HEADS
    o4 = o3 + 3 * GDN_DIM
    o5 = o4 + GDN_HEADS
    o6 = o5 + GDN_HEADS
    o7 = o6 + GDN_DIM

    e_ssd = _expansion(SM_DT, SSD_HEADS, SSD_HEAD_DIM)
    e_gdn_a = _expansion(SM_A, GDN_HEADS, GDN_HEAD_DIM)
    e_gdn_b = _expansion(SM_BETA, GDN_HEADS, GDN_HEAD_DIM)
    tm_proj = _row_tile(n, 1024)
    tm_merge = _row_tile(n, 512)
    tm_moe = _row_tile(n, 1024)

    for i in range(depth):
        wi = w_in[i]
        w_main = jnp.concatenate(
            [wi[:, o1:o2], wi[:, o3:o4], wi[:, :o1], wi[:, o7:], wi[:, o6:o7]], axis=1).astype(BF16)
        w_small = jnp.concatenate(
            [wi[:, o2:o3], wi[:, o4:o5], wi[:, o5:o6],
             jnp.zeros((d, LANES - SSD_HEADS - 2 * GDN_HEADS), F32)], axis=1).astype(BF16)
        proj, small = _in_proj(h, ln1_w[i][None, :], w_main, w_small, tm_proj, 1024)

        y = _ssd(proj, small, ssd_conv_w[i], ssd_conv_b[i][None, :],
                 _lane_row(ssd_dt_bias[i], SM_DT), _lane_row(ssd_a_log[i], SM_DT),
                 jnp.repeat(ssd_d[i], SSD_HEAD_DIM)[None, :], ssd_norm_w[i][None, :], e_ssd, bsz, seqlen)
        o = _gdn(proj, small, gdn_conv_w[i], _lane_row(gdn_dt_bias[i], SM_A), _lane_row(gdn_a_log[i], SM_A),
                 gdn_norm_w[i][None, :], e_gdn_a, e_gdn_b, bsz, seqlen)
        h = _merge(y, o, proj, h, b_merge[i][None, :], w_ssd_proj[i].astype(BF16), w_gdn_proj[i].astype(BF16),
                   w_out[i].astype(BF16), tm_merge)

        w_r = jnp.concatenate(
            [moe_w_group[i], moe_w_expert[i], jnp.zeros((d, LANES - MOE_GROUPS - MOE_EXPERTS), F32)], axis=1)
        w_r_hi = w_r.astype(BF16)
        w_r_lo = (w_r - w_r_hi.astype(F32)).astype(BF16)
        b_r = jnp.concatenate(
            [moe_b_group[i], moe_b_expert[i], jnp.zeros((LANES - MOE_GROUPS - MOE_EXPERTS,), F32)])[None, :]
        h = _moe(h, ln2_w[i][None, :], jnp.stack([w_r_hi, w_r_lo]), b_r,
                 moe_w_gate[i].astype(BF16), moe_w_up[i].astype(BF16), moe_w_down[i].astype(BF16),
                 final_norm_w[None, :], tm_moe, final_norm=(i == depth - 1))
    return h.reshape(bsz, seqlen, d)
```

```python
import functools

import jax
import jax.numpy as jnp
from jax import lax
from jax.experimental import pallas as pl
from jax.experimental.pallas import tpu as pltpu

F32 = jnp.float32
BF16 = jnp.bfloat16

EPS = 1e-6
CONV_K = 4
LANES = 128
HALO = 8

D_MODEL = 1024
SSD_D_INNER = 2 * D_MODEL
SSD_HEAD_DIM = 64
SSD_HEADS = SSD_D_INNER // SSD_HEAD_DIM
SSD_GROUPS = 4
SSD_STATE = 128
SSD_CHUNK = 128
SSD_XBC = SSD_D_INNER + 2 * SSD_GROUPS * SSD_STATE
GDN_HEADS = 8
GDN_HEAD_DIM = 128
GDN_DIM = GDN_HEADS * GDN_HEAD_DIM
GDN_CHUNK = 64
MOE_GROUPS = 4
MOE_EPG = 4
MOE_EXPERTS = MOE_GROUPS * MOE_EPG
MOE_D_FF = 512

MAIN_COLS = SSD_XBC + 3 * GDN_DIM + SSD_D_INNER + 2 * D_MODEL + GDN_DIM
XBC_BLK, QKV_BLK = 0, 1
ZS_BLK, GATE_BLK = 3, 4
ZG_BLK = 10
SM_DT, SM_A, SM_BETA = 0, SSD_HEADS, SSD_HEADS + GDN_HEADS
RT_G, RT_E = 0, MOE_GROUPS

VMEM_LIMIT = 56 * 1024 * 1024


def _dot(a, b):
    return jnp.dot(a, b, preferred_element_type=F32)


def _dot_nt(a, b):
    return lax.dot_general(a, b, (((1,), (1,)), ((), ())), preferred_element_type=F32)


def _split(x):
    hi = x.astype(BF16)
    lo = (x - hi.astype(F32)).astype(BF16)
    return hi, lo


def _dot_split_lhs(x, w):
    hi, lo = _split(x)
    return _dot(hi, w) + _dot(lo, w)


def _dot_split_rhs(w, x):
    hi, lo = _split(x)
    return _dot(w, hi) + _dot(w, lo)


def _sigmoid(x):
    return 1.0 / (1.0 + jnp.exp(-x))


def _silu(x):
    return x * _sigmoid(x)


def _softplus(x):
    return jnp.maximum(x, 0.0) + jnp.log1p(jnp.exp(-jnp.abs(x)))


def _causal_conv(cur, halo_ref, w, first):
    t = cur.shape[0]

    @pl.when(first)
    def _():
        halo_ref[...] = jnp.zeros_like(halo_ref)

    ext = jnp.concatenate([halo_ref[...], cur], axis=0)
    acc = None
    for k in range(CONV_K):
        off = HALO - (CONV_K - 1) + k
        term = ext[off:off + t, :] * w[k:k + 1, :]
        acc = term if acc is None else acc + term
    halo_ref[...] = cur[t - HALO:, :]
    return acc


def _inproj_kernel(x_ref, lnw_ref, w_ref, ws_ref, o_ref, os_ref, xn_ref):
    @pl.when(pl.program_id(1) == 0)
    def _():
        x = x_ref[...]
        xn = x * lax.rsqrt(jnp.mean(x * x, axis=-1, keepdims=True) + EPS) * lnw_ref[...]
        xn_ref[...] = xn.astype(BF16)
        os_ref[...] = _dot(xn_ref[...], ws_ref[...])

    o_ref[...] = _dot(xn_ref[...], w_ref[...])


def _in_proj(x, lnw, w_main, w_small, tm, tn):
    n, d = x.shape
    nm = w_main.shape[1]
    return pl.pallas_call(
        _inproj_kernel,
        grid=(n // tm, nm // tn),
        in_specs=[
            pl.BlockSpec((tm, d), lambda i, j: (i, 0)),
            pl.BlockSpec((1, d), lambda i, j: (0, 0)),
            pl.BlockSpec((d, tn), lambda i, j: (0, j)),
            pl.BlockSpec((d, LANES), lambda i, j: (0, 0)),
        ],
        out_specs=[
            pl.BlockSpec((tm, tn), lambda i, j: (i, j)),
            pl.BlockSpec((tm, LANES), lambda i, j: (i, 0)),
        ],
        out_shape=[jax.ShapeDtypeStruct((n, nm), F32), jax.ShapeDtypeStruct((n, LANES), F32)],
        scratch_shapes=[pltpu.VMEM((tm, d), BF16)],
        compiler_params=pltpu.CompilerParams(
            dimension_semantics=("arbitrary", "arbitrary"), vmem_limit_bytes=VMEM_LIMIT),
        name="in_proj",
    )(x, lnw, w_main, w_small)


def _ssd_kernel(xbc_ref, z_ref, sm_ref, convw_ref, convb_ref, dtb_ref, alog_ref, dexp_ref, normw_ref, e_ref,
                y_ref, halo_ref, state_ref, yacc_ref):
    q = SSD_CHUNK
    p = SSD_HEAD_DIM
    ns = SSD_STATE
    gw = (SSD_HEADS // SSD_GROUPS) * p
    first = pl.program_id(1) == 0

    @pl.when(first)
    def _():
        state_ref[...] = jnp.zeros_like(state_ref)

    conv = _causal_conv(xbc_ref[...], halo_ref, convw_ref[...], first) + convb_ref[...]
    xbc = _silu(conv)
    xs = xbc[:, :SSD_D_INNER]
    bm = xbc[:, SSD_D_INNER:SSD_D_INNER + SSD_GROUPS * ns]
    cm = xbc[:, SSD_D_INNER + SSD_GROUPS * ns:]

    dt = _softplus(sm_ref[...] + dtb_ref[...])
    d_a = dt * (-jnp.exp(alog_ref[...]))
    row = lax.broadcasted_iota(jnp.int32, (q, q), 0)
    col = lax.broadcasted_iota(jnp.int32, (q, q), 1)
    causal = row >= col
    tri = jnp.where(causal, 1.0, 0.0).astype(BF16)
    a_cs = _dot_split_rhs(tri, d_a)
    a_cs_t = a_cs.T
    e_acs = jnp.exp(a_cs)
    dec_end = jnp.exp(a_cs[q - 1:q, :] - a_cs)
    ex = _dot_split_lhs(jnp.concatenate([dt, dt * dec_end, e_acs], axis=0), e_ref[...])
    dt_x, dtdec_x, eacs_x = ex[:q], ex[q:2 * q], ex[2 * q:]

    xdt_b = (xs * dt_x).astype(BF16)
    xdd_b = (xs * dtdec_x).astype(BF16)
    lane = lax.broadcasted_iota(jnp.int32, (q, LANES), 1)
    for g in range(SSD_GROUPS):
        bg = bm[:, g * ns:(g + 1) * ns]
        cg_b = cm[:, g * ns:(g + 1) * ns].astype(BF16)
        scores = _dot_nt(cg_b, bg.astype(BF16))
        s_g = state_ref[:, g * gw:(g + 1) * gw]
        yacc_ref[:, g * gw:(g + 1) * gw] = _dot(cg_b, s_g.astype(BF16)) * eacs_x[:, g * gw:(g + 1) * gw]
        new_g = _dot(bg.T.astype(BF16), xdd_b[:, g * gw:(g + 1) * gw])
        state_ref[:, g * gw:(g + 1) * gw] = s_g * eacs_x[q - 1:q, g * gw:(g + 1) * gw] + new_g
        for j in range(gw // LANES):
            h0 = (g * gw + j * LANES) // p
            c0 = g * gw + j * LANES
            ms = []
            for h in (h0, h0 + 1):
                seg = a_cs[:, h:h + 1] - a_cs_t[h:h + 1, :]
                ms.append((scores * jnp.exp(jnp.where(causal, seg, -jnp.inf))).astype(BF16))
            xp = xdt_b[:, c0:c0 + LANES]
            zero = jnp.zeros_like(xp)
            rhs = jnp.concatenate([jnp.where(lane < p, xp, zero), jnp.where(lane >= p, xp, zero)], axis=0)
            yacc_ref[:, c0:c0 + LANES] += _dot(jnp.concatenate(ms, axis=1), rhs)

    y = yacc_ref[...] + dexp_ref[...] * xs
    y = y * _silu(z_ref[...])
    y = y * lax.rsqrt(jnp.mean(y * y, axis=-1, keepdims=True) + EPS) * normw_ref[...]
    y_ref[...] = y.astype(BF16)


def _ssd(proj, small, convw, convb, dtb, alog, dexp, normw, expand, bsz, seqlen):
    q = SSD_CHUNK
    nc = seqlen // q
    n = bsz * seqlen
    rowmap = lambda b, c: b * nc + c
    const = lambda b, c: (0, 0)
    return pl.pallas_call(
        _ssd_kernel,
        grid=(bsz, nc),
        in_specs=[
            pl.BlockSpec((q, SSD_XBC), lambda b, c: (rowmap(b, c), XBC_BLK)),
            pl.BlockSpec((q, SSD_D_INNER), lambda b, c: (rowmap(b, c), ZS_BLK)),
            pl.BlockSpec((q, LANES), lambda b, c: (rowmap(b, c), 0)),
            pl.BlockSpec((CONV_K, SSD_XBC), const),
            pl.BlockSpec((1, SSD_XBC), const),
            pl.BlockSpec((1, LANES), const),
            pl.BlockSpec((1, LANES), const),
            pl.BlockSpec((1, SSD_D_INNER), const),
            pl.BlockSpec((1, SSD_D_INNER), const),
            pl.BlockSpec((LANES, SSD_D_INNER), const),
        ],
        out_specs=pl.BlockSpec((q, SSD_D_INNER), lambda b, c: (rowmap(b, c), 0)),
        out_shape=jax.ShapeDtypeStruct((n, SSD_D_INNER), BF16),
        scratch_shapes=[
            pltpu.VMEM((HALO, SSD_XBC), F32),
            pltpu.VMEM((SSD_STATE, SSD_D_INNER), F32),
            pltpu.VMEM((q, SSD_D_INNER), F32),
        ],
        compiler_params=pltpu.CompilerParams(
            dimension_semantics=("arbitrary", "arbitrary"), vmem_limit_bytes=VMEM_LIMIT),
        name="ssd",
    )(proj, proj, small, convw, convb, dtb, alog, dexp, normw, expand)


GDN_PREP_ROWS = 2 * GDN_CHUNK
GDN_CHUNK_SHIFT = GDN_CHUNK.bit_length() - 1


def _unit_lower_inverse_many(ms, row, col):
    c = ms[0].shape[0]
    diff = row ^ col
    eye = jnp.where(row == col, 1.0, 0.0)
    ts = [eye - jnp.where((diff >> 1) == 0, m, 0.0) for m in ms]
    shift = 1
    while (1 << shift) < c:
        sel = (diff >> shift) == 1
        tbs = [t.astype(BF16) for t in ts]
        xs = [_dot(tb, jnp.where(sel, m, 0.0).astype(BF16)) for tb, m in zip(tbs, ms)]
        ts = [t - _dot(x.astype(BF16), tb) for t, x, tb in zip(ts, xs, tbs)]
        shift += 1
    return ts


def _gdn_prep_kernel(qkv_ref, sm_ref, convw_ref, dtb_ref, alog_ref, eg_ref, eb_ref,
                     u_ref, w_ref, qd_ref, aqk_ref, kdt_ref, gl_ref, halo_ref):
    c = GDN_CHUNK
    dh = GDN_HEAD_DIM
    t = GDN_PREP_ROWS
    ncb = t // c
    first = pl.program_id(1) == 0
    qkv = _silu(_causal_conv(qkv_ref[...], halo_ref, convw_ref[...], first))

    sm = sm_ref[...]
    lane1 = lax.broadcasted_iota(jnp.int32, (t, LANES), 1)
    is_a = (lane1 >= SM_A) & (lane1 < SM_A + GDN_HEADS)
    log_decay = jnp.where(is_a, -jnp.exp(alog_ref[...]) * _softplus(sm + dtb_ref[...]), 0.0)
    beta = _sigmoid(sm)
    row_t = lax.broadcasted_iota(jnp.int32, (t, t), 0)
    col_t = lax.broadcasted_iota(jnp.int32, (t, t), 1)
    same_chunk = (row_t >> GDN_CHUNK_SHIFT) == (col_t >> GDN_CHUNK_SHIFT)
    tri = jnp.where(same_chunk, jnp.where(row_t >= col_t, 1.0, 0.0), 0.0).astype(BF16)
    gc = _dot_split_rhs(tri, log_decay)
    gc_t = gc.T
    g_last = jnp.concatenate(
        [jnp.broadcast_to(gc[(i + 1) * c - 1:(i + 1) * c, :], (c, LANES)) for i in range(ncb)], axis=0)
    ex = _dot_split_lhs(jnp.concatenate([jnp.exp(gc), jnp.exp(g_last - gc)], axis=0), eg_ref[...])
    egc_x, erev_x = ex[:t], ex[t:]
    beta_x = _dot_split_lhs(beta, eb_ref[...])

    row = lax.broadcasted_iota(jnp.int32, (c, c), 0)
    col = lax.broadcasted_iota(jnp.int32, (c, c), 1)
    incl = row >= col
    strict = row > col

    qs, ks, kbs, rhs = [], [], [], []
    for h in range(GDN_HEADS):
        hs = slice(h * dh, (h + 1) * dh)
        qh = qkv[:, h * dh:(h + 1) * dh]
        kh = qkv[:, GDN_DIM + h * dh:GDN_DIM + (h + 1) * dh]
        vh = qkv[:, 2 * GDN_DIM + h * dh:2 * GDN_DIM + (h + 1) * dh]
        qh = qh * lax.rsqrt(jnp.sum(qh * qh, axis=-1, keepdims=True) + EPS) * (dh ** -0.5)
        kh = kh * lax.rsqrt(jnp.sum(kh * kh, axis=-1, keepdims=True) + EPS)
        kb = kh * beta_x[:, hs]
        qs.append(qh)
        ks.append(kh)
        kbs.append(kb)
        rhs.append(jnp.concatenate([vh * beta_x[:, hs], kb * egc_x[:, hs]], axis=1).astype(BF16))

    items = [(i, h) for i in range(ncb) for h in range(GDN_HEADS)]
    ms, aqks = [], []
    for i, h in items:
        rs = slice(i * c, (i + 1) * c)
        la = SM_A + h
        decay = jnp.exp(jnp.where(incl, gc[rs, la:la + 1] - gc_t[la:la + 1, rs], -jnp.inf))
        nt = _dot_nt(jnp.concatenate([kbs[h][rs], qs[h][rs]], axis=0).astype(BF16), ks[h][rs].astype(BF16))
        ms.append(jnp.where(strict, nt[:c] * decay, 0.0))
        aqks.append((nt[c:] * decay).astype(BF16))
    t_invs = _unit_lower_inverse_many(ms, row, col)
    uws = [_dot(t_inv.astype(BF16), rhs[h][i * c:(i + 1) * c]) for (i, h), t_inv in zip(items, t_invs)]

    def by_chunk(pieces):
        return jnp.concatenate(
            [jnp.concatenate(pieces[i * GDN_HEADS:(i + 1) * GDN_HEADS], axis=1) for i in range(ncb)], axis=0)

    u_ref[...] = by_chunk([uw[:, :dh] for uw in uws]).astype(BF16)
    w_ref[...] = by_chunk([uw[:, dh:] for uw in uws]).astype(BF16)
    aqk_ref[...] = by_chunk(aqks)
    qd_ref[...] = jnp.concatenate(
        [qs[h] * egc_x[:, h * dh:(h + 1) * dh] for h in range(GDN_HEADS)], axis=1).astype(BF16)
    for i in range(ncb):
        rs = slice(i * c, (i + 1) * c)
        kdt_ref[i] = jnp.concatenate(
            [(ks[h][rs] * erev_x[rs, h * dh:(h + 1) * dh]).T for h in range(GDN_HEADS)], axis=1).astype(BF16)
        gl_ref[i] = egc_x[(i + 1) * c - 1:(i + 1) * c, :]


def _gdn_scan_kernel(u_ref, w_ref, qd_ref, aqk_ref, kdt_ref, gl_ref, zg_ref, normw_ref, o_ref, state_ref):
    c = GDN_CHUNK
    dh = GDN_HEAD_DIM

    @pl.when(pl.program_id(0) == 0)
    def _():
        state_ref[...] = jnp.zeros_like(state_ref)

    zero = jnp.zeros((c, dh), BF16)
    for b in range(u_ref.shape[0]):
        outs = []
        for pr in range(GDN_HEADS // 2):
            heads = (2 * pr, 2 * pr + 1)
            ss, q_s, v_news = [], [], []
            for h in heads:
                hs = slice(h * dh, (h + 1) * dh)
                s = state_ref[b, h]
                p = _dot(jnp.concatenate([w_ref[b, :, hs], qd_ref[b, :, hs]], axis=0), s.astype(BF16))
                ss.append(s)
                q_s.append(p[c:])
                v_news.append((u_ref[b, :, hs].astype(F32) - p[:c]).astype(BF16))
            lhs = jnp.concatenate([aqk_ref[b, :, pr * dh:(pr + 1) * dh], kdt_ref[b, 0, :, pr * dh:(pr + 1) * dh]],
                                  axis=0)
            bd = jnp.concatenate([jnp.concatenate([v_news[0], zero], axis=1),
                                  jnp.concatenate([zero, v_news[1]], axis=1)], axis=0)
            r = _dot(lhs, bd)
            for j, h in enumerate(heads):
                hs = slice(h * dh, (h + 1) * dh)
                state_ref[b, h] = ss[j] * gl_ref[b, 0, :, hs] + r[c:, j * dh:(j + 1) * dh]
                o = q_s[j] + r[:c, j * dh:(j + 1) * dh]
                o = o * lax.rsqrt(jnp.mean(o * o, axis=-1, keepdims=True) + EPS) * normw_ref[...]
                outs.append(o * _silu(zg_ref[b, :, hs]))
        o_ref[b] = jnp.concatenate(outs, axis=1).astype(BF16)


def _gdn(proj, small, convw, dtb, alog, normw, expand_a, expand_b, bsz, seqlen):
    c = GDN_CHUNK
    t = GDN_PREP_ROWS
    nc = seqlen // c
    nb = seqlen // t
    n = bsz * seqlen
    hh = GDN_HEADS * c
    rowmap = lambda b, i: b * nb + i
    const = lambda b, i: (0, 0)
    params = pltpu.CompilerParams(dimension_semantics=("arbitrary", "arbitrary"), vmem_limit_bytes=VMEM_LIMIT)
    u, w, qd, aqk, kdt, gl = pl.pallas_call(
        _gdn_prep_kernel,
        grid=(bsz, nb),
        in_specs=[
            pl.BlockSpec((t, 3 * GDN_DIM), lambda b, i: (rowmap(b, i), QKV_BLK)),
            pl.BlockSpec((t, LANES), lambda b, i: (rowmap(b, i), 0)),
            pl.BlockSpec((CONV_K, 3 * GDN_DIM), const),
            pl.BlockSpec((1, LANES), const),
            pl.BlockSpec((1, LANES), const),
            pl.BlockSpec((LANES, GDN_DIM), const),
            pl.BlockSpec((LANES, GDN_DIM), const),
        ],
        out_specs=[
            pl.BlockSpec((t, GDN_DIM), lambda b, i: (rowmap(b, i), 0)),
            pl.BlockSpec((t, GDN_DIM), lambda b, i: (rowmap(b, i), 0)),
            pl.BlockSpec((t, GDN_DIM), lambda b, i: (rowmap(b, i), 0)),
            pl.BlockSpec((t, hh), lambda b, i: (rowmap(b, i), 0)),
            pl.BlockSpec((t // c, GDN_HEAD_DIM, hh), lambda b, i: (rowmap(b, i), 0, 0)),
            pl.BlockSpec((t // c, 1, GDN_DIM), lambda b, i: (rowmap(b, i), 0, 0)),
        ],
        out_shape=[
            jax.ShapeDtypeStruct((n, GDN_DIM), BF16),
            jax.ShapeDtypeStruct((n, GDN_DIM), BF16),
            jax.ShapeDtypeStruct((n, GDN_DIM), BF16),
            jax.ShapeDtypeStruct((n, hh), BF16),
            jax.ShapeDtypeStruct((n // c, GDN_HEAD_DIM, hh), BF16),
            jax.ShapeDtypeStruct((n // c, 1, GDN_DIM), F32),
        ],
        scratch_shapes=[pltpu.VMEM((HALO, 3 * GDN_DIM), F32)],
        compiler_params=params,
        name="gdn_prep",
    )(proj, small, convw, dtb, alog, expand_a, expand_b)

    seq3 = lambda i: (0, i, 0)
    seq4 = lambda i: (0, i, 0, 0)
    o = pl.pallas_call(
        _gdn_scan_kernel,
        grid=(nc,),
        in_specs=[
            pl.BlockSpec((bsz, c, GDN_DIM), seq3),
            pl.BlockSpec((bsz, c, GDN_DIM), seq3),
            pl.BlockSpec((bsz, c, GDN_DIM), seq3),
            pl.BlockSpec((bsz, c, hh), seq3),
            pl.BlockSpec((bsz, 1, GDN_HEAD_DIM, hh), seq4),
            pl.BlockSpec((bsz, 1, 1, GDN_DIM), seq4),
            pl.BlockSpec((bsz, c, GDN_DIM), lambda i: (0, i, ZG_BLK)),
            pl.BlockSpec((1, GDN_HEAD_DIM), lambda i: (0, 0)),
        ],
        out_specs=pl.BlockSpec((bsz, c, GDN_DIM), seq3),
        out_shape=jax.ShapeDtypeStruct((bsz, seqlen, GDN_DIM), BF16),
        scratch_shapes=[pltpu.VMEM((bsz, GDN_HEADS, GDN_HEAD_DIM, GDN_HEAD_DIM), F32)],
        compiler_params=pltpu.CompilerParams(dimension_semantics=("arbitrary",), vmem_limit_bytes=VMEM_LIMIT),
        name="gdn_scan",
    )(u.reshape(bsz, seqlen, GDN_DIM), w.reshape(bsz, seqlen, GDN_DIM), qd.reshape(bsz, seqlen, GDN_DIM),
      aqk.reshape(bsz, seqlen, hh), kdt.reshape(bsz, nc, GDN_HEAD_DIM, hh), gl.reshape(bsz, nc, 1, GDN_DIM),
      proj.reshape(bsz, seqlen, MAIN_COLS), normw)
    return o.reshape(n, GDN_DIM)


def _merge_kernel(y_ref, o_ref, gate_ref, h_ref, bm_ref, wsp_ref, wgp_ref, wout_ref, out_ref):
    d = h_ref.shape[1]
    y_ssd = _dot(y_ref[...], wsp_ref[...])
    y_gdn = _dot(o_ref[...], wgp_ref[...])
    gates = _sigmoid(gate_ref[...] + bm_ref[...])
    merged = gates[:, :d] * y_ssd + gates[:, d:] * y_gdn
    out_ref[...] = h_ref[...] + _dot(merged.astype(BF16), wout_ref[...])


def _merge(y, o, proj, h, b_merge, w_sp, w_gp, w_out, tm):
    n, d = h.shape
    const = lambda i: (0, 0)
    return pl.pallas_call(
        _merge_kernel,
        grid=(n // tm,),
        in_specs=[
            pl.BlockSpec((tm, SSD_D_INNER), lambda i: (i, 0)),
            pl.BlockSpec((tm, GDN_DIM), lambda i: (i, 0)),
            pl.BlockSpec((tm, 2 * d), lambda i: (i, GATE_BLK)),
            pl.BlockSpec((tm, d), lambda i: (i, 0)),
            pl.BlockSpec((1, 2 * d), const),
            pl.BlockSpec((SSD_D_INNER, d), const),
            pl.BlockSpec((GDN_DIM, d), const),
            pl.BlockSpec((d, d), const),
        ],
        out_specs=pl.BlockSpec((tm, d), lambda i: (i, 0)),
        out_shape=jax.ShapeDtypeStruct((n, d), F32),
        compiler_params=pltpu.CompilerParams(
            dimension_semantics=("arbitrary",), vmem_limit_bytes=VMEM_LIMIT),
        name="merge",
    )(y, o, proj, h, b_merge, w_sp, w_gp, w_out)


def _route(logits):
    lane = lax.broadcasted_iota(jnp.int32, logits.shape, 1)
    neg = -jnp.inf
    big = jnp.int32(LANES)
    is_g = (lane >= RT_G) & (lane < RT_G + MOE_GROUPS)
    gl = jnp.where(is_g, logits, neg)
    gmax = jnp.max(gl, axis=-1, keepdims=True)
    gsel = jnp.min(jnp.where(gl == gmax, lane, big), axis=-1, keepdims=True) - RT_G
    group_w = 1.0 / jnp.sum(jnp.exp(gl - gmax), axis=-1, keepdims=True)
    e_lo = RT_E + gsel * MOE_EPG
    in_grp = (lane >= e_lo) & (lane < e_lo + MOE_EPG)
    el = jnp.where(in_grp, logits, neg)
    m1 = jnp.max(el, axis=-1, keepdims=True)
    i1 = jnp.min(jnp.where(el == m1, lane, big), axis=-1, keepdims=True)
    el2 = jnp.where(lane == i1, neg, el)
    m2 = jnp.max(el2, axis=-1, keepdims=True)
    i2 = jnp.min(jnp.where(el2 == m2, lane, big), axis=-1, keepdims=True)
    z = jnp.sum(jnp.exp(el - m1), axis=-1, keepdims=True)
    p1 = 1.0 / z
    p2 = jnp.exp(m2 - m1) / z
    tot = p1 + p2
    return group_w * (jnp.where(lane == i1, p1 / tot, 0.0) + jnp.where(lane == i2, p2 / tot, 0.0))


def _moe_kernel(h_ref, lnw_ref, wr_ref, br_ref, wg_ref, wu_ref, wd_ref, fnw_ref, out_ref,
                xn_ref, comb_ref, acc_ref, *, final_norm):
    e = pl.program_id(1)

    @pl.when(e == 0)
    def _():
        x = h_ref[...]
        xn = x * lax.rsqrt(jnp.mean(x * x, axis=-1, keepdims=True) + EPS) * lnw_ref[...]
        xn_ref[...] = xn.astype(BF16)
        hi, lo = _split(xn)
        whi, wlo = wr_ref[0], wr_ref[1]
        logits = _dot(hi, whi) + _dot(lo, whi) + _dot(hi, wlo) + br_ref[...]
        comb_ref[...] = _route(logits)
        acc_ref[...] = jnp.zeros_like(acc_ref)

    xn = xn_ref[...]
    lane = lax.broadcasted_iota(jnp.int32, comb_ref.shape, 1)
    c_e = jnp.sum(jnp.where(lane == RT_E + e, comb_ref[...], 0.0), axis=-1, keepdims=True)
    hid = _silu(_dot(xn, wg_ref[0])) * _dot(xn, wu_ref[0]) * c_e
    acc_ref[...] += _dot(hid.astype(BF16), wd_ref[0])

    @pl.when(e == pl.num_programs(1) - 1)
    def _():
        y = h_ref[...] + acc_ref[...]
        if final_norm:
            y = y * lax.rsqrt(jnp.mean(y * y, axis=-1, keepdims=True) + EPS) * fnw_ref[...]
        out_ref[...] = y


def _moe(h, lnw, w_router, b_router, w_gate, w_up, w_down, fnw, tm, final_norm):
    n, d = h.shape
    ne, _, dff = w_gate.shape
    const2 = lambda i, e: (0, 0)
    return pl.pallas_call(
        functools.partial(_moe_kernel, final_norm=final_norm),
        grid=(n // tm, ne),
        in_specs=[
            pl.BlockSpec((tm, d), lambda i, e: (i, 0)),
            pl.BlockSpec((1, d), const2),
            pl.BlockSpec((2, d, LANES), lambda i, e: (0, 0, 0)),
            pl.BlockSpec((1, LANES), const2),
            pl.BlockSpec((1, d, dff), lambda i, e: (e, 0, 0)),
            pl.BlockSpec((1, d, dff), lambda i, e: (e, 0, 0)),
            pl.BlockSpec((1, dff, d), lambda i, e: (e, 0, 0)),
            pl.BlockSpec((1, d), const2),
        ],
        out_specs=pl.BlockSpec((tm, d), lambda i, e: (i, 0)),
        out_shape=jax.ShapeDtypeStruct((n, d), F32),
        scratch_shapes=[
            pltpu.VMEM((tm, d), BF16),
            pltpu.VMEM((tm, LANES), F32),
            pltpu.VMEM((tm, d), F32),
        ],
        compiler_params=pltpu.CompilerParams(
            dimension_semantics=("arbitrary", "arbitrary"), vmem_limit_bytes=VMEM_LIMIT),
        name="moe",
    )(h, lnw, w_router, b_router, w_gate, w_up, w_down, fnw)


def _lane_row(v, offset):
    return jnp.zeros((1, LANES), F32).at[0, offset:offset + v.shape[0]].set(v.astype(F32))


def _expansion(offset, heads, width):
    r = jnp.arange(LANES)[:, None]
    c = jnp.arange(heads * width)[None, :]
    return (r == offset + c // width).astype(BF16)


def _row_tile(n, want):
    t = min(n, want)
    while n % t:
        t //= 2
    return t


def kernel(x, ln1_w, w_in, ssd_conv_w, ssd_conv_b, ssd_dt_bias, ssd_a_log, ssd_d, ssd_norm_w, w_ssd_proj,
           gdn_conv_w, gdn_dt_bias, gdn_a_log, gdn_norm_w, w_gdn_proj, b_merge, w_out, ln2_w,
           moe_w_group, moe_b_group, moe_w_expert, moe_b_expert, moe_w_gate, moe_w_up, moe_w_down,
           final_norm_w):
    bsz, seqlen, d = x.shape
    n = bsz * seqlen
    depth = w_in.shape[0]
    h = x.reshape(n, d)

    o1 = SSD_D_INNER
    o2 = o1 + SSD_XBC
    o3 = o2 + SSD_HEADS
    o4 = o3 + 3 * GDN_DIM
    o5 = o4 + GDN_HEADS
    o6 = o5 + GDN_HEADS
    o7 = o6 + GDN_DIM

    e_ssd = _expansion(SM_DT, SSD_HEADS, SSD_HEAD_DIM)
    e_gdn_a = _expansion(SM_A, GDN_HEADS, GDN_HEAD_DIM)
    e_gdn_b = _expansion(SM_BETA, GDN_HEADS, GDN_HEAD_DIM)
    tm_proj = _row_tile(n, 1024)
    tm_merge = _row_tile(n, 512)
    tm_moe = _row_tile(n, 1024)

    for i in range(depth):
        wi = w_in[i]
        w_main = jnp.concatenate(
            [wi[:, o1:o2], wi[:, o3:o4], wi[:, :o1], wi[:, o7:], wi[:, o6:o7]], axis=1).astype(BF16)
        w_small = jnp.concatenate(
            [wi[:, o2:o3], wi[:, o4:o5], wi[:, o5:o6],
             jnp.zeros((d, LANES - SSD_HEADS - 2 * GDN_HEADS), F32)], axis=1).astype(BF16)
        proj, small = _in_proj(h, ln1_w[i][None, :], w_main, w_small, tm_proj, 1024)

        y = _ssd(proj, small, ssd_conv_w[i], ssd_conv_b[i][None, :],
                 _lane_row(ssd_dt_bias[i], SM_DT), _lane_row(ssd_a_log[i], SM_DT),
                 jnp.repeat(ssd_d[i], SSD_HEAD_DIM)[None, :], ssd_norm_w[i][None, :], e_ssd, bsz, seqlen)
        o = _gdn(proj, small, gdn_conv_w[i], _lane_row(gdn_dt_bias[i], SM_A), _lane_row(gdn_a_log[i], SM_A),
                 gdn_norm_w[i][None, :], e_gdn_a, e_gdn_b, bsz, seqlen)
        h = _merge(y, o, proj, h, b_merge[i][None, :], w_ssd_proj[i].astype(BF16), w_gdn_proj[i].astype(BF16),
                   w_out[i].astype(BF16), tm_merge)

        w_r = jnp.concatenate(
            [moe_w_group[i], moe_w_expert[i], jnp.zeros((d, LANES - MOE_GROUPS - MOE_EXPERTS), F32)], axis=1)
        w_r_hi = w_r.astype(BF16)
        w_r_lo = (w_r - w_r_hi.astype(F32)).astype(BF16)
        b_r = jnp.concatenate(
            [moe_b_group[i], moe_b_expert[i], jnp.zeros((LANES - MOE_GROUPS - MOE_EXPERTS,), F32)])[None, :]
        h = _moe(h, ln2_w[i][None, :], jnp.stack([w_r_hi, w_r_lo]), b_r,
                 moe_w_gate[i].astype(BF16), moe_w_up[i].astype(BF16), moe_w_down[i].astype(BF16),
                 final_norm_w[None, :], tm_moe, final_norm=(i == depth - 1))
    return h.reshape(bsz, seqlen, d)
```

```python
import functools

import jax
import jax.numpy as jnp
from jax import lax
from jax.experimental import pallas as pl
from jax.experimental.pallas import tpu as pltpu

F32 = jnp.float32
BF16 = jnp.bfloat16

EPS = 1e-6
CONV_K = 4
LANES = 128
HALO = 8

D_MODEL = 1024
SSD_D_INNER = 2 * D_MODEL
SSD_HEAD_DIM = 64
SSD_HEADS = SSD_D_INNER // SSD_HEAD_DIM
SSD_GROUPS = 4
SSD_STATE = 128
SSD_CHUNK = 128
SSD_XBC = SSD_D_INNER + 2 * SSD_GROUPS * SSD_STATE
GDN_HEADS = 8
GDN_HEAD_DIM = 128
GDN_DIM = GDN_HEADS * GDN_HEAD_DIM
GDN_CHUNK = 64
MOE_GROUPS = 4
MOE_EPG = 4
MOE_EXPERTS = MOE_GROUPS * MOE_EPG
MOE_D_FF = 512

MAIN_COLS = SSD_XBC + 3 * GDN_DIM + SSD_D_INNER + 2 * D_MODEL + GDN_DIM
XBC_BLK, QKV_BLK = 0, 1
ZS_BLK, GATE_BLK = 3, 4
ZG_BLK = 10
SM_DT, SM_A, SM_BETA = 0, SSD_HEADS, SSD_HEADS + GDN_HEADS
RT_G, RT_E = 0, MOE_GROUPS

VMEM_LIMIT = 56 * 1024 * 1024


def _dot(a, b):
    return jnp.dot(a, b, preferred_element_type=F32)


def _dot_nt(a, b):
    return lax.dot_general(a, b, (((1,), (1,)), ((), ())), preferred_element_type=F32)


def _split(x):
    hi = x.astype(BF16)
    lo = (x - hi.astype(F32)).astype(BF16)
    return hi, lo


def _dot_split_lhs(x, w):
    hi, lo = _split(x)
    return _dot(hi, w) + _dot(lo, w)


def _dot_split_rhs(w, x):
    hi, lo = _split(x)
    return _dot(w, hi) + _dot(w, lo)


def _sigmoid(x):
    return 1.0 / (1.0 + jnp.exp(-x))


def _silu(x):
    return x * _sigmoid(x)


def _softplus(x):
    return jnp.maximum(x, 0.0) + jnp.log1p(jnp.exp(-jnp.abs(x)))


def _causal_conv(cur, halo_ref, w, first):
    t = cur.shape[0]

    @pl.when(first)
    def _():
        halo_ref[...] = jnp.zeros_like(halo_ref)

    ext = jnp.concatenate([halo_ref[...], cur], axis=0)
    acc = None
    for k in range(CONV_K):
        off = HALO - (CONV_K - 1) + k
        term = ext[off:off + t, :] * w[k:k + 1, :]
        acc = term if acc is None else acc + term
    halo_ref[...] = cur[t - HALO:, :]
    return acc


def _inproj_kernel(x_ref, lnw_ref, w_ref, ws_ref, o_ref, os_ref, xn_ref):
    @pl.when(pl.program_id(1) == 0)
    def _():
        x = x_ref[...]
        xn = x * lax.rsqrt(jnp.mean(x * x, axis=-1, keepdims=True) + EPS) * lnw_ref[...]
        xn_ref[...] = xn.astype(BF16)
        os_ref[...] = _dot(xn_ref[...], ws_ref[...])

    o_ref[...] = _dot(xn_ref[...], w_ref[...])


def _in_proj(x, lnw, w_main, w_small, tm, tn):
    n, d = x.shape
    nm = w_main.shape[1]
    return pl.pallas_call(
        _inproj_kernel,
        grid=(n // tm, nm // tn),
        in_specs=[
            pl.BlockSpec((tm, d), lambda i, j: (i, 0)),
            pl.BlockSpec((1, d), lambda i, j: (0, 0)),
            pl.BlockSpec((d, tn), lambda i, j: (0, j)),
            pl.BlockSpec((d, LANES), lambda i, j: (0, 0)),
        ],
        out_specs=[
            pl.BlockSpec((tm, tn), lambda i, j: (i, j)),
            pl.BlockSpec((tm, LANES), lambda i, j: (i, 0)),
        ],
        out_shape=[jax.ShapeDtypeStruct((n, nm), F32), jax.ShapeDtypeStruct((n, LANES), F32)],
        scratch_shapes=[pltpu.VMEM((tm, d), BF16)],
        compiler_params=pltpu.CompilerParams(
            dimension_semantics=("arbitrary", "arbitrary"), vmem_limit_bytes=VMEM_LIMIT),
        name="in_proj",
    )(x, lnw, w_main, w_small)


def _ssd_kernel(xbc_ref, z_ref, sm_ref, convw_ref, convb_ref, dtb_ref, alog_ref, dexp_ref, normw_ref, e_ref,
                y_ref, halo_ref, state_ref, yacc_ref):
    q = SSD_CHUNK
    p = SSD_HEAD_DIM
    ns = SSD_STATE
    gw = (SSD_HEADS // SSD_GROUPS) * p
    first = pl.program_id(1) == 0

    @pl.when(first)
    def _():
        state_ref[...] = jnp.zeros_like(state_ref)

    conv = _causal_conv(xbc_ref[...], halo_ref, convw_ref[...], first) + convb_ref[...]
    xbc = _silu(conv)
    xs = xbc[:, :SSD_D_INNER]
    bm = xbc[:, SSD_D_INNER:SSD_D_INNER + SSD_GROUPS * ns]
    cm = xbc[:, SSD_D_INNER + SSD_GROUPS * ns:]

    dt = _softplus(sm_ref[...] + dtb_ref[...])
    d_a = dt * (-jnp.exp(alog_ref[...]))
    row = lax.broadcasted_iota(jnp.int32, (q, q), 0)
    col = lax.broadcasted_iota(jnp.int32, (q, q), 1)
    causal = row >= col
    tri = jnp.where(causal, 1.0, 0.0).astype(BF16)
    a_cs = _dot_split_rhs(tri, d_a)
    a_cs_t = a_cs.T
    e_acs = jnp.exp(a_cs)
    dec_end = jnp.exp(a_cs[q - 1:q, :] - a_cs)
    ex = _dot_split_lhs(jnp.concatenate([dt, dt * dec_end, e_acs], axis=0), e_ref[...])
    dt_x, dtdec_x, eacs_x = ex[:q], ex[q:2 * q], ex[2 * q:]

    xdt_b = (xs * dt_x).astype(BF16)
    xdd_b = (xs * dtdec_x).astype(BF16)
    lane = lax.broadcasted_iota(jnp.int32, (q, LANES), 1)
    for g in range(SSD_GROUPS):
        bg = bm[:, g * ns:(g + 1) * ns]
        cg_b = cm[:, g * ns:(g + 1) * ns].astype(BF16)
        scores = _dot_nt(cg_b, bg.astype(BF16))
        s_g = state_ref[:, g * gw:(g + 1) * gw]
        yacc_ref[:, g * gw:(g + 1) * gw] = _dot(cg_b, s_g.astype(BF16)) * eacs_x[:, g * gw:(g + 1) * gw]
        new_g = _dot(bg.T.astype(BF16), xdd_b[:, g * gw:(g + 1) * gw])
        state_ref[:, g * gw:(g + 1) * gw] = s_g * eacs_x[q - 1:q, g * gw:(g + 1) * gw] + new_g
        for j in range(gw // LANES):
            h0 = (g * gw + j * LANES) // p
            c0 = g * gw + j * LANES
            ms = []
            for h in (h0, h0 + 1):
                seg = a_cs[:, h:h + 1] - a_cs_t[h:h + 1, :]
                ms.append((scores * jnp.exp(jnp.where(causal, seg, -jnp.inf))).astype(BF16))
            xp = xdt_b[:, c0:c0 + LANES]
            zero = jnp.zeros_like(xp)
            rhs = jnp.concatenate([jnp.where(lane < p, xp, zero), jnp.where(lane >= p, xp, zero)], axis=0)
            yacc_ref[:, c0:c0 + LANES] += _dot(jnp.concatenate(ms, axis=1), rhs)

    y = yacc_ref[...] + dexp_ref[...] * xs
    y = y * _silu(z_ref[...])
    y = y * lax.rsqrt(jnp.mean(y * y, axis=-1, keepdims=True) + EPS) * normw_ref[...]
    y_ref[...] = y.astype(BF16)


def _ssd(proj, small, convw, convb, dtb, alog, dexp, normw, expand, bsz, seqlen):
    q = SSD_CHUNK
    nc = seqlen // q
    n = bsz * seqlen
    rowmap = lambda b, c: b * nc + c
    const = lambda b, c: (0, 0)
    return pl.pallas_call(
        _ssd_kernel,
        grid=(bsz, nc),
        in_specs=[
            pl.BlockSpec((q, SSD_XBC), lambda b, c: (rowmap(b, c), XBC_BLK)),
            pl.BlockSpec((q, SSD_D_INNER), lambda b, c: (rowmap(b, c), ZS_BLK)),
            pl.BlockSpec((q, LANES), lambda b, c: (rowmap(b, c), 0)),
            pl.BlockSpec((CONV_K, SSD_XBC), const),
            pl.BlockSpec((1, SSD_XBC), const),
            pl.BlockSpec((1, LANES), const),
            pl.BlockSpec((1, LANES), const),
            pl.BlockSpec((1, SSD_D_INNER), const),
            pl.BlockSpec((1, SSD_D_INNER), const),
            pl.BlockSpec((LANES, SSD_D_INNER), const),
        ],
        out_specs=pl.BlockSpec((q, SSD_D_INNER), lambda b, c: (rowmap(b, c), 0)),
        out_shape=jax.ShapeDtypeStruct((n, SSD_D_INNER), BF16),
        scratch_shapes=[
            pltpu.VMEM((HALO, SSD_XBC), F32),
            pltpu.VMEM((SSD_STATE, SSD_D_INNER), F32),
            pltpu.VMEM((q, SSD_D_INNER), F32),
        ],
        compiler_params=pltpu.CompilerParams(
            dimension_semantics=("arbitrary", "arbitrary"), vmem_limit_bytes=VMEM_LIMIT),
        name="ssd",
    )(proj, proj, small, convw, convb, dtb, alog, dexp, normw, expand)


GDN_PREP_ROWS = 2 * GDN_CHUNK
GDN_CHUNK_SHIFT = GDN_CHUNK.bit_length() - 1


def _unit_lower_inverse_many(ms, row, col):
    c = ms[0].shape[0]
    diff = row ^ col
    eye = jnp.where(row == col, 1.0, 0.0)
    ts = [eye - jnp.where((diff >> 1) == 0, m, 0.0) for m in ms]
    shift = 1
    while (1 << shift) < c:
        sel = (diff >> shift) == 1
        tbs = [t.astype(BF16) for t in ts]
        xs = [_dot(tb, jnp.where(sel, m, 0.0).astype(BF16)) for tb, m in zip(tbs, ms)]
        ts = [t - _dot(x.astype(BF16), tb) for t, x, tb in zip(ts, xs, tbs)]
        shift += 1
    return ts


def _gdn_prep_kernel(qkv_ref, sm_ref, convw_ref, dtb_ref, alog_ref, eg_ref, eb_ref,
                     u_ref, w_ref, qd_ref, aqk_ref, kdt_ref, gl_ref, halo_ref):
    c = GDN_CHUNK
    dh = GDN_HEAD_DIM
    t = GDN_PREP_ROWS
    ncb = t // c
    first = pl.program_id(1) == 0
    qkv = _silu(_causal_conv(qkv_ref[...], halo_ref, convw_ref[...], first))

    sm = sm_ref[...]
    lane1 = lax.broadcasted_iota(jnp.int32, (t, LANES), 1)
    is_a = (lane1 >= SM_A) & (lane1 < SM_A + GDN_HEADS)
    log_decay = jnp.where(is_a, -jnp.exp(alog_ref[...]) * _softplus(sm + dtb_ref[...]), 0.0)
    beta = _sigmoid(sm)
    row_t = lax.broadcasted_iota(jnp.int32, (t, t), 0)
    col_t = lax.broadcasted_iota(jnp.int32, (t, t), 1)
    same_chunk = (row_t >> GDN_CHUNK_SHIFT) == (col_t >> GDN_CHUNK_SHIFT)
    tri = jnp.where(same_chunk, jnp.where(row_t >= col_t, 1.0, 0.0), 0.0).astype(BF16)
    gc = _dot_split_rhs(tri, log_decay)
    gc_t = gc.T
    g_last = jnp.concatenate(
        [jnp.broadcast_to(gc[(i + 1) * c - 1:(i + 1) * c, :], (c, LANES)) for i in range(ncb)], axis=0)
    ex = _dot_split_lhs(jnp.concatenate([jnp.exp(gc), jnp.exp(g_last - gc)], axis=0), eg_ref[...])
    egc_x, erev_x = ex[:t], ex[t:]
    beta_x = _dot_split_lhs(beta, eb_ref[...])

    row = lax.broadcasted_iota(jnp.int32, (c, c), 0)
    col = lax.broadcasted_iota(jnp.int32, (c, c), 1)
    incl = row >= col
    strict = row > col

    qs, ks, kbs, rhs = [], [], [], []
    for h in range(GDN_HEADS):
        hs = slice(h * dh, (h + 1) * dh)
        qh = qkv[:, h * dh:(h + 1) * dh]
        kh = qkv[:, GDN_DIM + h * dh:GDN_DIM + (h + 1) * dh]
        vh = qkv[:, 2 * GDN_DIM + h * dh:2 * GDN_DIM + (h + 1) * dh]
        qh = qh * lax.rsqrt(jnp.sum(qh * qh, axis=-1, keepdims=True) + EPS) * (dh ** -0.5)
        kh = kh * lax.rsqrt(jnp.sum(kh * kh, axis=-1, keepdims=True) + EPS)
        kb = kh * beta_x[:, hs]
        qs.append(qh)
        ks.append(kh)
        kbs.append(kb)
        rhs.append(jnp.concatenate([vh * beta_x[:, hs], kb * egc_x[:, hs]], axis=1).astype(BF16))

    items = [(i, h) for i in range(ncb) for h in range(GDN_HEADS)]
    ms, aqks = [], []
    for i, h in items:
        rs = slice(i * c, (i + 1) * c)
        la = SM_A + h
        decay = jnp.exp(jnp.where(incl, gc[rs, la:la + 1] - gc_t[la:la + 1, rs], -jnp.inf))
        nt = _dot_nt(jnp.concatenate([kbs[h][rs], qs[h][rs]], axis=0).astype(BF16), ks[h][rs].astype(BF16))
        ms.append(jnp.where(strict, nt[:c] * decay, 0.0))
        aqks.append((nt[c:] * decay).astype(BF16))
    t_invs = _unit_lower_inverse_many(ms, row, col)
    uws = [_dot(t_inv.astype(BF16), rhs[h][i * c:(i + 1) * c]) for (i, h), t_inv in zip(items, t_invs)]

    def by_chunk(pieces):
        return jnp.concatenate(
            [jnp.concatenate(pieces[i * GDN_HEADS:(i + 1) * GDN_HEADS], axis=1) for i in range(ncb)], axis=0)

    u_ref[...] = by_chunk([uw[:, :dh] for uw in uws]).astype(BF16)
    w_ref[...] = by_chunk([uw[:, dh:] for uw in uws]).astype(BF16)
    aqk_ref[...] = by_chunk(aqks)
    qd_ref[...] = jnp.concatenate(
        [qs[h] * egc_x[:, h * dh:(h + 1) * dh] for h in range(GDN_HEADS)], axis=1).astype(BF16)
    for i in range(ncb):
        rs = slice(i * c, (i + 1) * c)
        kdt_ref[i] = jnp.concatenate(
            [(ks[h][rs] * erev_x[rs, h * dh:(h + 1) * dh]).T for h in range(GDN_HEADS)], axis=1).astype(BF16)
        gl_ref[i] = egc_x[(i + 1) * c - 1:(i + 1) * c, :]


def _gdn_scan_kernel(u_ref, w_ref, qd_ref, aqk_ref, kdt_ref, gl_ref, zg_ref, normw_ref, o_ref, state_ref):
    c = GDN_CHUNK
    dh = GDN_HEAD_DIM

    @pl.when(pl.program_id(0) == 0)
    def _():
        state_ref[...] = jnp.zeros_like(state_ref)

    zero = jnp.zeros((c, dh), BF16)
    for b in range(u_ref.shape[0]):
        outs = []
        for pr in range(GDN_HEADS // 2):
            heads = (2 * pr, 2 * pr + 1)
            ss, q_s, v_news = [], [], []
            for h in heads:
                hs = slice(h * dh, (h + 1) * dh)
                s = state_ref[b, h]
                p = _dot(jnp.concatenate([w_ref[b, :, hs], qd_ref[b, :, hs]], axis=0), s.astype(BF16))
                ss.append(s)
                q_s.append(p[c:])
                v_news.append((u_ref[b, :, hs].astype(F32) - p[:c]).astype(BF16))
            lhs = jnp.concatenate([aqk_ref[b, :, pr * dh:(pr + 1) * dh], kdt_ref[b, 0, :, pr * dh:(pr + 1) * dh]],
                                  axis=0)
            bd = jnp.concatenate([jnp.concatenate([v_news[0], zero], axis=1),
                                  jnp.concatenate([zero, v_news[1]], axis=1)], axis=0)
            r = _dot(lhs, bd)
            for j, h in enumerate(heads):
                hs = slice(h * dh, (h + 1) * dh)
                state_ref[b, h] = ss[j] * gl_ref[b, 0, :, hs] + r[c:, j * dh:(j + 1) * dh]
                o = q_s[j] + r[:c, j * dh:(j + 1) * dh]
                o = o * lax.rsqrt(jnp.mean(o * o, axis=-1, keepdims=True) + EPS) * normw_ref[...]
                outs.append(o * _silu(zg_ref[b, :, hs]))
        o_ref[b] = jnp.concatenate(outs, axis=1).astype(BF16)


def _gdn(proj, small, convw, dtb, alog, normw, expand_a, expand_b, bsz, seqlen):
    c = GDN_CHUNK
    t = GDN_PREP_ROWS
    nc = seqlen // c
    nb = seqlen // t
    n = bsz * seqlen
    hh = GDN_HEADS * c
    rowmap = lambda b, i: b * nb + i
    const = lambda b, i: (0, 0)
    params = pltpu.CompilerParams(dimension_semantics=("arbitrary", "arbitrary"), vmem_limit_bytes=VMEM_LIMIT)
    u, w, qd, aqk, kdt, gl = pl.pallas_call(
        _gdn_prep_kernel,
        grid=(bsz, nb),
        in_specs=[
            pl.BlockSpec((t, 3 * GDN_DIM), lambda b, i: (rowmap(b, i), QKV_BLK)),
            pl.BlockSpec((t, LANES), lambda b, i: (rowmap(b, i), 0)),
            pl.BlockSpec((CONV_K, 3 * GDN_DIM), const),
            pl.BlockSpec((1, LANES), const),
            pl.BlockSpec((1, LANES), const),
            pl.BlockSpec((LANES, GDN_DIM), const),
            pl.BlockSpec((LANES, GDN_DIM), const),
        ],
        out_specs=[
            pl.BlockSpec((t, GDN_DIM), lambda b, i: (rowmap(b, i), 0)),
            pl.BlockSpec((t, GDN_DIM), lambda b, i: (rowmap(b, i), 0)),
            pl.BlockSpec((t, GDN_DIM), lambda b, i: (rowmap(b, i), 0)),
            pl.BlockSpec((t, hh), lambda b, i: (rowmap(b, i), 0)),
            pl.BlockSpec((t // c, GDN_HEAD_DIM, hh), lambda b, i: (rowmap(b, i), 0, 0)),
            pl.BlockSpec((t // c, 1, GDN_DIM), lambda b, i: (rowmap(b, i), 0, 0)),
        ],
        out_shape=[
            jax.ShapeDtypeStruct((n, GDN_DIM), BF16),
            jax.ShapeDtypeStruct((n, GDN_DIM), BF16),
            jax.ShapeDtypeStruct((n, GDN_DIM), BF16),
            jax.ShapeDtypeStruct((n, hh), BF16),
            jax.ShapeDtypeStruct((n // c, GDN_HEAD_DIM, hh), BF16),
            jax.ShapeDtypeStruct((n // c, 1, GDN_DIM), F32),
        ],
        scratch_shapes=[pltpu.VMEM((HALO, 3 * GDN_DIM), F32)],
        compiler_params=params,
        name="gdn_prep",
    )(proj, small, convw, dtb, alog, expand_a, expand_b)

    seq3 = lambda i: (0, i, 0)
    seq4 = lambda i: (0, i, 0, 0)
    o = pl.pallas_call(
        _gdn_scan_kernel,
        grid=(nc,),
        in_specs=[
            pl.BlockSpec((bsz, c, GDN_DIM), seq3),
            pl.BlockSpec((bsz, c, GDN_DIM), seq3),
            pl.BlockSpec((bsz, c, GDN_DIM), seq3),
            pl.BlockSpec((bsz, c, hh), seq3),
            pl.BlockSpec((bsz, 1, GDN_HEAD_DIM, hh), seq4),
            pl.BlockSpec((bsz, 1, 1, GDN_DIM), seq4),
            pl.BlockSpec((bsz, c, GDN_DIM), lambda i: (0, i, ZG_BLK)),
            pl.BlockSpec((1, GDN_HEAD_DIM), lambda i: (0, 0)),
        ],
        out_specs=pl.BlockSpec((bsz, c, GDN_DIM), seq3),
        out_shape=jax.ShapeDtypeStruct((bsz, seqlen, GDN_DIM), BF16),
        scratch_shapes=[pltpu.VMEM((bsz, GDN_HEADS, GDN_HEAD_DIM, GDN_HEAD_DIM), F32)],
        compiler_params=pltpu.CompilerParams(dimension_semantics=("arbitrary",), vmem_limit_bytes=VMEM_LIMIT),
        name="gdn_scan",
    )(u.reshape(bsz, seqlen, GDN_DIM), w.reshape(bsz, seqlen, GDN_DIM), qd.reshape(bsz, seqlen, GDN_DIM),
      aqk.reshape(bsz, seqlen, hh), kdt.reshape(bsz, nc, GDN_HEAD_DIM, hh), gl.reshape(bsz, nc, 1, GDN_DIM),
      proj.reshape(bsz, seqlen, MAIN_COLS), normw)
    return o.reshape(n, GDN_DIM)


def _merge_kernel(y_ref, o_ref, gate_ref, h_ref, bm_ref, wsp_ref, wgp_ref, wout_ref, out_ref):
    d = h_ref.shape[1]
    y_ssd = _dot(y_ref[...], wsp_ref[...])
    y_gdn = _dot(o_ref[...], wgp_ref[...])
    gates = _sigmoid(gate_ref[...] + bm_ref[...])
    merged = gates[:, :d] * y_ssd + gates[:, d:] * y_gdn
    out_ref[...] = h_ref[...] + _dot(merged.astype(BF16), wout_ref[...])


def _merge(y, o, proj, h, b_merge, w_sp, w_gp, w_out, tm):
    n, d = h.shape
    const = lambda i: (0, 0)
    return pl.pallas_call(
        _merge_kernel,
        grid=(n // tm,),
        in_specs=[
            pl.BlockSpec((tm, SSD_D_INNER), lambda i: (i, 0)),
            pl.BlockSpec((tm, GDN_DIM), lambda i: (i, 0)),
            pl.BlockSpec((tm, 2 * d), lambda i: (i, GATE_BLK)),
            pl.BlockSpec((tm, d), lambda i: (i, 0)),
            pl.BlockSpec((1, 2 * d), const),
            pl.BlockSpec((SSD_D_INNER, d), const),
            pl.BlockSpec((GDN_DIM, d), const),
            pl.BlockSpec((d, d), const),
        ],
        out_specs=pl.BlockSpec((tm, d), lambda i: (i, 0)),
        out_shape=jax.ShapeDtypeStruct((n, d), F32),
        compiler_params=pltpu.CompilerParams(
            dimension_semantics=("arbitrary",), vmem_limit_bytes=VMEM_LIMIT),
        name="merge",
    )(y, o, proj, h, b_merge, w_sp, w_gp, w_out)


MOE_PAIRS = MOE_EPG * (MOE_EPG - 1) // 2
MOE_BUCKETS = MOE_GROUPS * MOE_PAIRS
MOE_TILE = 256
RT_BUCKET, RT_WLO, RT_WHI = 0, 1, 2


def _route(logits):
    lane = lax.broadcasted_iota(jnp.int32, logits.shape, 1)
    neg = -jnp.inf
    big = jnp.int32(LANES)
    is_g = (lane >= RT_G) & (lane < RT_G + MOE_GROUPS)
    gl = jnp.where(is_g, logits, neg)
    gmax = jnp.max(gl, axis=-1, keepdims=True)
    gsel = jnp.min(jnp.where(gl == gmax, lane, big), axis=-1, keepdims=True) - RT_G
    group_w = 1.0 / jnp.sum(jnp.exp(gl - gmax), axis=-1, keepdims=True)
    e_lo = RT_E + gsel * MOE_EPG
    in_grp = (lane >= e_lo) & (lane < e_lo + MOE_EPG)
    el = jnp.where(in_grp, logits, neg)
    m1 = jnp.max(el, axis=-1, keepdims=True)
    i1 = jnp.min(jnp.where(el == m1, lane, big), axis=-1, keepdims=True)
    el2 = jnp.where(lane == i1, neg, el)
    m2 = jnp.max(el2, axis=-1, keepdims=True)
    i2 = jnp.min(jnp.where(el2 == m2, lane, big), axis=-1, keepdims=True)
    z = jnp.sum(jnp.exp(el - m1), axis=-1, keepdims=True)
    p1 = 1.0 / z
    p2 = jnp.exp(m2 - m1) / z
    tot = p1 + p2
    w1 = group_w * (p1 / tot)
    w2 = group_w * (p2 / tot)
    l1 = i1 - e_lo
    l2 = i2 - e_lo
    lo = jnp.minimum(l1, l2)
    hi = jnp.maximum(l1, l2)
    pair = ((lo * (2 * MOE_EPG - 1 - lo)) >> 1) + (hi - lo - 1)
    bucket = (gsel * MOE_PAIRS + pair).astype(F32)
    first_lo = l1 < l2
    w_lo = jnp.where(first_lo, w1, w2)
    w_hi = jnp.where(first_lo, w2, w1)
    return jnp.where(lane == RT_BUCKET, bucket,
                     jnp.where(lane == RT_WLO, w_lo, jnp.where(lane == RT_WHI, w_hi, 0.0)))


def _moe_rmsnorm(x, lnw):
    return x * lax.rsqrt(jnp.mean(x * x, axis=-1, keepdims=True) + EPS) * lnw


def _router_kernel(h_ref, lnw_ref, wr_ref, br_ref, info_ref):
    hi, lo = _split(_moe_rmsnorm(h_ref[...], lnw_ref[...]))
    whi, wlo = wr_ref[0], wr_ref[1]
    info_ref[...] = _route(_dot(hi, whi) + _dot(lo, whi) + _dot(hi, wlo) + br_ref[...])


def _gather_rows_kernel(idx_ref, src_hbm, dst_hbm, sem):
    base = pl.program_id(0) * MOE_TILE

    def issue(r, carry):
        pltpu.make_async_copy(src_hbm.at[pl.ds(idx_ref[base + r], 1)], dst_hbm.at[pl.ds(base + r, 1)], sem).start()
        return carry

    lax.fori_loop(0, MOE_TILE, issue, 0, unroll=8)
    pltpu.make_async_copy(src_hbm.at[pl.ds(0, MOE_TILE)], dst_hbm.at[pl.ds(base, MOE_TILE)], sem).wait()


def _experts_kernel(lo_ref, hi_ref, nt_ref, x_ref, w_ref, lnw_ref, wg_lo, wu_lo, wd_lo, wg_hi, wu_hi, wd_hi,
                    y_ref):
    live = pl.program_id(0) < nt_ref[0]

    @pl.when(live)
    def _():
        xn = _moe_rmsnorm(x_ref[...], lnw_ref[...]).astype(BF16)
        wts = w_ref[...]
        y = None
        for j, (wg, wu, wd) in enumerate(((wg_lo, wu_lo, wd_lo), (wg_hi, wu_hi, wd_hi))):
            hid = _silu(_dot(xn, wg[0])) * _dot(xn, wu[0]) * wts[:, j:j + 1]
            part = _dot(hid.astype(BF16), wd[0])
            y = part if y is None else y + part
        y_ref[...] = y

    @pl.when(jnp.logical_not(live))
    def _():
        y_ref[...] = jnp.zeros_like(y_ref)


def _combine_kernel(pos_ref, y_hbm, h_ref, fnw_ref, out_ref, buf_ref, sem, *, final_norm):
    tm = h_ref.shape[0]
    base = pl.program_id(0) * tm

    def issue(r, carry):
        pltpu.make_async_copy(y_hbm.at[pl.ds(pos_ref[base + r], 1)], buf_ref.at[pl.ds(r, 1)], sem).start()
        return carry

    lax.fori_loop(0, tm, issue, 0, unroll=8)
    pltpu.make_async_copy(y_hbm.at[pl.ds(0, tm)], buf_ref, sem).wait()
    y = h_ref[...] + buf_ref[...]
    if final_norm:
        y = _moe_rmsnorm(y, fnw_ref[...])
    out_ref[...] = y


def _moe(h, lnw, w_router, b_router, w_gate, w_up, w_down, fnw, tm, final_norm):
    n, d = h.shape
    ne, _, dff = w_gate.shape
    tile = MOE_TILE
    n_tiles = n // tile + MOE_BUCKETS
    p = n_tiles * tile
    const = lambda i: (0, 0)
    info = pl.pallas_call(
        _router_kernel,
        grid=(n // tm,),
        in_specs=[
            pl.BlockSpec((tm, d), lambda i: (i, 0)),
            pl.BlockSpec((1, d), const),
            pl.BlockSpec((2, d, LANES), lambda i: (0, 0, 0)),
            pl.BlockSpec((1, LANES), const),
        ],
        out_specs=pl.BlockSpec((tm, LANES), lambda i: (i, 0)),
        out_shape=jax.ShapeDtypeStruct((n, LANES), F32),
        compiler_params=pltpu.CompilerParams(dimension_semantics=("arbitrary",), vmem_limit_bytes=VMEM_LIMIT),
        name="moe_router",
    )(h, lnw, w_router, b_router)

    bucket = info[:, RT_BUCKET].astype(jnp.int32)
    onehot = (bucket[:, None] == jnp.arange(MOE_BUCKETS, dtype=jnp.int32)[None, :]).astype(jnp.int32)
    running = jnp.cumsum(onehot, axis=0)
    counts = running[-1]
    rank = jnp.sum(onehot * running, axis=1) - 1
    tiles_per = (counts + tile - 1) // tile
    tile_end = jnp.cumsum(tiles_per)
    pos = ((tile_end - tiles_per) * tile)[bucket] + rank
    src = jnp.zeros((p,), jnp.int32).at[pos].set(jnp.arange(n, dtype=jnp.int32))
    w_sorted = jnp.zeros((p, 2), F32).at[pos].set(info[:, RT_WLO:RT_WHI + 1])
    n_live = tile_end[-1]
    tile_bucket = jnp.searchsorted(tile_end, jnp.minimum(jnp.arange(n_tiles, dtype=jnp.int32), n_live - 1),
                                   side="right").astype(jnp.int32)
    pairs = [(a, b) for a in range(MOE_EPG) for b in range(a + 1, MOE_EPG)]
    grp = tile_bucket // MOE_PAIRS
    tile_lo = grp * MOE_EPG + jnp.array([a for a, _ in pairs], jnp.int32)[tile_bucket % MOE_PAIRS]
    tile_hi = grp * MOE_EPG + jnp.array([b for _, b in pairs], jnp.int32)[tile_bucket % MOE_PAIRS]

    x_sorted = pl.pallas_call(
        _gather_rows_kernel,
        grid_spec=pltpu.PrefetchScalarGridSpec(
            num_scalar_prefetch=1, grid=(n_tiles,),
            in_specs=[pl.BlockSpec(memory_space=pl.ANY)],
            out_specs=pl.BlockSpec(memory_space=pl.ANY),
            scratch_shapes=[pltpu.SemaphoreType.DMA(())]),
        out_shape=jax.ShapeDtypeStruct((p, d), F32),
        compiler_params=pltpu.CompilerParams(dimension_semantics=("arbitrary",)),
        name="moe_gather",
    )(src, h)

    lo_map = lambda t, lo, hi, nt: (lo[t], 0, 0)
    hi_map = lambda t, lo, hi, nt: (hi[t], 0, 0)
    y_sorted = pl.pallas_call(
        _experts_kernel,
        grid_spec=pltpu.PrefetchScalarGridSpec(
            num_scalar_prefetch=3, grid=(n_tiles,),
            in_specs=[
                pl.BlockSpec((tile, d), lambda t, lo, hi, nt: (t, 0)),
                pl.BlockSpec((tile, 2), lambda t, lo, hi, nt: (t, 0)),
                pl.BlockSpec((1, d), lambda t, lo, hi, nt: (0, 0)),
                pl.BlockSpec((1, d, dff), lo_map),
                pl.BlockSpec((1, d, dff), lo_map),
                pl.BlockSpec((1, dff, d), lo_map),
                pl.BlockSpec((1, d, dff), hi_map),
                pl.BlockSpec((1, d, dff), hi_map),
                pl.BlockSpec((1, dff, d), hi_map),
            ],
            out_specs=pl.BlockSpec((tile, d), lambda t, lo, hi, nt: (t, 0))),
        out_shape=jax.ShapeDtypeStruct((p, d), F32),
        compiler_params=pltpu.CompilerParams(dimension_semantics=("arbitrary",), vmem_limit_bytes=VMEM_LIMIT),
        name="moe_experts",
    )(tile_lo, tile_hi, n_live.reshape(1).astype(jnp.int32), x_sorted, w_sorted, lnw,
      w_gate, w_up, w_down, w_gate, w_up, w_down)

    return pl.pallas_call(
        functools.partial(_combine_kernel, final_norm=final_norm),
        grid_spec=pltpu.PrefetchScalarGridSpec(
            num_scalar_prefetch=1, grid=(n // tile,),
            in_specs=[
                pl.BlockSpec(memory_space=pl.ANY),
                pl.BlockSpec((tile, d), lambda i, pos: (i, 0)),
                pl.BlockSpec((1, d), lambda i, pos: (0, 0)),
            ],
            out_specs=pl.BlockSpec((tile, d), lambda i, pos: (i, 0)),
            scratch_shapes=[pltpu.VMEM((tile, d), F32), pltpu.SemaphoreType.DMA(())]),
        out_shape=jax.ShapeDtypeStruct((n, d), F32),
        compiler_params=pltpu.CompilerParams(dimension_semantics=("arbitrary",), vmem_limit_bytes=VMEM_LIMIT),
        name="moe_combine",
    )(pos.astype(jnp.int32), y_sorted, h, fnw)


def _lane_row(v, offset):
    return jnp.zeros((1, LANES), F32).at[0, offset:offset + v.shape[0]].set(v.astype(F32))


def _expansion(offset, heads, width):
    r = jnp.arange(LANES)[:, None]
    c = jnp.arange(heads * width)[None, :]
    return (r == offset + c // width).astype(BF16)


def _row_tile(n, want):
    t = min(n, want)
    while n % t:
        t //= 2
    return t


def kernel(x, ln1_w, w_in, ssd_conv_w, ssd_conv_b, ssd_dt_bias, ssd_a_log, ssd_d, ssd_norm_w, w_ssd_proj,
           gdn_conv_w, gdn_dt_bias, gdn_a_log, gdn_norm_w, w_gdn_proj, b_merge, w_out, ln2_w,
           moe_w_group, moe_b_group, moe_w_expert, moe_b_expert, moe_w_gate, moe_w_up, moe_w_down,
           final_norm_w):
    bsz, seqlen, d = x.shape
    n = bsz * seqlen
    depth = w_in.shape[0]
    h = x.reshape(n, d)

    o1 = SSD_D_INNER
    o2 = o1 + SSD_XBC
    o3 = o2 + SSD_HEADS
    o4 = o3 + 3 * GDN_DIM
    o5 = o4 + GDN_HEADS
    o6 = o5 + GDN_HEADS
    o7 = o6 + GDN_DIM

    e_ssd = _expansion(SM_DT, SSD_HEADS, SSD_HEAD_DIM)
    e_gdn_a = _expansion(SM_A, GDN_HEADS, GDN_HEAD_DIM)
    e_gdn_b = _expansion(SM_BETA, GDN_HEADS, GDN_HEAD_DIM)
    tm_proj = _row_tile(n, 1024)
    tm_merge = _row_tile(n, 512)
    tm_moe = _row_tile(n, 1024)

    for i in range(depth):
        wi = w_in[i]
        w_main = jnp.concatenate(
            [wi[:, o1:o2], wi[:, o3:o4], wi[:, :o1], wi[:, o7:], wi[:, o6:o7]], axis=1).astype(BF16)
        w_small = jnp.concatenate(
            [wi[:, o2:o3], wi[:, o4:o5], wi[:, o5:o6],
             jnp.zeros((d, LANES - SSD_HEADS - 2 * GDN_HEADS), F32)], axis=1).astype(BF16)
        proj, small = _in_proj(h, ln1_w[i][None, :], w_main, w_small, tm_proj, 1024)

        y = _ssd(proj, small, ssd_conv_w[i], ssd_conv_b[i][None, :],
                 _lane_row(ssd_dt_bias[i], SM_DT), _lane_row(ssd_a_log[i], SM_DT),
                 jnp.repeat(ssd_d[i], SSD_HEAD_DIM)[None, :], ssd_norm_w[i][None, :], e_ssd, bsz, seqlen)
        o = _gdn(proj, small, gdn_conv_w[i], _lane_row(gdn_dt_bias[i], SM_A), _lane_row(gdn_a_log[i], SM_A),
                 gdn_norm_w[i][None, :], e_gdn_a, e_gdn_b, bsz, seqlen)
        h = _merge(y, o, proj, h, b_merge[i][None, :], w_ssd_proj[i].astype(BF16), w_gdn_proj[i].astype(BF16),
                   w_out[i].astype(BF16), tm_merge)

        w_r = jnp.concatenate(
            [moe_w_group[i], moe_w_expert[i], jnp.zeros((d, LANES - MOE_GROUPS - MOE_EXPERTS), F32)], axis=1)
        w_r_hi = w_r.astype(BF16)
        w_r_lo = (w_r - w_r_hi.astype(F32)).astype(BF16)
        b_r = jnp.concatenate(
            [moe_b_group[i], moe_b_expert[i], jnp.zeros((LANES - MOE_GROUPS - MOE_EXPERTS,), F32)])[None, :]
        h = _moe(h, ln2_w[i][None, :], jnp.stack([w_r_hi, w_r_lo]), b_r,
                 moe_w_gate[i].astype(BF16), moe_w_up[i].astype(BF16), moe_w_down[i].astype(BF16),
                 final_norm_w[None, :], tm_moe, final_norm=(i == depth - 1))
    return h.reshape(bsz, seqlen, d)
```

```python
import functools

import jax
import jax.numpy as jnp
from jax import lax
from jax.experimental import pallas as pl
from jax.experimental.pallas import tpu as pltpu

F32 = jnp.float32
BF16 = jnp.bfloat16

EPS = 1e-6
CONV_K = 4
LANES = 128
SUBLANES = 8
TIME_BLOCK = SUBLANES * SUBLANES
HALO = (CONV_K - 1) * SUBLANES

D_MODEL = 1024
SSD_D_INNER = 2 * D_MODEL
SSD_HEAD_DIM = 64
SSD_HEADS = SSD_D_INNER // SSD_HEAD_DIM
SSD_GROUPS = 4
SSD_STATE = 128
SSD_CHUNK = 128
SSD_XBC = SSD_D_INNER + 2 * SSD_GROUPS * SSD_STATE
GDN_HEADS = 8
GDN_HEAD_DIM = 128
GDN_DIM = GDN_HEADS * GDN_HEAD_DIM
GDN_CHUNK = 64
MOE_GROUPS = 4
MOE_EPG = 4
MOE_EXPERTS = MOE_GROUPS * MOE_EPG
MOE_D_FF = 512

MAIN_COLS = SSD_XBC + 3 * GDN_DIM + SSD_D_INNER + 2 * D_MODEL + GDN_DIM
XBC_BLK, QKV_BLK = 0, 1
ZS_BLK, GATE_BLK = 3, 4
ZG_BLK = 10
SM_DT, SM_A, SM_BETA = 0, SSD_HEADS, SSD_HEADS + GDN_HEADS
RT_G, RT_E = 0, MOE_GROUPS

VMEM_LIMIT = 56 * 1024 * 1024


def _dot(a, b):
    return jnp.dot(a, b, preferred_element_type=F32)


def _dot_nt(a, b):
    return lax.dot_general(a, b, (((1,), (1,)), ((), ())), preferred_element_type=F32)


def _split(x):
    hi = x.astype(BF16)
    lo = (x - hi.astype(F32)).astype(BF16)
    return hi, lo


def _dot_split_lhs(x, w):
    hi, lo = _split(x)
    return _dot(hi, w) + _dot(lo, w)


def _dot_split_rhs(w, x):
    hi, lo = _split(x)
    return _dot(w, hi) + _dot(w, lo)


def _sigmoid(x):
    return 1.0 / (1.0 + jnp.exp(-x))


def _silu(x):
    return x * _sigmoid(x)


def _softplus(x):
    return jnp.maximum(x, 0.0) + jnp.log1p(jnp.exp(-jnp.abs(x)))


def _time_index(r):
    return (r & ~(TIME_BLOCK - 1)) | ((r >> 3) & (SUBLANES - 1)) | ((r & (SUBLANES - 1)) << 3)


def _causal_conv(cur, tail_ref, w, first):
    t, c = cur.shape

    @pl.when(first)
    def _():
        tail_ref[...] = jnp.zeros_like(tail_ref)

    sub = lax.broadcasted_iota(jnp.int32, (SUBLANES, c), 0)
    prev_tail = tail_ref[...]
    outs = []
    for i in range(t // TIME_BLOCK):
        blk = cur[i * TIME_BLOCK:(i + 1) * TIME_BLOCK, :]
        tail = blk[TIME_BLOCK - HALO:, :]
        wrapped = [
            jnp.where(sub == 0,
                      pltpu.roll(prev_tail[j * SUBLANES:(j + 1) * SUBLANES, :], 1, axis=0),
                      pltpu.roll(tail[j * SUBLANES:(j + 1) * SUBLANES, :], 1, axis=0))
            for j in range(CONV_K - 1)]
        ext = jnp.concatenate(wrapped + [blk], axis=0)
        acc = None
        for k in range(CONV_K):
            term = ext[k * SUBLANES:k * SUBLANES + TIME_BLOCK, :] * w[k:k + 1, :]
            acc = term if acc is None else acc + term
        outs.append(acc)
        prev_tail = tail
    tail_ref[...] = prev_tail
    return jnp.concatenate(outs, axis=0)


def _inproj_kernel(x_ref, lnw_ref, w_ref, ws_ref, o_ref, os_ref, xn_ref):
    @pl.when(pl.program_id(1) == 0)
    def _():
        x = x_ref[...]
        xn = x * lax.rsqrt(jnp.mean(x * x, axis=-1, keepdims=True) + EPS) * lnw_ref[...]
        xn_ref[...] = xn.astype(BF16)
        os_ref[...] = _dot(xn_ref[...], ws_ref[...])

    o_ref[...] = _dot(xn_ref[...], w_ref[...])


def _in_proj(x, lnw, w_main, w_small, tm, tn):
    n, d = x.shape
    nm = w_main.shape[1]
    return pl.pallas_call(
        _inproj_kernel,
        grid=(n // tm, nm // tn),
        in_specs=[
            pl.BlockSpec((tm, d), lambda i, j: (i, 0)),
            pl.BlockSpec((1, d), lambda i, j: (0, 0)),
            pl.BlockSpec((d, tn), lambda i, j: (0, j)),
            pl.BlockSpec((d, LANES), lambda i, j: (0, 0)),
        ],
        out_specs=[
            pl.BlockSpec((tm, tn), lambda i, j: (i, j)),
            pl.BlockSpec((tm, LANES), lambda i, j: (i, 0)),
        ],
        out_shape=[jax.ShapeDtypeStruct((n, nm), F32), jax.ShapeDtypeStruct((n, LANES), F32)],
        scratch_shapes=[pltpu.VMEM((tm, d), BF16)],
        compiler_params=pltpu.CompilerParams(
            dimension_semantics=("arbitrary", "arbitrary"), vmem_limit_bytes=VMEM_LIMIT),
        name="in_proj",
    )(x, lnw, w_main, w_small)


def _ssd_kernel(xbc_ref, z_ref, sm_ref, convw_ref, convb_ref, dtb_ref, alog_ref, dexp_ref, normw_ref, e_ref,
                y_ref, halo_ref, state_ref, yacc_ref):
    q = SSD_CHUNK
    p = SSD_HEAD_DIM
    ns = SSD_STATE
    gw = (SSD_HEADS // SSD_GROUPS) * p
    first = pl.program_id(1) == 0

    @pl.when(first)
    def _():
        state_ref[...] = jnp.zeros_like(state_ref)

    conv = _causal_conv(xbc_ref[...], halo_ref, convw_ref[...], first) + convb_ref[...]
    xbc = _silu(conv)
    xs = xbc[:, :SSD_D_INNER]
    bm = xbc[:, SSD_D_INNER:SSD_D_INNER + SSD_GROUPS * ns]
    cm = xbc[:, SSD_D_INNER + SSD_GROUPS * ns:]

    dt = _softplus(sm_ref[...] + dtb_ref[...])
    d_a = dt * (-jnp.exp(alog_ref[...]))
    row = lax.broadcasted_iota(jnp.int32, (q, q), 0)
    col = lax.broadcasted_iota(jnp.int32, (q, q), 1)
    causal = _time_index(row) >= _time_index(col)
    tri = jnp.where(causal, 1.0, 0.0).astype(BF16)
    a_cs = _dot_split_rhs(tri, d_a)
    a_cs_t = a_cs.T
    e_acs = jnp.exp(a_cs)
    dec_end = jnp.exp(a_cs[q - 1:q, :] - a_cs)
    ex = _dot_split_lhs(jnp.concatenate([dt, dt * dec_end, e_acs], axis=0), e_ref[...])
    dt_x, dtdec_x, eacs_x = ex[:q], ex[q:2 * q], ex[2 * q:]

    xdt_b = (xs * dt_x).astype(BF16)
    xdd_b = (xs * dtdec_x).astype(BF16)
    lane = lax.broadcasted_iota(jnp.int32, (q, LANES), 1)
    for g in range(SSD_GROUPS):
        bg = bm[:, g * ns:(g + 1) * ns]
        cg_b = cm[:, g * ns:(g + 1) * ns].astype(BF16)
        scores = _dot_nt(cg_b, bg.astype(BF16))
        s_g = state_ref[:, g * gw:(g + 1) * gw]
        yacc_ref[:, g * gw:(g + 1) * gw] = _dot(cg_b, s_g.astype(BF16)) * eacs_x[:, g * gw:(g + 1) * gw]
        new_g = _dot(bg.T.astype(BF16), xdd_b[:, g * gw:(g + 1) * gw])
        state_ref[:, g * gw:(g + 1) * gw] = s_g * eacs_x[q - 1:q, g * gw:(g + 1) * gw] + new_g
        for j in range(gw // LANES):
            h0 = (g * gw + j * LANES) // p
            c0 = g * gw + j * LANES
            ms = []
            for h in (h0, h0 + 1):
                seg = a_cs[:, h:h + 1] - a_cs_t[h:h + 1, :]
                ms.append((scores * jnp.exp(jnp.where(causal, seg, -jnp.inf))).astype(BF16))
            xp = xdt_b[:, c0:c0 + LANES]
            zero = jnp.zeros_like(xp)
            rhs = jnp.concatenate([jnp.where(lane < p, xp, zero), jnp.where(lane >= p, xp, zero)], axis=0)
            yacc_ref[:, c0:c0 + LANES] += _dot(jnp.concatenate(ms, axis=1), rhs)

    y = yacc_ref[...] + dexp_ref[...] * xs
    y = y * _silu(z_ref[...])
    y = y * lax.rsqrt(jnp.mean(y * y, axis=-1, keepdims=True) + EPS) * normw_ref[...]
    y_ref[...] = y.astype(BF16)


def _ssd(proj, small, convw, convb, dtb, alog, dexp, normw, expand, bsz, seqlen):
    q = SSD_CHUNK
    nc = seqlen // q
    n = bsz * seqlen
    rowmap = lambda b, c: b * nc + c
    const = lambda b, c: (0, 0)
    return pl.pallas_call(
        _ssd_kernel,
        grid=(bsz, nc),
        in_specs=[
            pl.BlockSpec((q, SSD_XBC), lambda b, c: (rowmap(b, c), XBC_BLK)),
            pl.BlockSpec((q, SSD_D_INNER), lambda b, c: (rowmap(b, c), ZS_BLK)),
            pl.BlockSpec((q, LANES), lambda b, c: (rowmap(b, c), 0)),
            pl.BlockSpec((CONV_K, SSD_XBC), const),
            pl.BlockSpec((1, SSD_XBC), const),
            pl.BlockSpec((1, LANES), const),
            pl.BlockSpec((1, LANES), const),
            pl.BlockSpec((1, SSD_D_INNER), const),
            pl.BlockSpec((1, SSD_D_INNER), const),
            pl.BlockSpec((LANES, SSD_D_INNER), const),
        ],
        out_specs=pl.BlockSpec((q, SSD_D_INNER), lambda b, c: (rowmap(b, c), 0)),
        out_shape=jax.ShapeDtypeStruct((n, SSD_D_INNER), BF16),
        scratch_shapes=[
            pltpu.VMEM((HALO, SSD_XBC), F32),
            pltpu.VMEM((SSD_STATE, SSD_D_INNER), F32),
            pltpu.VMEM((q, SSD_D_INNER), F32),
        ],
        compiler_params=pltpu.CompilerParams(
            dimension_semantics=("arbitrary", "arbitrary"), vmem_limit_bytes=VMEM_LIMIT),
        name="ssd",
    )(proj, proj, small, convw, convb, dtb, alog, dexp, normw, expand)


GDN_PREP_ROWS = 2 * GDN_CHUNK
GDN_CHUNK_SHIFT = GDN_CHUNK.bit_length() - 1


def _unit_lower_inverse_many(ms, row, col):
    c = ms[0].shape[0]
    diff = row ^ col
    eye = jnp.where(row == col, 1.0, 0.0)
    ts = [eye - jnp.where((diff >> 1) == 0, m, 0.0) for m in ms]
    shift = 1
    while (1 << shift) < c:
        sel = (diff >> shift) == 1
        tbs = [t.astype(BF16) for t in ts]
        xs = [_dot(tb, jnp.where(sel, m, 0.0).astype(BF16)) for tb, m in zip(tbs, ms)]
        ts = [t - _dot(x.astype(BF16), tb) for t, x, tb in zip(ts, xs, tbs)]
        shift += 1
    return ts


def _gdn_prep_kernel(qkv_ref, sm_ref, convw_ref, dtb_ref, alog_ref, eg_ref, eb_ref,
                     u_ref, w_ref, qd_ref, aqk_ref, kdt_ref, gl_ref, halo_ref):
    c = GDN_CHUNK
    dh = GDN_HEAD_DIM
    t = GDN_PREP_ROWS
    ncb = t // c
    first = pl.program_id(1) == 0
    qkv = _silu(_causal_conv(qkv_ref[...], halo_ref, convw_ref[...], first))

    sm = sm_ref[...]
    lane1 = lax.broadcasted_iota(jnp.int32, (t, LANES), 1)
    is_a = (lane1 >= SM_A) & (lane1 < SM_A + GDN_HEADS)
    log_decay = jnp.where(is_a, -jnp.exp(alog_ref[...]) * _softplus(sm + dtb_ref[...]), 0.0)
    beta = _sigmoid(sm)
    row_t = lax.broadcasted_iota(jnp.int32, (t, t), 0)
    col_t = lax.broadcasted_iota(jnp.int32, (t, t), 1)
    same_chunk = (row_t >> GDN_CHUNK_SHIFT) == (col_t >> GDN_CHUNK_SHIFT)
    tri = jnp.where(same_chunk, jnp.where(_time_index(row_t) >= _time_index(col_t), 1.0, 0.0), 0.0).astype(BF16)
    gc = _dot_split_rhs(tri, log_decay)
    gc_t = gc.T
    g_last = jnp.concatenate(
        [jnp.broadcast_to(gc[(i + 1) * c - 1:(i + 1) * c, :], (c, LANES)) for i in range(ncb)], axis=0)
    ex = _dot_split_lhs(jnp.concatenate([jnp.exp(gc), jnp.exp(g_last - gc)], axis=0), eg_ref[...])
    egc_x, erev_x = ex[:t], ex[t:]
    beta_x = _dot_split_lhs(beta, eb_ref[...])

    row = _time_index(lax.broadcasted_iota(jnp.int32, (c, c), 0))
    col = _time_index(lax.broadcasted_iota(jnp.int32, (c, c), 1))
    incl = row >= col
    strict = row > col

    qs, ks, kbs, rhs = [], [], [], []
    for h in range(GDN_HEADS):
        hs = slice(h * dh, (h + 1) * dh)
        qh = qkv[:, h * dh:(h + 1) * dh]
        kh = qkv[:, GDN_DIM + h * dh:GDN_DIM + (h + 1) * dh]
        vh = qkv[:, 2 * GDN_DIM + h * dh:2 * GDN_DIM + (h + 1) * dh]
        qh = qh * lax.rsqrt(jnp.sum(qh * qh, axis=-1, keepdims=True) + EPS) * (dh ** -0.5)
        kh = kh * lax.rsqrt(jnp.sum(kh * kh, axis=-1, keepdims=True) + EPS)
        kb = kh * beta_x[:, hs]
        qs.append(qh)
        ks.append(kh)
        kbs.append(kb)
        rhs.append(jnp.concatenate([vh * beta_x[:, hs], kb * egc_x[:, hs]], axis=1).astype(BF16))

    items = [(i, h) for i in range(ncb) for h in range(GDN_HEADS)]
    ms, aqks = [], []
    for i, h in items:
        rs = slice(i * c, (i + 1) * c)
        la = SM_A + h
        decay = jnp.exp(jnp.where(incl, gc[rs, la:la + 1] - gc_t[la:la + 1, rs], -jnp.inf))
        nt = _dot_nt(jnp.concatenate([kbs[h][rs], qs[h][rs]], axis=0).astype(BF16), ks[h][rs].astype(BF16))
        ms.append(jnp.where(strict, nt[:c] * decay, 0.0))
        aqks.append((nt[c:] * decay).astype(BF16))
    t_invs = _unit_lower_inverse_many(ms, row, col)
    uws = [_dot(t_inv.astype(BF16), rhs[h][i * c:(i + 1) * c]) for (i, h), t_inv in zip(items, t_invs)]

    def by_chunk(pieces):
        return jnp.concatenate(
            [jnp.concatenate(pieces[i * GDN_HEADS:(i + 1) * GDN_HEADS], axis=1) for i in range(ncb)], axis=0)

    u_ref[...] = by_chunk([uw[:, :dh] for uw in uws]).astype(BF16)
    w_ref[...] = by_chunk([uw[:, dh:] for uw in uws]).astype(BF16)
    aqk_ref[...] = by_chunk(aqks)
    qd_ref[...] = jnp.concatenate(
        [qs[h] * egc_x[:, h * dh:(h + 1) * dh] for h in range(GDN_HEADS)], axis=1).astype(BF16)
    for i in range(ncb):
        rs = slice(i * c, (i + 1) * c)
        kdt_ref[i] = jnp.concatenate(
            [(ks[h][rs] * erev_x[rs, h * dh:(h + 1) * dh]).T for h in range(GDN_HEADS)], axis=1).astype(BF16)
        gl_ref[i] = egc_x[(i + 1) * c - 1:(i + 1) * c, :]


def _gdn_scan_kernel(u_ref, w_ref, qd_ref, aqk_ref, kdt_ref, gl_ref, zg_ref, normw_ref, o_ref, state_ref):
    c = GDN_CHUNK
    dh = GDN_HEAD_DIM

    @pl.when(pl.program_id(0) == 0)
    def _():
        state_ref[...] = jnp.zeros_like(state_ref)

    zero = jnp.zeros((c, dh), BF16)
    for b in range(u_ref.shape[0]):
        outs = []
        for pr in range(GDN_HEADS // 2):
            heads = (2 * pr, 2 * pr + 1)
            ss, q_s, v_news = [], [], []
            for h in heads:
                hs = slice(h * dh, (h + 1) * dh)
                s = state_ref[b, h]
                p = _dot(jnp.concatenate([w_ref[b, :, hs], qd_ref[b, :, hs]], axis=0), s.astype(BF16))
                ss.append(s)
                q_s.append(p[c:])
                v_news.append((u_ref[b, :, hs].astype(F32) - p[:c]).astype(BF16))
            lhs = jnp.concatenate([aqk_ref[b, :, pr * dh:(pr + 1) * dh], kdt_ref[b, 0, :, pr * dh:(pr + 1) * dh]],
                                  axis=0)
            bd = jnp.concatenate([jnp.concatenate([v_news[0], zero], axis=1),
                                  jnp.concatenate([zero, v_news[1]], axis=1)], axis=0)
            r = _dot(lhs, bd)
            for j, h in enumerate(heads):
                hs = slice(h * dh, (h + 1) * dh)
                state_ref[b, h] = ss[j] * gl_ref[b, 0, :, hs] + r[c:, j * dh:(j + 1) * dh]
                o = q_s[j] + r[:c, j * dh:(j + 1) * dh]
                o = o * lax.rsqrt(jnp.mean(o * o, axis=-1, keepdims=True) + EPS) * normw_ref[...]
                outs.append(o * _silu(zg_ref[b, :, hs]))
        o_ref[b] = jnp.concatenate(outs, axis=1).astype(BF16)


def _gdn(proj, small, convw, dtb, alog, normw, expand_a, expand_b, bsz, seqlen):
    c = GDN_CHUNK
    t = GDN_PREP_ROWS
    nc = seqlen // c
    nb = seqlen // t
    n = bsz * seqlen
    hh = GDN_HEADS * c
    rowmap = lambda b, i: b * nb + i
    const = lambda b, i: (0, 0)
    params = pltpu.CompilerParams(dimension_semantics=("arbitrary", "arbitrary"), vmem_limit_bytes=VMEM_LIMIT)
    u, w, qd, aqk, kdt, gl = pl.pallas_call(
        _gdn_prep_kernel,
        grid=(bsz, nb),
        in_specs=[
            pl.BlockSpec((t, 3 * GDN_DIM), lambda b, i: (rowmap(b, i), QKV_BLK)),
            pl.BlockSpec((t, LANES), lambda b, i: (rowmap(b, i), 0)),
            pl.BlockSpec((CONV_K, 3 * GDN_DIM), const),
            pl.BlockSpec((1, LANES), const),
            pl.BlockSpec((1, LANES), const),
            pl.BlockSpec((LANES, GDN_DIM), const),
            pl.BlockSpec((LANES, GDN_DIM), const),
        ],
        out_specs=[
            pl.BlockSpec((t, GDN_DIM), lambda b, i: (rowmap(b, i), 0)),
            pl.BlockSpec((t, GDN_DIM), lambda b, i: (rowmap(b, i), 0)),
            pl.BlockSpec((t, GDN_DIM), lambda b, i: (rowmap(b, i), 0)),
            pl.BlockSpec((t, hh), lambda b, i: (rowmap(b, i), 0)),
            pl.BlockSpec((t // c, GDN_HEAD_DIM, hh), lambda b, i: (rowmap(b, i), 0, 0)),
            pl.BlockSpec((t // c, 1, GDN_DIM), lambda b, i: (rowmap(b, i), 0, 0)),
        ],
        out_shape=[
            jax.ShapeDtypeStruct((n, GDN_DIM), BF16),
            jax.ShapeDtypeStruct((n, GDN_DIM), BF16),
            jax.ShapeDtypeStruct((n, GDN_DIM), BF16),
            jax.ShapeDtypeStruct((n, hh), BF16),
            jax.ShapeDtypeStruct((n // c, GDN_HEAD_DIM, hh), BF16),
            jax.ShapeDtypeStruct((n // c, 1, GDN_DIM), F32),
        ],
        scratch_shapes=[pltpu.VMEM((HALO, 3 * GDN_DIM), F32)],
        compiler_params=params,
        name="gdn_prep",
    )(proj, small, convw, dtb, alog, expand_a, expand_b)

    seq3 = lambda i: (0, i, 0)
    seq4 = lambda i: (0, i, 0, 0)
    o = pl.pallas_call(
        _gdn_scan_kernel,
        grid=(nc,),
        in_specs=[
            pl.BlockSpec((bsz, c, GDN_DIM), seq3),
            pl.BlockSpec((bsz, c, GDN_DIM), seq3),
            pl.BlockSpec((bsz, c, GDN_DIM), seq3),
            pl.BlockSpec((bsz, c, hh), seq3),
            pl.BlockSpec((bsz, 1, GDN_HEAD_DIM, hh), seq4),
            pl.BlockSpec((bsz, 1, 1, GDN_DIM), seq4),
            pl.BlockSpec((bsz, c, GDN_DIM), lambda i: (0, i, ZG_BLK)),
            pl.BlockSpec((1, GDN_HEAD_DIM), lambda i: (0, 0)),
        ],
        out_specs=pl.BlockSpec((bsz, c, GDN_DIM), seq3),
        out_shape=jax.ShapeDtypeStruct((bsz, seqlen, GDN_DIM), BF16),
        scratch_shapes=[pltpu.VMEM((bsz, GDN_HEADS, GDN_HEAD_DIM, GDN_HEAD_DIM), F32)],
        compiler_params=pltpu.CompilerParams(dimension_semantics=("arbitrary",), vmem_limit_bytes=VMEM_LIMIT),
        name="gdn_scan",
    )(u.reshape(bsz, seqlen, GDN_DIM), w.reshape(bsz, seqlen, GDN_DIM), qd.reshape(bsz, seqlen, GDN_DIM),
      aqk.reshape(bsz, seqlen, hh), kdt.reshape(bsz, nc, GDN_HEAD_DIM, hh), gl.reshape(bsz, nc, 1, GDN_DIM),
      proj.reshape(bsz, seqlen, MAIN_COLS), normw)
    return o.reshape(n, GDN_DIM)


def _merge_kernel(y_ref, o_ref, gate_ref, h_ref, bm_ref, wsp_ref, wgp_ref, wout_ref, out_ref):
    d = h_ref.shape[1]
    y_ssd = _dot(y_ref[...], wsp_ref[...])
    y_gdn = _dot(o_ref[...], wgp_ref[...])
    gates = _sigmoid(gate_ref[...] + bm_ref[...])
    merged = gates[:, :d] * y_ssd + gates[:, d:] * y_gdn
    out_ref[...] = h_ref[...] + _dot(merged.astype(BF16), wout_ref[...])


def _merge(y, o, proj, h, b_merge, w_sp, w_gp, w_out, tm):
    n, d = h.shape
    const = lambda i: (0, 0)
    return pl.pallas_call(
        _merge_kernel,
        grid=(n // tm,),
        in_specs=[
            pl.BlockSpec((tm, SSD_D_INNER), lambda i: (i, 0)),
            pl.BlockSpec((tm, GDN_DIM), lambda i: (i, 0)),
            pl.BlockSpec((tm, 2 * d), lambda i: (i, GATE_BLK)),
            pl.BlockSpec((tm, d), lambda i: (i, 0)),
            pl.BlockSpec((1, 2 * d), const),
            pl.BlockSpec((SSD_D_INNER, d), const),
            pl.BlockSpec((GDN_DIM, d), const),
            pl.BlockSpec((d, d), const),
        ],
        out_specs=pl.BlockSpec((tm, d), lambda i: (i, 0)),
        out_shape=jax.ShapeDtypeStruct((n, d), F32),
        compiler_params=pltpu.CompilerParams(
            dimension_semantics=("arbitrary",), vmem_limit_bytes=VMEM_LIMIT),
        name="merge",
    )(y, o, proj, h, b_merge, w_sp, w_gp, w_out)


MOE_PAIRS = MOE_EPG * (MOE_EPG - 1) // 2
MOE_BUCKETS = MOE_GROUPS * MOE_PAIRS
MOE_TILE = 256
RT_BUCKET, RT_WLO, RT_WHI, RT_RANK = 0, 1, 2, 3


def _route(logits):
    lane = lax.broadcasted_iota(jnp.int32, logits.shape, 1)
    neg = -jnp.inf
    big = jnp.int32(LANES)
    is_g = (lane >= RT_G) & (lane < RT_G + MOE_GROUPS)
    gl = jnp.where(is_g, logits, neg)
    gmax = jnp.max(gl, axis=-1, keepdims=True)
    gsel = jnp.min(jnp.where(gl == gmax, lane, big), axis=-1, keepdims=True) - RT_G
    group_w = 1.0 / jnp.sum(jnp.exp(gl - gmax), axis=-1, keepdims=True)
    e_lo = RT_E + gsel * MOE_EPG
    in_grp = (lane >= e_lo) & (lane < e_lo + MOE_EPG)
    el = jnp.where(in_grp, logits, neg)
    m1 = jnp.max(el, axis=-1, keepdims=True)
    i1 = jnp.min(jnp.where(el == m1, lane, big), axis=-1, keepdims=True)
    el2 = jnp.where(lane == i1, neg, el)
    m2 = jnp.max(el2, axis=-1, keepdims=True)
    i2 = jnp.min(jnp.where(el2 == m2, lane, big), axis=-1, keepdims=True)
    z = jnp.sum(jnp.exp(el - m1), axis=-1, keepdims=True)
    p1 = 1.0 / z
    p2 = jnp.exp(m2 - m1) / z
    tot = p1 + p2
    w1 = group_w * (p1 / tot)
    w2 = group_w * (p2 / tot)
    l1 = i1 - e_lo
    l2 = i2 - e_lo
    lo = jnp.minimum(l1, l2)
    hi = jnp.maximum(l1, l2)
    pair = ((lo * (2 * MOE_EPG - 1 - lo)) >> 1) + (hi - lo - 1)
    bucket = (gsel * MOE_PAIRS + pair).astype(F32)
    first_lo = l1 < l2
    w_lo = jnp.where(first_lo, w1, w2)
    w_hi = jnp.where(first_lo, w2, w1)
    return jnp.where(lane == RT_BUCKET, bucket,
                     jnp.where(lane == RT_WLO, w_lo, jnp.where(lane == RT_WHI, w_hi, 0.0)))


def _moe_rmsnorm(x, lnw):
    return x * lax.rsqrt(jnp.mean(x * x, axis=-1, keepdims=True) + EPS) * lnw


def _router_kernel(h_ref, lnw_ref, wr_ref, br_ref, info_ref, counts_ref, run_ref):
    t = h_ref.shape[0]

    @pl.when(pl.program_id(0) == 0)
    def _():
        run_ref[...] = jnp.zeros_like(run_ref)

    hi, lo = _split(_moe_rmsnorm(h_ref[...], lnw_ref[...]))
    whi, wlo = wr_ref[0], wr_ref[1]
    info = _route(_dot(hi, whi) + _dot(lo, whi) + _dot(hi, wlo) + br_ref[...])
    lane = lax.broadcasted_iota(jnp.int32, info.shape, 1)
    onehot = jnp.where(lane.astype(F32) == info[:, RT_BUCKET:RT_BUCKET + 1], 1.0, 0.0)
    row = lax.broadcasted_iota(jnp.int32, (t, t), 0)
    col = lax.broadcasted_iota(jnp.int32, (t, t), 1)
    earlier = _dot(jnp.where(row > col, 1.0, 0.0).astype(BF16), onehot.astype(BF16))
    rank = jnp.sum(onehot * (earlier + run_ref[...]), axis=-1, keepdims=True)
    run_ref[...] += jnp.sum(onehot, axis=0, keepdims=True)
    info_ref[...] = jnp.where(lane == RT_RANK, rank, info)
    counts_ref[...] = run_ref[...]


def _dispatch_kernel(pos_ref, h_ref, info_ref, xs_in_hbm, xs_hbm, buf_ref, sem):
    del xs_in_hbm
    tm, d = h_ref.shape
    base = pl.program_id(0) * tm
    buf_ref[:, :d] = h_ref[...]
    buf_ref[:, d:] = info_ref[...]

    def issue(r, carry):
        pltpu.make_async_copy(buf_ref.at[pl.ds(r, 1)], xs_hbm.at[pl.ds(pos_ref[base + r], 1)], sem).start()
        return carry

    lax.fori_loop(0, tm, issue, 0, unroll=8)
    pltpu.make_async_copy(buf_ref, xs_hbm.at[pl.ds(0, tm)], sem).wait()


def _experts_kernel(lo_ref, hi_ref, nt_ref, x_ref, lnw_ref, wg_lo, wu_lo, wd_lo, wg_hi, wu_hi, wd_hi, y_ref):
    live = pl.program_id(0) < nt_ref[0]
    d = y_ref.shape[1]

    @pl.when(live)
    def _():
        xn = _moe_rmsnorm(x_ref[:, :d], lnw_ref[...]).astype(BF16)
        wts = x_ref[:, d + RT_WLO:d + RT_WHI + 1]
        y = None
        for j, (wg, wu, wd) in enumerate(((wg_lo, wu_lo, wd_lo), (wg_hi, wu_hi, wd_hi))):
            hid = _silu(_dot(xn, wg[0])) * _dot(xn, wu[0]) * wts[:, j:j + 1]
            part = _dot(hid.astype(BF16), wd[0])
            y = part if y is None else y + part
        y_ref[...] = x_ref[:, :d] + y

    @pl.when(jnp.logical_not(live))
    def _():
        y_ref[...] = jnp.zeros_like(y_ref)


def _combine_kernel(pos_ref, y_hbm, fnw_ref, out_ref, buf_ref, sem, *, final_norm):
    tm = out_ref.shape[0]
    base = pl.program_id(0) * tm

    def issue(r, carry):
        pltpu.make_async_copy(y_hbm.at[pl.ds(pos_ref[base + r], 1)], buf_ref.at[pl.ds(r, 1)], sem).start()
        return carry

    lax.fori_loop(0, tm, issue, 0, unroll=8)
    pltpu.make_async_copy(y_hbm.at[pl.ds(0, tm)], buf_ref, sem).wait()
    y = buf_ref[...]
    if final_norm:
        y = _moe_rmsnorm(y, fnw_ref[...])
    out_ref[...] = y


def _moe(h, lnw, w_router, b_router, w_gate, w_up, w_down, fnw, tm, final_norm, out_order):
    n, d = h.shape
    ne, _, dff = w_gate.shape
    tile = MOE_TILE
    n_tiles = n // tile + MOE_BUCKETS
    p = n_tiles * tile
    const = lambda i: (0, 0)
    info, counts = pl.pallas_call(
        _router_kernel,
        grid=(n // tm,),
        in_specs=[
            pl.BlockSpec((tm, d), lambda i: (i, 0)),
            pl.BlockSpec((1, d), const),
            pl.BlockSpec((2, d, LANES), lambda i: (0, 0, 0)),
            pl.BlockSpec((1, LANES), const),
        ],
        out_specs=[pl.BlockSpec((tm, LANES), lambda i: (i, 0)), pl.BlockSpec((1, LANES), const)],
        out_shape=[jax.ShapeDtypeStruct((n, LANES), F32), jax.ShapeDtypeStruct((1, LANES), F32)],
        scratch_shapes=[pltpu.VMEM((1, LANES), F32)],
        compiler_params=pltpu.CompilerParams(dimension_semantics=("arbitrary",), vmem_limit_bytes=VMEM_LIMIT),
        name="moe_router",
    )(h, lnw, w_router, b_router)

    tiles_per = (counts[0, :MOE_BUCKETS].astype(jnp.int32) + tile - 1) // tile
    tile_end = jnp.cumsum(tiles_per)
    first_slot = ((tile_end - tiles_per) * tile).astype(F32)
    bucket_is = info[:, RT_BUCKET:RT_BUCKET + 1] == jnp.arange(MOE_BUCKETS, dtype=F32)[None, :]
    pos = (jnp.sum(jnp.where(bucket_is, first_slot[None, :], 0.0), axis=1) + info[:, RT_RANK]).astype(jnp.int32)
    n_live = tile_end[-1]
    tile_bucket = jnp.searchsorted(tile_end, jnp.minimum(jnp.arange(n_tiles, dtype=jnp.int32), n_live - 1),
                                   side="right").astype(jnp.int32)
    pairs = [(a, b) for a in range(MOE_EPG) for b in range(a + 1, MOE_EPG)]
    grp = tile_bucket // MOE_PAIRS
    tile_lo = grp * MOE_EPG + jnp.array([a for a, _ in pairs], jnp.int32)[tile_bucket % MOE_PAIRS]
    tile_hi = grp * MOE_EPG + jnp.array([b for _, b in pairs], jnp.int32)[tile_bucket % MOE_PAIRS]

    x_sorted = pl.pallas_call(
        _dispatch_kernel,
        grid_spec=pltpu.PrefetchScalarGridSpec(
            num_scalar_prefetch=1, grid=(n // tile,),
            in_specs=[
                pl.BlockSpec((tile, d), lambda i, pos: (i, 0)),
                pl.BlockSpec((tile, LANES), lambda i, pos: (i, 0)),
                pl.BlockSpec(memory_space=pl.ANY),
            ],
            out_specs=pl.BlockSpec(memory_space=pl.ANY),
            scratch_shapes=[pltpu.VMEM((tile, d + LANES), F32), pltpu.SemaphoreType.DMA(())]),
        out_shape=jax.ShapeDtypeStruct((p, d + LANES), F32),
        input_output_aliases={3: 0},
        compiler_params=pltpu.CompilerParams(dimension_semantics=("arbitrary",), vmem_limit_bytes=VMEM_LIMIT),
        name="moe_dispatch",
    )(pos, h, info, jnp.zeros((p, d + LANES), F32))

    lo_map = lambda t, lo, hi, nt: (lo[t], 0, 0)
    hi_map = lambda t, lo, hi, nt: (hi[t], 0, 0)
    y_sorted = pl.pallas_call(
        _experts_kernel,
        grid_spec=pltpu.PrefetchScalarGridSpec(
            num_scalar_prefetch=3, grid=(n_tiles,),
            in_specs=[
                pl.BlockSpec((tile, d + LANES), lambda t, lo, hi, nt: (t, 0)),
                pl.BlockSpec((1, d), lambda t, lo, hi, nt: (0, 0)),
                pl.BlockSpec((1, d, dff), lo_map),
                pl.BlockSpec((1, d, dff), lo_map),
                pl.BlockSpec((1, dff, d), lo_map),
                pl.BlockSpec((1, d, dff), hi_map),
                pl.BlockSpec((1, d, dff), hi_map),
                pl.BlockSpec((1, dff, d), hi_map),
            ],
            out_specs=pl.BlockSpec((tile, d), lambda t, lo, hi, nt: (t, 0))),
        out_shape=jax.ShapeDtypeStruct((p, d), F32),
        compiler_params=pltpu.CompilerParams(dimension_semantics=("arbitrary",), vmem_limit_bytes=VMEM_LIMIT),
        name="moe_experts",
    )(tile_lo, tile_hi, n_live.reshape(1).astype(jnp.int32), x_sorted, lnw,
      w_gate, w_up, w_down, w_gate, w_up, w_down)

    return pl.pallas_call(
        functools.partial(_combine_kernel, final_norm=final_norm),
        grid_spec=pltpu.PrefetchScalarGridSpec(
            num_scalar_prefetch=1, grid=(n // tile,),
            in_specs=[
                pl.BlockSpec(memory_space=pl.ANY),
                pl.BlockSpec((1, d), lambda i, pos: (0, 0)),
            ],
            out_specs=pl.BlockSpec((tile, d), lambda i, pos: (i, 0)),
            scratch_shapes=[pltpu.VMEM((tile, d), F32), pltpu.SemaphoreType.DMA(())]),
        out_shape=jax.ShapeDtypeStruct((n, d), F32),
        compiler_params=pltpu.CompilerParams(dimension_semantics=("arbitrary",), vmem_limit_bytes=VMEM_LIMIT),
        name="moe_combine",
    )(pos if out_order is None else out_order(pos), y_sorted, fnw)


def _lane_row(v, offset):
    return jnp.zeros((1, LANES), F32).at[0, offset:offset + v.shape[0]].set(v.astype(F32))


def _expansion(offset, heads, width):
    r = jnp.arange(LANES)[:, None]
    c = jnp.arange(heads * width)[None, :]
    return (r == offset + c // width).astype(BF16)


def _row_tile(n, want):
    t = min(n, want)
    while n % t:
        t //= 2
    return t


def kernel(x, ln1_w, w_in, ssd_conv_w, ssd_conv_b, ssd_dt_bias, ssd_a_log, ssd_d, ssd_norm_w, w_ssd_proj,
           gdn_conv_w, gdn_dt_bias, gdn_a_log, gdn_norm_w, w_gdn_proj, b_merge, w_out, ln2_w,
           moe_w_group, moe_b_group, moe_w_expert, moe_b_expert, moe_w_gate, moe_w_up, moe_w_down,
           final_norm_w):
    bsz, seqlen, d = x.shape
    n = bsz * seqlen
    depth = w_in.shape[0]

    def swap_order(a):
        blocks = a.reshape((n // TIME_BLOCK, SUBLANES, SUBLANES) + a.shape[1:])
        return blocks.swapaxes(1, 2).reshape(a.shape)

    h = swap_order(x.reshape(n, d))

    o1 = SSD_D_INNER
    o2 = o1 + SSD_XBC
    o3 = o2 + SSD_HEADS
    o4 = o3 + 3 * GDN_DIM
    o5 = o4 + GDN_HEADS
    o6 = o5 + GDN_HEADS
    o7 = o6 + GDN_DIM

    e_ssd = _expansion(SM_DT, SSD_HEADS, SSD_HEAD_DIM)
    e_gdn_a = _expansion(SM_A, GDN_HEADS, GDN_HEAD_DIM)
    e_gdn_b = _expansion(SM_BETA, GDN_HEADS, GDN_HEAD_DIM)
    tm_proj = _row_tile(n, 1024)
    tm_merge = _row_tile(n, 512)
    tm_moe = _row_tile(n, 512)

    for i in range(depth):
        wi = w_in[i]
        w_main = jnp.concatenate(
            [wi[:, o1:o2], wi[:, o3:o4], wi[:, :o1], wi[:, o7:], wi[:, o6:o7]], axis=1).astype(BF16)
        w_small = jnp.concatenate(
            [wi[:, o2:o3], wi[:, o4:o5], wi[:, o5:o6],
             jnp.zeros((d, LANES - SSD_HEADS - 2 * GDN_HEADS), F32)], axis=1).astype(BF16)
        proj, small = _in_proj(h, ln1_w[i][None, :], w_main, w_small, tm_proj, 1024)

        y = _ssd(proj, small, ssd_conv_w[i], ssd_conv_b[i][None, :],
                 _lane_row(ssd_dt_bias[i], SM_DT), _lane_row(ssd_a_log[i], SM_DT),
                 jnp.repeat(ssd_d[i], SSD_HEAD_DIM)[None, :], ssd_norm_w[i][None, :], e_ssd, bsz, seqlen)
        o = _gdn(proj, small, gdn_conv_w[i], _lane_row(gdn_dt_bias[i], SM_A), _lane_row(gdn_a_log[i], SM_A),
                 gdn_norm_w[i][None, :], e_gdn_a, e_gdn_b, bsz, seqlen)
        h = _merge(y, o, proj, h, b_merge[i][None, :], w_ssd_proj[i].astype(BF16), w_gdn_proj[i].astype(BF16),
                   w_out[i].astype(BF16), tm_merge)

        w_r = jnp.concatenate(
            [moe_w_group[i], moe_w_expert[i], jnp.zeros((d, LANES - MOE_GROUPS - MOE_EXPERTS), F32)], axis=1)
        w_r_hi = w_r.astype(BF16)
        w_r_lo = (w_r - w_r_hi.astype(F32)).astype(BF16)
        b_r = jnp.concatenate(
            [moe_b_group[i], moe_b_expert[i], jnp.zeros((LANES - MOE_GROUPS - MOE_EXPERTS,), F32)])[None, :]
        h = _moe(h, ln2_w[i][None, :], jnp.stack([w_r_hi, w_r_lo]), b_r,
                 moe_w_gate[i].astype(BF16), moe_w_up[i].astype(BF16), moe_w_down[i].astype(BF16),
                 final_norm_w[None, :], tm_moe, final_norm=(i == depth - 1),
                 out_order=swap_order if i == depth - 1 else None)
    return h.reshape(bsz, seqlen, d)
```

```python
import functools

import jax
import jax.numpy as jnp
from jax import lax
from jax.experimental import pallas as pl
from jax.experimental.pallas import tpu as pltpu

F32 = jnp.float32
BF16 = jnp.bfloat16

EPS = 1e-6
CONV_K = 4
LANES = 128
SUBLANES = 8
TIME_BLOCK = SUBLANES * SUBLANES
HALO = (CONV_K - 1) * SUBLANES

D_MODEL = 1024
SSD_D_INNER = 2 * D_MODEL
SSD_HEAD_DIM = 64
SSD_HEADS = SSD_D_INNER // SSD_HEAD_DIM
SSD_GROUPS = 4
SSD_STATE = 128
SSD_CHUNK = 128
SSD_XBC = SSD_D_INNER + 2 * SSD_GROUPS * SSD_STATE
GDN_HEADS = 8
GDN_HEAD_DIM = 128
GDN_DIM = GDN_HEADS * GDN_HEAD_DIM
GDN_CHUNK = 64
MOE_GROUPS = 4
MOE_EPG = 4
MOE_EXPERTS = MOE_GROUPS * MOE_EPG
MOE_D_FF = 512

MAIN_COLS = SSD_XBC + 3 * GDN_DIM + SSD_D_INNER + 2 * D_MODEL + GDN_DIM
XBC_BLK, QKV_BLK = 0, 1
ZS_BLK, GATE_BLK = 3, 4
ZG_BLK = 10
SM_DT, SM_A, SM_BETA = 0, SSD_HEADS, SSD_HEADS + GDN_HEADS
RT_G, RT_E = 0, MOE_GROUPS

VMEM_LIMIT = 56 * 1024 * 1024


def _dot(a, b):
    return jnp.dot(a, b, preferred_element_type=F32)


def _dot_nt(a, b):
    return lax.dot_general(a, b, (((1,), (1,)), ((), ())), preferred_element_type=F32)


def _split(x):
    hi = x.astype(BF16)
    lo = (x - hi.astype(F32)).astype(BF16)
    return hi, lo


def _dot_split_lhs(x, w):
    hi, lo = _split(x)
    return _dot(hi, w) + _dot(lo, w)


def _dot_split_rhs(w, x):
    hi, lo = _split(x)
    return _dot(w, hi) + _dot(w, lo)


def _sigmoid(x):
    return 1.0 / (1.0 + jnp.exp(-x))


def _silu(x):
    return x * _sigmoid(x)


def _softplus(x):
    return jnp.maximum(x, 0.0) + jnp.log1p(jnp.exp(-jnp.abs(x)))


def _time_index(r):
    return (r & ~(TIME_BLOCK - 1)) | ((r >> 3) & (SUBLANES - 1)) | ((r & (SUBLANES - 1)) << 3)


def _causal_conv(cur, tail_ref, w, first):
    t, c = cur.shape

    @pl.when(first)
    def _():
        tail_ref[...] = jnp.zeros_like(tail_ref)

    sub = lax.broadcasted_iota(jnp.int32, (SUBLANES, c), 0)
    prev_tail = tail_ref[...]
    outs = []
    for i in range(t // TIME_BLOCK):
        blk = cur[i * TIME_BLOCK:(i + 1) * TIME_BLOCK, :]
        tail = blk[TIME_BLOCK - HALO:, :]
        wrapped = [
            jnp.where(sub == 0,
                      pltpu.roll(prev_tail[j * SUBLANES:(j + 1) * SUBLANES, :], 1, axis=0),
                      pltpu.roll(tail[j * SUBLANES:(j + 1) * SUBLANES, :], 1, axis=0))
            for j in range(CONV_K - 1)]
        ext = jnp.concatenate(wrapped + [blk], axis=0)
        acc = None
        for k in range(CONV_K):
            term = ext[k * SUBLANES:k * SUBLANES + TIME_BLOCK, :] * w[k:k + 1, :]
            acc = term if acc is None else acc + term
        outs.append(acc)
        prev_tail = tail
    tail_ref[...] = prev_tail
    return jnp.concatenate(outs, axis=0)


def _inproj_kernel(x_ref, lnw_ref, w_ref, ws_ref, o_ref, os_ref, xn_ref):
    @pl.when(pl.program_id(1) == 0)
    def _():
        x = x_ref[...]
        xn = x * lax.rsqrt(jnp.mean(x * x, axis=-1, keepdims=True) + EPS) * lnw_ref[...]
        xn_ref[...] = xn.astype(BF16)
        os_ref[...] = _dot(xn_ref[...], ws_ref[...])

    o_ref[...] = _dot(xn_ref[...], w_ref[...]).astype(o_ref.dtype)


def _in_proj(x, lnw, w_main, w_small, tm, tn):
    n, d = x.shape
    nm = w_main.shape[1]
    return pl.pallas_call(
        _inproj_kernel,
        grid=(n // tm, nm // tn),
        in_specs=[
            pl.BlockSpec((tm, d), lambda i, j: (i, 0)),
            pl.BlockSpec((1, d), lambda i, j: (0, 0)),
            pl.BlockSpec((d, tn), lambda i, j: (0, j)),
            pl.BlockSpec((d, LANES), lambda i, j: (0, 0)),
        ],
        out_specs=[
            pl.BlockSpec((tm, tn), lambda i, j: (i, j)),
            pl.BlockSpec((tm, LANES), lambda i, j: (i, 0)),
        ],
        out_shape=[jax.ShapeDtypeStruct((n, nm), BF16), jax.ShapeDtypeStruct((n, LANES), F32)],
        scratch_shapes=[pltpu.VMEM((tm, d), BF16)],
        compiler_params=pltpu.CompilerParams(
            dimension_semantics=("arbitrary", "arbitrary"), vmem_limit_bytes=VMEM_LIMIT),
        name="in_proj",
    )(x, lnw, w_main, w_small)


def _ssd_kernel(xbc_ref, z_ref, sm_ref, convw_ref, convb_ref, dtb_ref, alog_ref, dexp_ref, normw_ref, e_ref,
                y_ref, halo_ref, state_ref, yacc_ref):
    q = SSD_CHUNK
    p = SSD_HEAD_DIM
    ns = SSD_STATE
    gw = (SSD_HEADS // SSD_GROUPS) * p
    first = pl.program_id(1) == 0

    @pl.when(first)
    def _():
        state_ref[...] = jnp.zeros_like(state_ref)

    conv = _causal_conv(xbc_ref[...].astype(F32), halo_ref, convw_ref[...], first) + convb_ref[...]
    xbc = _silu(conv)
    xs = xbc[:, :SSD_D_INNER]
    bm = xbc[:, SSD_D_INNER:SSD_D_INNER + SSD_GROUPS * ns]
    cm = xbc[:, SSD_D_INNER + SSD_GROUPS * ns:]

    dt = _softplus(sm_ref[...] + dtb_ref[...])
    d_a = dt * (-jnp.exp(alog_ref[...]))
    row = lax.broadcasted_iota(jnp.int32, (q, q), 0)
    col = lax.broadcasted_iota(jnp.int32, (q, q), 1)
    causal = _time_index(row) >= _time_index(col)
    tri = jnp.where(causal, 1.0, 0.0).astype(BF16)
    a_cs = _dot_split_rhs(tri, d_a)
    a_cs_t = a_cs.T
    e_acs = jnp.exp(a_cs)
    dec_end = jnp.exp(a_cs[q - 1:q, :] - a_cs)
    ex = _dot_split_lhs(jnp.concatenate([dt, dt * dec_end, e_acs], axis=0), e_ref[...])
    dt_x, dtdec_x, eacs_x = ex[:q], ex[q:2 * q], ex[2 * q:]

    xdt_b = (xs * dt_x).astype(BF16)
    xdd_b = (xs * dtdec_x).astype(BF16)
    lane = lax.broadcasted_iota(jnp.int32, (q, LANES), 1)
    for g in range(SSD_GROUPS):
        bg = bm[:, g * ns:(g + 1) * ns]
        cg_b = cm[:, g * ns:(g + 1) * ns].astype(BF16)
        scores = _dot_nt(cg_b, bg.astype(BF16))
        s_g = state_ref[:, g * gw:(g + 1) * gw]
        yacc_ref[:, g * gw:(g + 1) * gw] = _dot(cg_b, s_g.astype(BF16)) * eacs_x[:, g * gw:(g + 1) * gw]
        new_g = _dot(bg.T.astype(BF16), xdd_b[:, g * gw:(g + 1) * gw])
        state_ref[:, g * gw:(g + 1) * gw] = s_g * eacs_x[q - 1:q, g * gw:(g + 1) * gw] + new_g
        for j in range(gw // LANES):
            h0 = (g * gw + j * LANES) // p
            c0 = g * gw + j * LANES
            ms = []
            for h in (h0, h0 + 1):
                seg = a_cs[:, h:h + 1] - a_cs_t[h:h + 1, :]
                ms.append((scores * jnp.exp(jnp.where(causal, seg, -jnp.inf))).astype(BF16))
            xp = xdt_b[:, c0:c0 + LANES]
            zero = jnp.zeros_like(xp)
            rhs = jnp.concatenate([jnp.where(lane < p, xp, zero), jnp.where(lane >= p, xp, zero)], axis=0)
            yacc_ref[:, c0:c0 + LANES] += _dot(jnp.concatenate(ms, axis=1), rhs)

    y = yacc_ref[...] + dexp_ref[...] * xs
    y = y * _silu(z_ref[...].astype(F32))
    y = y * lax.rsqrt(jnp.mean(y * y, axis=-1, keepdims=True) + EPS) * normw_ref[...]
    y_ref[...] = y.astype(BF16)


def _ssd(proj, small, convw, convb, dtb, alog, dexp, normw, expand, bsz, seqlen):
    q = SSD_CHUNK
    nc = seqlen // q
    n = bsz * seqlen
    rowmap = lambda b, c: b * nc + c
    const = lambda b, c: (0, 0)
    return pl.pallas_call(
        _ssd_kernel,
        grid=(bsz, nc),
        in_specs=[
            pl.BlockSpec((q, SSD_XBC), lambda b, c: (rowmap(b, c), XBC_BLK)),
            pl.BlockSpec((q, SSD_D_INNER), lambda b, c: (rowmap(b, c), ZS_BLK)),
            pl.BlockSpec((q, LANES), lambda b, c: (rowmap(b, c), 0)),
            pl.BlockSpec((CONV_K, SSD_XBC), const),
            pl.BlockSpec((1, SSD_XBC), const),
            pl.BlockSpec((1, LANES), const),
            pl.BlockSpec((1, LANES), const),
            pl.BlockSpec((1, SSD_D_INNER), const),
            pl.BlockSpec((1, SSD_D_INNER), const),
            pl.BlockSpec((LANES, SSD_D_INNER), const),
        ],
        out_specs=pl.BlockSpec((q, SSD_D_INNER), lambda b, c: (rowmap(b, c), 0)),
        out_shape=jax.ShapeDtypeStruct((n, SSD_D_INNER), BF16),
        scratch_shapes=[
            pltpu.VMEM((HALO, SSD_XBC), F32),
            pltpu.VMEM((SSD_STATE, SSD_D_INNER), F32),
            pltpu.VMEM((q, SSD_D_INNER), F32),
        ],
        compiler_params=pltpu.CompilerParams(
            dimension_semantics=("arbitrary", "arbitrary"), vmem_limit_bytes=VMEM_LIMIT),
        name="ssd",
    )(proj, proj, small, convw, convb, dtb, alog, dexp, normw, expand)


GDN_PREP_ROWS = 2 * GDN_CHUNK
GDN_CHUNK_SHIFT = GDN_CHUNK.bit_length() - 1


def _unit_lower_inverse_many(ms, row, col):
    c = ms[0].shape[0]
    diff = row ^ col
    eye = jnp.where(row == col, 1.0, 0.0)
    ts = [eye - jnp.where((diff >> 1) == 0, m, 0.0) for m in ms]
    shift = 1
    while (1 << shift) < c:
        sel = (diff >> shift) == 1
        tbs = [t.astype(BF16) for t in ts]
        xs = [_dot(tb, jnp.where(sel, m, 0.0).astype(BF16)) for tb, m in zip(tbs, ms)]
        ts = [t - _dot(x.astype(BF16), tb) for t, x, tb in zip(ts, xs, tbs)]
        shift += 1
    return ts


def _gdn_prep_kernel(qkv_ref, sm_ref, convw_ref, dtb_ref, alog_ref, eg_ref, eb_ref,
                     u_ref, w_ref, qd_ref, aqk_ref, kdt_ref, gl_ref, halo_ref):
    c = GDN_CHUNK
    dh = GDN_HEAD_DIM
    t = GDN_PREP_ROWS
    ncb = t // c
    first = pl.program_id(1) == 0
    qkv = _silu(_causal_conv(qkv_ref[...].astype(F32), halo_ref, convw_ref[...], first))

    sm = sm_ref[...]
    lane1 = lax.broadcasted_iota(jnp.int32, (t, LANES), 1)
    is_a = (lane1 >= SM_A) & (lane1 < SM_A + GDN_HEADS)
    log_decay = jnp.where(is_a, -jnp.exp(alog_ref[...]) * _softplus(sm + dtb_ref[...]), 0.0)
    beta = _sigmoid(sm)
    row_t = lax.broadcasted_iota(jnp.int32, (t, t), 0)
    col_t = lax.broadcasted_iota(jnp.int32, (t, t), 1)
    same_chunk = (row_t >> GDN_CHUNK_SHIFT) == (col_t >> GDN_CHUNK_SHIFT)
    tri = jnp.where(same_chunk, jnp.where(_time_index(row_t) >= _time_index(col_t), 1.0, 0.0), 0.0).astype(BF16)
    gc = _dot_split_rhs(tri, log_decay)
    gc_t = gc.T
    g_last = jnp.concatenate(
        [jnp.broadcast_to(gc[(i + 1) * c - 1:(i + 1) * c, :], (c, LANES)) for i in range(ncb)], axis=0)
    ex = _dot_split_lhs(jnp.concatenate([jnp.exp(gc), jnp.exp(g_last - gc)], axis=0), eg_ref[...])
    egc_x, erev_x = ex[:t], ex[t:]
    beta_x = _dot_split_lhs(beta, eb_ref[...])

    row = _time_index(lax.broadcasted_iota(jnp.int32, (c, c), 0))
    col = _time_index(lax.broadcasted_iota(jnp.int32, (c, c), 1))
    incl = row >= col
    strict = row > col

    qs, ks, kbs, rhs = [], [], [], []
    for h in range(GDN_HEADS):
        hs = slice(h * dh, (h + 1) * dh)
        qh = qkv[:, h * dh:(h + 1) * dh]
        kh = qkv[:, GDN_DIM + h * dh:GDN_DIM + (h + 1) * dh]
        vh = qkv[:, 2 * GDN_DIM + h * dh:2 * GDN_DIM + (h + 1) * dh]
        qh = qh * lax.rsqrt(jnp.sum(qh * qh, axis=-1, keepdims=True) + EPS) * (dh ** -0.5)
        kh = kh * lax.rsqrt(jnp.sum(kh * kh, axis=-1, keepdims=True) + EPS)
        kb = kh * beta_x[:, hs]
        qs.append(qh)
        ks.append(kh)
        kbs.append(kb)
        rhs.append(jnp.concatenate([vh * beta_x[:, hs], kb * egc_x[:, hs]], axis=1).astype(BF16))

    items = [(i, h) for i in range(ncb) for h in range(GDN_HEADS)]
    ms, aqks = [], []
    for i, h in items:
        rs = slice(i * c, (i + 1) * c)
        la = SM_A + h
        decay = jnp.exp(jnp.where(incl, gc[rs, la:la + 1] - gc_t[la:la + 1, rs], -jnp.inf))
        nt = _dot_nt(jnp.concatenate([kbs[h][rs], qs[h][rs]], axis=0).astype(BF16), ks[h][rs].astype(BF16))
        ms.append(jnp.where(strict, nt[:c] * decay, 0.0))
        aqks.append((nt[c:] * decay).astype(BF16))
    t_invs = _unit_lower_inverse_many(ms, row, col)
    uws = [_dot(t_inv.astype(BF16), rhs[h][i * c:(i + 1) * c]) for (i, h), t_inv in zip(items, t_invs)]

    def by_chunk(pieces):
        return jnp.concatenate(
            [jnp.concatenate(pieces[i * GDN_HEADS:(i + 1) * GDN_HEADS], axis=1) for i in range(ncb)], axis=0)

    u_ref[...] = by_chunk([uw[:, :dh] for uw in uws]).astype(BF16)
    w_ref[...] = by_chunk([uw[:, dh:] for uw in uws]).astype(BF16)
    aqk_ref[...] = by_chunk(aqks)
    qd_ref[...] = jnp.concatenate(
        [qs[h] * egc_x[:, h * dh:(h + 1) * dh] for h in range(GDN_HEADS)], axis=1).astype(BF16)
    for i in range(ncb):
        rs = slice(i * c, (i + 1) * c)
        kdt_ref[i] = jnp.concatenate(
            [(ks[h][rs] * erev_x[rs, h * dh:(h + 1) * dh]).T for h in range(GDN_HEADS)], axis=1).astype(BF16)
        gl_ref[i] = egc_x[(i + 1) * c - 1:(i + 1) * c, :]


def _gdn_scan_kernel(u_ref, w_ref, qd_ref, aqk_ref, kdt_ref, gl_ref, zg_ref, normw_ref, o_ref, state_ref):
    c = GDN_CHUNK
    dh = GDN_HEAD_DIM

    @pl.when(pl.program_id(0) == 0)
    def _():
        state_ref[...] = jnp.zeros_like(state_ref)

    zero = jnp.zeros((c, dh), BF16)
    for b in range(u_ref.shape[0]):
        outs = []
        for pr in range(GDN_HEADS // 2):
            heads = (2 * pr, 2 * pr + 1)
            ss, q_s, v_news = [], [], []
            for h in heads:
                hs = slice(h * dh, (h + 1) * dh)
                s = state_ref[b, h]
                p = _dot(jnp.concatenate([w_ref[b, :, hs], qd_ref[b, :, hs]], axis=0), s.astype(BF16))
                ss.append(s)
                q_s.append(p[c:])
                v_news.append((u_ref[b, :, hs].astype(F32) - p[:c]).astype(BF16))
            lhs = jnp.concatenate([aqk_ref[b, :, pr * dh:(pr + 1) * dh], kdt_ref[b, 0, :, pr * dh:(pr + 1) * dh]],
                                  axis=0)
            bd = jnp.concatenate([jnp.concatenate([v_news[0], zero], axis=1),
                                  jnp.concatenate([zero, v_news[1]], axis=1)], axis=0)
            r = _dot(lhs, bd)
            for j, h in enumerate(heads):
                hs = slice(h * dh, (h + 1) * dh)
                state_ref[b, h] = ss[j] * gl_ref[b, 0, :, hs] + r[c:, j * dh:(j + 1) * dh]
                o = q_s[j] + r[:c, j * dh:(j + 1) * dh]
                o = o * lax.rsqrt(jnp.mean(o * o, axis=-1, keepdims=True) + EPS) * normw_ref[...]
                outs.append(o * _silu(zg_ref[b, :, hs].astype(F32)))
        o_ref[b] = jnp.concatenate(outs, axis=1).astype(BF16)


def _gdn(proj, small, convw, dtb, alog, normw, expand_a, expand_b, bsz, seqlen):
    c = GDN_CHUNK
    t = GDN_PREP_ROWS
    nc = seqlen // c
    nb = seqlen // t
    n = bsz * seqlen
    hh = GDN_HEADS * c
    rowmap = lambda b, i: b * nb + i
    const = lambda b, i: (0, 0)
    params = pltpu.CompilerParams(dimension_semantics=("arbitrary", "arbitrary"), vmem_limit_bytes=VMEM_LIMIT)
    u, w, qd, aqk, kdt, gl = pl.pallas_call(
        _gdn_prep_kernel,
        grid=(bsz, nb),
        in_specs=[
            pl.BlockSpec((t, 3 * GDN_DIM), lambda b, i: (rowmap(b, i), QKV_BLK)),
            pl.BlockSpec((t, LANES), lambda b, i: (rowmap(b, i), 0)),
            pl.BlockSpec((CONV_K, 3 * GDN_DIM), const),
            pl.BlockSpec((1, LANES), const),
            pl.BlockSpec((1, LANES), const),
            pl.BlockSpec((LANES, GDN_DIM), const),
            pl.BlockSpec((LANES, GDN_DIM), const),
        ],
        out_specs=[
            pl.BlockSpec((t, GDN_DIM), lambda b, i: (rowmap(b, i), 0)),
            pl.BlockSpec((t, GDN_DIM), lambda b, i: (rowmap(b, i), 0)),
            pl.BlockSpec((t, GDN_DIM), lambda b, i: (rowmap(b, i), 0)),
            pl.BlockSpec((t, hh), lambda b, i: (rowmap(b, i), 0)),
            pl.BlockSpec((t // c, GDN_HEAD_DIM, hh), lambda b, i: (rowmap(b, i), 0, 0)),
            pl.BlockSpec((t // c, 1, GDN_DIM), lambda b, i: (rowmap(b, i), 0, 0)),
        ],
        out_shape=[
            jax.ShapeDtypeStruct((n, GDN_DIM), BF16),
            jax.ShapeDtypeStruct((n, GDN_DIM), BF16),
            jax.ShapeDtypeStruct((n, GDN_DIM), BF16),
            jax.ShapeDtypeStruct((n, hh), BF16),
            jax.ShapeDtypeStruct((n // c, GDN_HEAD_DIM, hh), BF16),
            jax.ShapeDtypeStruct((n // c, 1, GDN_DIM), F32),
        ],
        scratch_shapes=[pltpu.VMEM((HALO, 3 * GDN_DIM), F32)],
        compiler_params=params,
        name="gdn_prep",
    )(proj, small, convw, dtb, alog, expand_a, expand_b)

    seq3 = lambda i: (0, i, 0)
    seq4 = lambda i: (0, i, 0, 0)
    o = pl.pallas_call(
        _gdn_scan_kernel,
        grid=(nc,),
        in_specs=[
            pl.BlockSpec((bsz, c, GDN_DIM), seq3),
            pl.BlockSpec((bsz, c, GDN_DIM), seq3),
            pl.BlockSpec((bsz, c, GDN_DIM), seq3),
            pl.BlockSpec((bsz, c, hh), seq3),
            pl.BlockSpec((bsz, 1, GDN_HEAD_DIM, hh), seq4),
            pl.BlockSpec((bsz, 1, 1, GDN_DIM), seq4),
            pl.BlockSpec((bsz, c, GDN_DIM), lambda i: (0, i, ZG_BLK)),
            pl.BlockSpec((1, GDN_HEAD_DIM), lambda i: (0, 0)),
        ],
        out_specs=pl.BlockSpec((bsz, c, GDN_DIM), seq3),
        out_shape=jax.ShapeDtypeStruct((bsz, seqlen, GDN_DIM), BF16),
        scratch_shapes=[pltpu.VMEM((bsz, GDN_HEADS, GDN_HEAD_DIM, GDN_HEAD_DIM), F32)],
        compiler_params=pltpu.CompilerParams(dimension_semantics=("arbitrary",), vmem_limit_bytes=VMEM_LIMIT),
        name="gdn_scan",
    )(u.reshape(bsz, seqlen, GDN_DIM), w.reshape(bsz, seqlen, GDN_DIM), qd.reshape(bsz, seqlen, GDN_DIM),
      aqk.reshape(bsz, seqlen, hh), kdt.reshape(bsz, nc, GDN_HEAD_DIM, hh), gl.reshape(bsz, nc, 1, GDN_DIM),
      proj.reshape(bsz, seqlen, MAIN_COLS), normw)
    return o.reshape(n, GDN_DIM)


def _merge_kernel(y_ref, o_ref, gate_ref, h_ref, bm_ref, wsp_ref, wgp_ref, wout_ref, out_ref):
    d = h_ref.shape[1]
    y_ssd = _dot(y_ref[...], wsp_ref[...])
    y_gdn = _dot(o_ref[...], wgp_ref[...])
    gates = _sigmoid(gate_ref[...].astype(F32) + bm_ref[...])
    merged = gates[:, :d] * y_ssd + gates[:, d:] * y_gdn
    out_ref[...] = h_ref[...] + _dot(merged.astype(BF16), wout_ref[...])


def _merge(y, o, proj, h, b_merge, w_sp, w_gp, w_out, tm):
    n, d = h.shape
    const = lambda i: (0, 0)
    return pl.pallas_call(
        _merge_kernel,
        grid=(n // tm,),
        in_specs=[
            pl.BlockSpec((tm, SSD_D_INNER), lambda i: (i, 0)),
            pl.BlockSpec((tm, GDN_DIM), lambda i: (i, 0)),
            pl.BlockSpec((tm, 2 * d), lambda i: (i, GATE_BLK)),
            pl.BlockSpec((tm, d), lambda i: (i, 0)),
            pl.BlockSpec((1, 2 * d), const),
            pl.BlockSpec((SSD_D_INNER, d), const),
            pl.BlockSpec((GDN_DIM, d), const),
            pl.BlockSpec((d, d), const),
        ],
        out_specs=pl.BlockSpec((tm, d), lambda i: (i, 0)),
        out_shape=jax.ShapeDtypeStruct((n, d), F32),
        compiler_params=pltpu.CompilerParams(
            dimension_semantics=("arbitrary",), vmem_limit_bytes=VMEM_LIMIT),
        name="merge",
    )(y, o, proj, h, b_merge, w_sp, w_gp, w_out)


MOE_PAIRS = MOE_EPG * (MOE_EPG - 1) // 2
MOE_BUCKETS = MOE_GROUPS * MOE_PAIRS
MOE_TILE = 256
RT_BUCKET, RT_WLO, RT_WHI, RT_RANK = 0, 1, 2, 3


def _route(logits):
    lane = lax.broadcasted_iota(jnp.int32, logits.shape, 1)
    neg = -jnp.inf
    big = jnp.int32(LANES)
    is_g = (lane >= RT_G) & (lane < RT_G + MOE_GROUPS)
    gl = jnp.where(is_g, logits, neg)
    gmax = jnp.max(gl, axis=-1, keepdims=True)
    gsel = jnp.min(jnp.where(gl == gmax, lane, big), axis=-1, keepdims=True) - RT_G
    group_w = 1.0 / jnp.sum(jnp.exp(gl - gmax), axis=-1, keepdims=True)
    e_lo = RT_E + gsel * MOE_EPG
    in_grp = (lane >= e_lo) & (lane < e_lo + MOE_EPG)
    el = jnp.where(in_grp, logits, neg)
    m1 = jnp.max(el, axis=-1, keepdims=True)
    i1 = jnp.min(jnp.where(el == m1, lane, big), axis=-1, keepdims=True)
    el2 = jnp.where(lane == i1, neg, el)
    m2 = jnp.max(el2, axis=-1, keepdims=True)
    i2 = jnp.min(jnp.where(el2 == m2, lane, big), axis=-1, keepdims=True)
    z = jnp.sum(jnp.exp(el - m1), axis=-1, keepdims=True)
    p1 = 1.0 / z
    p2 = jnp.exp(m2 - m1) / z
    tot = p1 + p2
    w1 = group_w * (p1 / tot)
    w2 = group_w * (p2 / tot)
    l1 = i1 - e_lo
    l2 = i2 - e_lo
    lo = jnp.minimum(l1, l2)
    hi = jnp.maximum(l1, l2)
    pair = ((lo * (2 * MOE_EPG - 1 - lo)) >> 1) + (hi - lo - 1)
    bucket = (gsel * MOE_PAIRS + pair).astype(F32)
    first_lo = l1 < l2
    w_lo = jnp.where(first_lo, w1, w2)
    w_hi = jnp.where(first_lo, w2, w1)
    return jnp.where(lane == RT_BUCKET, bucket,
                     jnp.where(lane == RT_WLO, w_lo, jnp.where(lane == RT_WHI, w_hi, 0.0)))


def _moe_rmsnorm(x, lnw):
    return x * lax.rsqrt(jnp.mean(x * x, axis=-1, keepdims=True) + EPS) * lnw


def _router_kernel(h_ref, lnw_ref, wr_ref, br_ref, info_ref, counts_ref, run_ref):
    t = h_ref.shape[0]

    @pl.when(pl.program_id(0) == 0)
    def _():
        run_ref[...] = jnp.zeros_like(run_ref)

    hi, lo = _split(_moe_rmsnorm(h_ref[...], lnw_ref[...]))
    whi, wlo = wr_ref[0], wr_ref[1]
    info = _route(_dot(hi, whi) + _dot(lo, whi) + _dot(hi, wlo) + br_ref[...])
    lane = lax.broadcasted_iota(jnp.int32, info.shape, 1)
    onehot = jnp.where(lane.astype(F32) == info[:, RT_BUCKET:RT_BUCKET + 1], 1.0, 0.0)
    row = lax.broadcasted_iota(jnp.int32, (t, t), 0)
    col = lax.broadcasted_iota(jnp.int32, (t, t), 1)
    earlier = _dot(jnp.where(row > col, 1.0, 0.0).astype(BF16), onehot.astype(BF16))
    rank = jnp.sum(onehot * (earlier + run_ref[...]), axis=-1, keepdims=True)
    run_ref[...] += jnp.sum(onehot, axis=0, keepdims=True)
    info_ref[...] = jnp.where(lane == RT_RANK, rank, info)
    counts_ref[...] = run_ref[...]


def _dispatch_kernel(pos_ref, h_ref, info_ref, xs_in_hbm, xs_hbm, buf_ref, sem):
    del xs_in_hbm
    tm, d = h_ref.shape
    i = pl.program_id(0)
    last = pl.num_programs(0) - 1
    base = i * tm
    slot = i % 2

    def wait_rows(s):
        pltpu.make_async_copy(buf_ref.at[s], xs_hbm.at[pl.ds(0, tm)], sem.at[s]).wait()

    @pl.when(i >= 2)
    def _():
        wait_rows(slot)

    buf_ref[slot, :, :d] = h_ref[...]
    buf_ref[slot, :, d:] = info_ref[...]

    def issue(r, carry):
        pltpu.make_async_copy(buf_ref.at[slot, pl.ds(r, 1)], xs_hbm.at[pl.ds(pos_ref[base + r], 1)],
                              sem.at[slot]).start()
        return carry

    lax.fori_loop(0, tm, issue, 0, unroll=8)

    @pl.when(i == last)
    def _():
        wait_rows(slot)

    @pl.when((i == last) & (i >= 1))
    def _():
        wait_rows(1 - slot)


def _experts_kernel(lo_ref, hi_ref, nt_ref, x_ref, lnw_ref, wg_lo, wu_lo, wd_lo, wg_hi, wu_hi, wd_hi, y_ref):
    live = pl.program_id(0) < nt_ref[0]
    d = y_ref.shape[1]

    @pl.when(live)
    def _():
        xn = _moe_rmsnorm(x_ref[:, :d], lnw_ref[...]).astype(BF16)
        wts = x_ref[:, d + RT_WLO:d + RT_WHI + 1]
        y = None
        for j, (wg, wu, wd) in enumerate(((wg_lo, wu_lo, wd_lo), (wg_hi, wu_hi, wd_hi))):
            hid = _silu(_dot(xn, wg[0])) * _dot(xn, wu[0]) * wts[:, j:j + 1]
            part = _dot(hid.astype(BF16), wd[0])
            y = part if y is None else y + part
        y_ref[...] = x_ref[:, :d] + y

    @pl.when(jnp.logical_not(live))
    def _():
        y_ref[...] = jnp.zeros_like(y_ref)


def _combine_kernel(pos_ref, y_hbm, fnw_ref, out_ref, buf_ref, sem, *, final_norm):
    tm = out_ref.shape[0]
    i = pl.program_id(0)
    slot = i % 2

    def issue_tile(tile, s):
        base = tile * tm

        def issue(r, carry):
            pltpu.make_async_copy(y_hbm.at[pl.ds(pos_ref[base + r], 1)], buf_ref.at[s, pl.ds(r, 1)],
                                  sem.at[s]).start()
            return carry

        lax.fori_loop(0, tm, issue, 0, unroll=8)

    @pl.when(i == 0)
    def _():
        issue_tile(0, 0)

    @pl.when(i + 1 < pl.num_programs(0))
    def _():
        issue_tile(i + 1, 1 - slot)

    pltpu.make_async_copy(y_hbm.at[pl.ds(0, tm)], buf_ref.at[slot], sem.at[slot]).wait()
    y = buf_ref[slot]
    if final_norm:
        y = _moe_rmsnorm(y, fnw_ref[...])
    out_ref[...] = y


def _moe(h, lnw, w_router, b_router, w_gate, w_up, w_down, fnw, tm, final_norm, out_order):
    n, d = h.shape
    ne, _, dff = w_gate.shape
    tile = MOE_TILE
    n_tiles = n // tile + MOE_BUCKETS
    p = n_tiles * tile
    const = lambda i: (0, 0)
    info, counts = pl.pallas_call(
        _router_kernel,
        grid=(n // tm,),
        in_specs=[
            pl.BlockSpec((tm, d), lambda i: (i, 0)),
            pl.BlockSpec((1, d), const),
            pl.BlockSpec((2, d, LANES), lambda i: (0, 0, 0)),
            pl.BlockSpec((1, LANES), const),
        ],
        out_specs=[pl.BlockSpec((tm, LANES), lambda i: (i, 0)), pl.BlockSpec((1, LANES), const)],
        out_shape=[jax.ShapeDtypeStruct((n, LANES), F32), jax.ShapeDtypeStruct((1, LANES), F32)],
        scratch_shapes=[pltpu.VMEM((1, LANES), F32)],
        compiler_params=pltpu.CompilerParams(dimension_semantics=("arbitrary",), vmem_limit_bytes=VMEM_LIMIT),
        name="moe_router",
    )(h, lnw, w_router, b_router)

    tiles_per = (counts[0, :MOE_BUCKETS].astype(jnp.int32) + tile - 1) // tile
    tile_end = jnp.cumsum(tiles_per)
    first_slot = ((tile_end - tiles_per) * tile).astype(F32)
    bucket_is = info[:, RT_BUCKET:RT_BUCKET + 1] == jnp.arange(MOE_BUCKETS, dtype=F32)[None, :]
    pos = (jnp.sum(jnp.where(bucket_is, first_slot[None, :], 0.0), axis=1) + info[:, RT_RANK]).astype(jnp.int32)
    n_live = tile_end[-1]
    tile_bucket = jnp.searchsorted(tile_end, jnp.minimum(jnp.arange(n_tiles, dtype=jnp.int32), n_live - 1),
                                   side="right").astype(jnp.int32)
    pairs = [(a, b) for a in range(MOE_EPG) for b in range(a + 1, MOE_EPG)]
    grp = tile_bucket // MOE_PAIRS
    tile_lo = grp * MOE_EPG + jnp.array([a for a, _ in pairs], jnp.int32)[tile_bucket % MOE_PAIRS]
    tile_hi = grp * MOE_EPG + jnp.array([b for _, b in pairs], jnp.int32)[tile_bucket % MOE_PAIRS]

    x_sorted = pl.pallas_call(
        _dispatch_kernel,
        grid_spec=pltpu.PrefetchScalarGridSpec(
            num_scalar_prefetch=1, grid=(n // tile,),
            in_specs=[
                pl.BlockSpec((tile, d), lambda i, pos: (i, 0)),
                pl.BlockSpec((tile, LANES), lambda i, pos: (i, 0)),
                pl.BlockSpec(memory_space=pl.ANY),
            ],
            out_specs=pl.BlockSpec(memory_space=pl.ANY),
            scratch_shapes=[pltpu.VMEM((2, tile, d + LANES), F32), pltpu.SemaphoreType.DMA((2,))]),
        out_shape=jax.ShapeDtypeStruct((p, d + LANES), F32),
        input_output_aliases={3: 0},
        compiler_params=pltpu.CompilerParams(dimension_semantics=("arbitrary",), vmem_limit_bytes=VMEM_LIMIT),
        name="moe_dispatch",
    )(pos, h, info, jnp.zeros((p, d + LANES), F32))

    lo_map = lambda t, lo, hi, nt: (lo[t], 0, 0)
    hi_map = lambda t, lo, hi, nt: (hi[t], 0, 0)
    y_sorted = pl.pallas_call(
        _experts_kernel,
        grid_spec=pltpu.PrefetchScalarGridSpec(
            num_scalar_prefetch=3, grid=(n_tiles,),
            in_specs=[
                pl.BlockSpec((tile, d + LANES), lambda t, lo, hi, nt: (t, 0)),
                pl.BlockSpec((1, d), lambda t, lo, hi, nt: (0, 0)),
                pl.BlockSpec((1, d, dff), lo_map),
                pl.BlockSpec((1, d, dff), lo_map),
                pl.BlockSpec((1, dff, d), lo_map),
                pl.BlockSpec((1, d, dff), hi_map),
                pl.BlockSpec((1, d, dff), hi_map),
                pl.BlockSpec((1, dff, d), hi_map),
            ],
            out_specs=pl.BlockSpec((tile, d), lambda t, lo, hi, nt: (t, 0))),
        out_shape=jax.ShapeDtypeStruct((p, d), F32),
        compiler_params=pltpu.CompilerParams(dimension_semantics=("arbitrary",), vmem_limit_bytes=VMEM_LIMIT),
        name="moe_experts",
    )(tile_lo, tile_hi, n_live.reshape(1).astype(jnp.int32), x_sorted, lnw,
      w_gate, w_up, w_down, w_gate, w_up, w_down)

    return pl.pallas_call(
        functools.partial(_combine_kernel, final_norm=final_norm),
        grid_spec=pltpu.PrefetchScalarGridSpec(
            num_scalar_prefetch=1, grid=(n // tile,),
            in_specs=[
                pl.BlockSpec(memory_space=pl.ANY),
                pl.BlockSpec((1, d), lambda i, pos: (0, 0)),
            ],
            out_specs=pl.BlockSpec((tile, d), lambda i, pos: (i, 0)),
            scratch_shapes=[pltpu.VMEM((2, tile, d), F32), pltpu.SemaphoreType.DMA((2,))]),
        out_shape=jax.ShapeDtypeStruct((n, d), F32),
        compiler_params=pltpu.CompilerParams(dimension_semantics=("arbitrary",), vmem_limit_bytes=VMEM_LIMIT),
        name="moe_combine",
    )(pos if out_order is None else out_order(pos), y_sorted, fnw)


def _lane_row(v, offset):
    return jnp.zeros((1, LANES), F32).at[0, offset:offset + v.shape[0]].set(v.astype(F32))


def _expansion(offset, heads, width):
    r = jnp.arange(LANES)[:, None]
    c = jnp.arange(heads * width)[None, :]
    return (r == offset + c // width).astype(BF16)


def _row_tile(n, want):
    t = min(n, want)
    while n % t:
        t //= 2
    return t


def kernel(x, ln1_w, w_in, ssd_conv_w, ssd_conv_b, ssd_dt_bias, ssd_a_log, ssd_d, ssd_norm_w, w_ssd_proj,
           gdn_conv_w, gdn_dt_bias, gdn_a_log, gdn_norm_w, w_gdn_proj, b_merge, w_out, ln2_w,
           moe_w_group, moe_b_group, moe_w_expert, moe_b_expert, moe_w_gate, moe_w_up, moe_w_down,
           final_norm_w):
    bsz, seqlen, d = x.shape
    n = bsz * seqlen
    depth = w_in.shape[0]

    def swap_order(a):
        blocks = a.reshape((n // TIME_BLOCK, SUBLANES, SUBLANES) + a.shape[1:])
        return blocks.swapaxes(1, 2).reshape(a.shape)

    h = swap_order(x.reshape(n, d))

    o1 = SSD_D_INNER
    o2 = o1 + SSD_XBC
    o3 = o2 + SSD_HEADS
    o4 = o3 + 3 * GDN_DIM
    o5 = o4 + GDN_HEADS
    o6 = o5 + GDN_HEADS
    o7 = o6 + GDN_DIM

    e_ssd = _expansion(SM_DT, SSD_HEADS, SSD_HEAD_DIM)
    e_gdn_a = _expansion(SM_A, GDN_HEADS, GDN_HEAD_DIM)
    e_gdn_b = _expansion(SM_BETA, GDN_HEADS, GDN_HEAD_DIM)
    tm_proj = _row_tile(n, 1024)
    tm_merge = _row_tile(n, 512)
    tm_moe = _row_tile(n, 512)

    for i in range(depth):
        wi = w_in[i]
        w_main = jnp.concatenate(
            [wi[:, o1:o2], wi[:, o3:o4], wi[:, :o1], wi[:, o7:], wi[:, o6:o7]], axis=1).astype(BF16)
        w_small = jnp.concatenate(
            [wi[:, o2:o3], wi[:, o4:o5], wi[:, o5:o6],
             jnp.zeros((d, LANES - SSD_HEADS - 2 * GDN_HEADS), F32)], axis=1).astype(BF16)
        proj, small = _in_proj(h, ln1_w[i][None, :], w_main, w_small, tm_proj, 1024)

        y = _ssd(proj, small, ssd_conv_w[i], ssd_conv_b[i][None, :],
                 _lane_row(ssd_dt_bias[i], SM_DT), _lane_row(ssd_a_log[i], SM_DT),
                 jnp.repeat(ssd_d[i], SSD_HEAD_DIM)[None, :], ssd_norm_w[i][None, :], e_ssd, bsz, seqlen)
        o = _gdn(proj, small, gdn_conv_w[i], _lane_row(gdn_dt_bias[i], SM_A), _lane_row(gdn_a_log[i], SM_A),
                 gdn_norm_w[i][None, :], e_gdn_a, e_gdn_b, bsz, seqlen)
        h = _merge(y, o, proj, h, b_merge[i][None, :], w_ssd_proj[i].astype(BF16), w_gdn_proj[i].astype(BF16),
                   w_out[i].astype(BF16), tm_merge)

        w_r = jnp.concatenate(
            [moe_w_group[i], moe_w_expert[i], jnp.zeros((d, LANES - MOE_GROUPS - MOE_EXPERTS), F32)], axis=1)
        w_r_hi = w_r.astype(BF16)
        w_r_lo = (w_r - w_r_hi.astype(F32)).astype(BF16)
        b_r = jnp.concatenate(
            [moe_b_group[i], moe_b_expert[i], jnp.zeros((LANES - MOE_GROUPS - MOE_EXPERTS,), F32)])[None, :]
        h = _moe(h, ln2_w[i][None, :], jnp.stack([w_r_hi, w_r_lo]), b_r,
                 moe_w_gate[i].astype(BF16), moe_w_up[i].astype(BF16), moe_w_down[i].astype(BF16),
                 final_norm_w[None, :], tm_moe, final_norm=(i == depth - 1),
                 out_order=swap_order if i == depth - 1 else None)
    return h.reshape(bsz, seqlen, d)
```

```python
import functools

import jax
import jax.numpy as jnp
from jax import lax
from jax.experimental import pallas as pl
from jax.experimental.pallas import tpu as pltpu

F32 = jnp.float32
BF16 = jnp.bfloat16

EPS = 1e-6
CONV_K = 4
LANES = 128
SUBLANES = 8
TIME_BLOCK = SUBLANES * SUBLANES
HALO = (CONV_K - 1) * SUBLANES

D_MODEL = 1024
SSD_D_INNER = 2 * D_MODEL
SSD_HEAD_DIM = 64
SSD_HEADS = SSD_D_INNER // SSD_HEAD_DIM
SSD_GROUPS = 4
SSD_STATE = 128
SSD_CHUNK = 128
SSD_XBC = SSD_D_INNER + 2 * SSD_GROUPS * SSD_STATE
GDN_HEADS = 8
GDN_HEAD_DIM = 128
GDN_DIM = GDN_HEADS * GDN_HEAD_DIM
GDN_CHUNK = 64
MOE_GROUPS = 4
MOE_EPG = 4
MOE_EXPERTS = MOE_GROUPS * MOE_EPG
MOE_D_FF = 512

MAIN_COLS = SSD_XBC + 3 * GDN_DIM + SSD_D_INNER + 2 * D_MODEL + GDN_DIM
XBC_BLK, QKV_BLK = 0, 1
ZS_BLK, GATE_BLK = 3, 4
ZG_BLK = 10
SM_DT, SM_A, SM_BETA = 0, SSD_HEADS, SSD_HEADS + GDN_HEADS
RT_G, RT_E = 0, SUBLANES
MOE_BUCKET_ROWS = 32

VMEM_LIMIT = 56 * 1024 * 1024


def _dot(a, b):
    return jnp.dot(a, b, preferred_element_type=F32)


def _dot_nt(a, b):
    return lax.dot_general(a, b, (((1,), (1,)), ((), ())), preferred_element_type=F32)


def _split(x):
    hi = x.astype(BF16)
    lo = (x - hi.astype(F32)).astype(BF16)
    return hi, lo


def _dot_split_lhs(x, w):
    hi, lo = _split(x)
    return _dot(hi, w) + _dot(lo, w)


def _dot_split_rhs(w, x):
    hi, lo = _split(x)
    return _dot(w, hi) + _dot(w, lo)


def _sigmoid(x):
    return 1.0 / (1.0 + jnp.exp(-x))


def _silu(x):
    return x * _sigmoid(x)


def _softplus(x):
    return jnp.maximum(x, 0.0) + jnp.log1p(jnp.exp(-jnp.abs(x)))


def _time_index(r):
    return (r & ~(TIME_BLOCK - 1)) | ((r >> 3) & (SUBLANES - 1)) | ((r & (SUBLANES - 1)) << 3)


def _causal_conv(cur, tail_ref, w, first):
    t, c = cur.shape

    @pl.when(first)
    def _():
        tail_ref[...] = jnp.zeros_like(tail_ref)

    sub = lax.broadcasted_iota(jnp.int32, (SUBLANES, c), 0)
    prev_tail = tail_ref[...]
    outs = []
    for i in range(t // TIME_BLOCK):
        blk = cur[i * TIME_BLOCK:(i + 1) * TIME_BLOCK, :]
        tail = blk[TIME_BLOCK - HALO:, :]
        wrapped = [
            jnp.where(sub == 0,
                      pltpu.roll(prev_tail[j * SUBLANES:(j + 1) * SUBLANES, :], 1, axis=0),
                      pltpu.roll(tail[j * SUBLANES:(j + 1) * SUBLANES, :], 1, axis=0))
            for j in range(CONV_K - 1)]
        ext = jnp.concatenate(wrapped + [blk], axis=0)
        acc = None
        for k in range(CONV_K):
            term = ext[k * SUBLANES:k * SUBLANES + TIME_BLOCK, :] * w[k:k + 1, :]
            acc = term if acc is None else acc + term
        outs.append(acc)
        prev_tail = tail
    tail_ref[...] = prev_tail
    return jnp.concatenate(outs, axis=0)


def _inproj_kernel(x_ref, lnw_ref, w_ref, ws_ref, o_ref, os_ref, xn_ref):
    @pl.when(pl.program_id(1) == 0)
    def _():
        x = x_ref[...]
        xn = x * lax.rsqrt(jnp.mean(x * x, axis=-1, keepdims=True) + EPS) * lnw_ref[...]
        xn_ref[...] = xn.astype(BF16)
        os_ref[...] = _dot(xn_ref[...], ws_ref[...])

    o_ref[...] = _dot(xn_ref[...], w_ref[...]).astype(o_ref.dtype)


def _in_proj(x, lnw, w_main, w_small, layer, tm, tn):
    n, d = x.shape
    nm = w_main.shape[2]
    return pl.pallas_call(
        _inproj_kernel,
        grid=(n // tm, nm // tn),
        in_specs=[
            pl.BlockSpec((tm, d), lambda i, j: (i, 0)),
            pl.BlockSpec((1, d), lambda i, j: (0, 0)),
            pl.BlockSpec((pl.Squeezed(), d, tn), lambda i, j: (layer, 0, j)),
            pl.BlockSpec((pl.Squeezed(), d, LANES), lambda i, j: (layer, 0, 0)),
        ],
        out_specs=[
            pl.BlockSpec((tm, tn), lambda i, j: (i, j)),
            pl.BlockSpec((tm, LANES), lambda i, j: (i, 0)),
        ],
        out_shape=[jax.ShapeDtypeStruct((n, nm), BF16), jax.ShapeDtypeStruct((n, LANES), F32)],
        scratch_shapes=[pltpu.VMEM((tm, d), BF16)],
        compiler_params=pltpu.CompilerParams(
            dimension_semantics=("arbitrary", "arbitrary"), vmem_limit_bytes=VMEM_LIMIT),
        name="in_proj",
    )(x, lnw, w_main, w_small)


def _ssd_kernel(xbc_ref, z_ref, sm_ref, convw_ref, convb_ref, dtb_ref, alog_ref, dexp_ref, normw_ref, e_ref,
                y_ref, halo_ref, state_ref, yacc_ref):
    q = SSD_CHUNK
    p = SSD_HEAD_DIM
    ns = SSD_STATE
    gw = (SSD_HEADS // SSD_GROUPS) * p
    first = pl.program_id(1) == 0

    @pl.when(first)
    def _():
        state_ref[...] = jnp.zeros_like(state_ref)

    conv = _causal_conv(xbc_ref[...].astype(F32), halo_ref, convw_ref[...], first) + convb_ref[...]
    xbc = _silu(conv)
    xs = xbc[:, :SSD_D_INNER]
    bm = xbc[:, SSD_D_INNER:SSD_D_INNER + SSD_GROUPS * ns]
    cm = xbc[:, SSD_D_INNER + SSD_GROUPS * ns:]

    dt = _softplus(sm_ref[...] + dtb_ref[...])
    d_a = dt * (-jnp.exp(alog_ref[...]))
    row = lax.broadcasted_iota(jnp.int32, (q, q), 0)
    col = lax.broadcasted_iota(jnp.int32, (q, q), 1)
    causal = _time_index(row) >= _time_index(col)
    tri = jnp.where(causal, 1.0, 0.0).astype(BF16)
    a_cs = _dot_split_rhs(tri, d_a)
    a_cs_t = a_cs.T
    e_acs = jnp.exp(a_cs)
    dec_end = jnp.exp(a_cs[q - 1:q, :] - a_cs)
    ex = _dot_split_lhs(jnp.concatenate([dt, dt * dec_end, e_acs], axis=0), e_ref[...])
    dt_x, dtdec_x, eacs_x = ex[:q], ex[q:2 * q], ex[2 * q:]

    xdt_b = (xs * dt_x).astype(BF16)
    xdd_b = (xs * dtdec_x).astype(BF16)
    lane = lax.broadcasted_iota(jnp.int32, (q, LANES), 1)
    for g in range(SSD_GROUPS):
        bg = bm[:, g * ns:(g + 1) * ns]
        cg_b = cm[:, g * ns:(g + 1) * ns].astype(BF16)
        scores = _dot_nt(cg_b, bg.astype(BF16))
        s_g = state_ref[:, g * gw:(g + 1) * gw]
        yacc_ref[:, g * gw:(g + 1) * gw] = _dot(cg_b, s_g.astype(BF16)) * eacs_x[:, g * gw:(g + 1) * gw]
        new_g = _dot(bg.T.astype(BF16), xdd_b[:, g * gw:(g + 1) * gw])
        state_ref[:, g * gw:(g + 1) * gw] = s_g * eacs_x[q - 1:q, g * gw:(g + 1) * gw] + new_g
        for j in range(gw // LANES):
            h0 = (g * gw + j * LANES) // p
            c0 = g * gw + j * LANES
            ms = []
            for h in (h0, h0 + 1):
                seg = a_cs[:, h:h + 1] - a_cs_t[h:h + 1, :]
                ms.append((scores * jnp.exp(jnp.where(causal, seg, -jnp.inf))).astype(BF16))
            xp = xdt_b[:, c0:c0 + LANES]
            zero = jnp.zeros_like(xp)
            rhs = jnp.concatenate([jnp.where(lane < p, xp, zero), jnp.where(lane >= p, xp, zero)], axis=0)
            yacc_ref[:, c0:c0 + LANES] += _dot(jnp.concatenate(ms, axis=1), rhs)

    y = yacc_ref[...] + dexp_ref[...] * xs
    y = y * _silu(z_ref[...].astype(F32))
    y = y * lax.rsqrt(jnp.mean(y * y, axis=-1, keepdims=True) + EPS) * normw_ref[...]
    y_ref[...] = y.astype(BF16)


def _ssd(proj, small, convw, convb, dtb, alog, dexp, normw, expand, bsz, seqlen):
    q = SSD_CHUNK
    nc = seqlen // q
    n = bsz * seqlen
    rowmap = lambda b, c: b * nc + c
    const = lambda b, c: (0, 0)
    return pl.pallas_call(
        _ssd_kernel,
        grid=(bsz, nc),
        in_specs=[
            pl.BlockSpec((q, SSD_XBC), lambda b, c: (rowmap(b, c), XBC_BLK)),
            pl.BlockSpec((q, SSD_D_INNER), lambda b, c: (rowmap(b, c), ZS_BLK)),
            pl.BlockSpec((q, LANES), lambda b, c: (rowmap(b, c), 0)),
            pl.BlockSpec((CONV_K, SSD_XBC), const),
            pl.BlockSpec((1, SSD_XBC), const),
            pl.BlockSpec((1, LANES), const),
            pl.BlockSpec((1, LANES), const),
            pl.BlockSpec((1, SSD_D_INNER), const),
            pl.BlockSpec((1, SSD_D_INNER), const),
            pl.BlockSpec((LANES, SSD_D_INNER), const),
        ],
        out_specs=pl.BlockSpec((q, SSD_D_INNER), lambda b, c: (rowmap(b, c), 0)),
        out_shape=jax.ShapeDtypeStruct((n, SSD_D_INNER), BF16),
        scratch_shapes=[
            pltpu.VMEM((HALO, SSD_XBC), F32),
            pltpu.VMEM((SSD_STATE, SSD_D_INNER), F32),
            pltpu.VMEM((q, SSD_D_INNER), F32),
        ],
        compiler_params=pltpu.CompilerParams(
            dimension_semantics=("arbitrary", "arbitrary"), vmem_limit_bytes=VMEM_LIMIT),
        name="ssd",
    )(proj, proj, small, convw, convb, dtb, alog, dexp, normw, expand)


GDN_PREP_ROWS = 2 * GDN_CHUNK
GDN_CHUNK_SHIFT = GDN_CHUNK.bit_length() - 1


def _unit_lower_inverse_many(ms, row, col):
    c = ms[0].shape[0]
    diff = row ^ col
    eye = jnp.where(row == col, 1.0, 0.0)
    ts = [eye - jnp.where((diff >> 1) == 0, m, 0.0) for m in ms]
    shift = 1
    while (1 << shift) < c:
        sel = (diff >> shift) == 1
        tbs = [t.astype(BF16) for t in ts]
        xs = [_dot(tb, jnp.where(sel, m, 0.0).astype(BF16)) for tb, m in zip(tbs, ms)]
        ts = [t - _dot(x.astype(BF16), tb) for t, x, tb in zip(ts, xs, tbs)]
        shift += 1
    return ts


def _gdn_prep_kernel(qkv_ref, sm_ref, convw_ref, dtb_ref, alog_ref, eg_ref, eb_ref,
                     u_ref, w_ref, qd_ref, aqk_ref, kdt_ref, gl_ref, halo_ref):
    c = GDN_CHUNK
    dh = GDN_HEAD_DIM
    t = GDN_PREP_ROWS
    ncb = t // c
    first = pl.program_id(1) == 0
    qkv = _silu(_causal_conv(qkv_ref[...].astype(F32), halo_ref, convw_ref[...], first))

    sm = sm_ref[...]
    lane1 = lax.broadcasted_iota(jnp.int32, (t, LANES), 1)
    is_a = (lane1 >= SM_A) & (lane1 < SM_A + GDN_HEADS)
    log_decay = jnp.where(is_a, -jnp.exp(alog_ref[...]) * _softplus(sm + dtb_ref[...]), 0.0)
    beta = _sigmoid(sm)
    row_t = lax.broadcasted_iota(jnp.int32, (t, t), 0)
    col_t = lax.broadcasted_iota(jnp.int32, (t, t), 1)
    same_chunk = (row_t >> GDN_CHUNK_SHIFT) == (col_t >> GDN_CHUNK_SHIFT)
    tri = jnp.where(same_chunk, jnp.where(_time_index(row_t) >= _time_index(col_t), 1.0, 0.0), 0.0).astype(BF16)
    gc = _dot_split_rhs(tri, log_decay)
    gc_t = gc.T
    g_last = jnp.concatenate(
        [jnp.broadcast_to(gc[(i + 1) * c - 1:(i + 1) * c, :], (c, LANES)) for i in range(ncb)], axis=0)
    ex = _dot_split_lhs(jnp.concatenate([jnp.exp(gc), jnp.exp(g_last - gc)], axis=0), eg_ref[...])
    egc_x, erev_x = ex[:t], ex[t:]
    beta_x = _dot_split_lhs(beta, eb_ref[...])

    row = _time_index(lax.broadcasted_iota(jnp.int32, (c, c), 0))
    col = _time_index(lax.broadcasted_iota(jnp.int32, (c, c), 1))
    incl = row >= col
    strict = row > col

    qs, ks, kbs, rhs = [], [], [], []
    for h in range(GDN_HEADS):
        hs = slice(h * dh, (h + 1) * dh)
        qh = qkv[:, h * dh:(h + 1) * dh]
        kh = qkv[:, GDN_DIM + h * dh:GDN_DIM + (h + 1) * dh]
        vh = qkv[:, 2 * GDN_DIM + h * dh:2 * GDN_DIM + (h + 1) * dh]
        qh = qh * lax.rsqrt(jnp.sum(qh * qh, axis=-1, keepdims=True) + EPS) * (dh ** -0.5)
        kh = kh * lax.rsqrt(jnp.sum(kh * kh, axis=-1, keepdims=True) + EPS)
        kb = kh * beta_x[:, hs]
        qs.append(qh)
        ks.append(kh)
        kbs.append(kb)
        rhs.append(jnp.concatenate([vh * beta_x[:, hs], kb * egc_x[:, hs]], axis=1).astype(BF16))

    items = [(i, h) for i in range(ncb) for h in range(GDN_HEADS)]
    ms, aqks = [], []
    for i, h in items:
        rs = slice(i * c, (i + 1) * c)
        la = SM_A + h
        decay = jnp.exp(jnp.where(incl, gc[rs, la:la + 1] - gc_t[la:la + 1, rs], -jnp.inf))
        nt = _dot_nt(jnp.concatenate([kbs[h][rs], qs[h][rs]], axis=0).astype(BF16), ks[h][rs].astype(BF16))
        ms.append(jnp.where(strict, nt[:c] * decay, 0.0))
        aqks.append((nt[c:] * decay).astype(BF16))
    t_invs = _unit_lower_inverse_many(ms, row, col)
    uws = [_dot(t_inv.astype(BF16), rhs[h][i * c:(i + 1) * c]) for (i, h), t_inv in zip(items, t_invs)]

    def by_chunk(pieces):
        return jnp.concatenate(
            [jnp.concatenate(pieces[i * GDN_HEADS:(i + 1) * GDN_HEADS], axis=1) for i in range(ncb)], axis=0)

    u_ref[...] = by_chunk([uw[:, :dh] for uw in uws]).astype(BF16)
    w_ref[...] = by_chunk([uw[:, dh:] for uw in uws]).astype(BF16)
    aqk_ref[...] = by_chunk(aqks)
    qd_ref[...] = jnp.concatenate(
        [qs[h] * egc_x[:, h * dh:(h + 1) * dh] for h in range(GDN_HEADS)], axis=1).astype(BF16)
    for i in range(ncb):
        rs = slice(i * c, (i + 1) * c)
        kdt_ref[i] = jnp.concatenate(
            [(ks[h][rs] * erev_x[rs, h * dh:(h + 1) * dh]).T for h in range(GDN_HEADS)], axis=1).astype(BF16)
        gl_ref[i] = egc_x[(i + 1) * c - 1:(i + 1) * c, :]


def _gdn_scan_kernel(u_ref, w_ref, qd_ref, aqk_ref, kdt_ref, gl_ref, zg_ref, normw_ref, o_ref, state_ref):
    c = GDN_CHUNK
    dh = GDN_HEAD_DIM

    @pl.when(pl.program_id(0) == 0)
    def _():
        state_ref[...] = jnp.zeros_like(state_ref)

    zero = jnp.zeros((c, dh), BF16)
    for b in range(u_ref.shape[0]):
        outs = []
        for pr in range(GDN_HEADS // 2):
            heads = (2 * pr, 2 * pr + 1)
            ss, q_s, v_news = [], [], []
            for h in heads:
                hs = slice(h * dh, (h + 1) * dh)
                s = state_ref[b, h]
                p = _dot(jnp.concatenate([w_ref[b, :, hs], qd_ref[b, :, hs]], axis=0), s.astype(BF16))
                ss.append(s)
                q_s.append(p[c:])
                v_news.append((u_ref[b, :, hs].astype(F32) - p[:c]).astype(BF16))
            lhs = jnp.concatenate([aqk_ref[b, :, pr * dh:(pr + 1) * dh], kdt_ref[b, 0, :, pr * dh:(pr + 1) * dh]],
                                  axis=0)
            bd = jnp.concatenate([jnp.concatenate([v_news[0], zero], axis=1),
                                  jnp.concatenate([zero, v_news[1]], axis=1)], axis=0)
            r = _dot(lhs, bd)
            for j, h in enumerate(heads):
                hs = slice(h * dh, (h + 1) * dh)
                state_ref[b, h] = ss[j] * gl_ref[b, 0, :, hs] + r[c:, j * dh:(j + 1) * dh]
                o = q_s[j] + r[:c, j * dh:(j + 1) * dh]
                o = o * lax.rsqrt(jnp.mean(o * o, axis=-1, keepdims=True) + EPS) * normw_ref[...]
                outs.append(o * _silu(zg_ref[b, :, hs].astype(F32)))
        o_ref[b] = jnp.concatenate(outs, axis=1).astype(BF16)


def _gdn(proj, small, convw, dtb, alog, normw, expand_a, expand_b, bsz, seqlen):
    c = GDN_CHUNK
    t = GDN_PREP_ROWS
    nc = seqlen // c
    nb = seqlen // t
    n = bsz * seqlen
    hh = GDN_HEADS * c
    rowmap = lambda b, i: b * nb + i
    const = lambda b, i: (0, 0)
    params = pltpu.CompilerParams(dimension_semantics=("arbitrary", "arbitrary"), vmem_limit_bytes=VMEM_LIMIT)
    u, w, qd, aqk, kdt, gl = pl.pallas_call(
        _gdn_prep_kernel,
        grid=(bsz, nb),
        in_specs=[
            pl.BlockSpec((t, 3 * GDN_DIM), lambda b, i: (rowmap(b, i), QKV_BLK)),
            pl.BlockSpec((t, LANES), lambda b, i: (rowmap(b, i), 0)),
            pl.BlockSpec((CONV_K, 3 * GDN_DIM), const),
            pl.BlockSpec((1, LANES), const),
            pl.BlockSpec((1, LANES), const),
            pl.BlockSpec((LANES, GDN_DIM), const),
            pl.BlockSpec((LANES, GDN_DIM), const),
        ],
        out_specs=[
            pl.BlockSpec((t, GDN_DIM), lambda b, i: (rowmap(b, i), 0)),
            pl.BlockSpec((t, GDN_DIM), lambda b, i: (rowmap(b, i), 0)),
            pl.BlockSpec((t, GDN_DIM), lambda b, i: (rowmap(b, i), 0)),
            pl.BlockSpec((t, hh), lambda b, i: (rowmap(b, i), 0)),
            pl.BlockSpec((t // c, GDN_HEAD_DIM, hh), lambda b, i: (rowmap(b, i), 0, 0)),
            pl.BlockSpec((t // c, 1, GDN_DIM), lambda b, i: (rowmap(b, i), 0, 0)),
        ],
        out_shape=[
            jax.ShapeDtypeStruct((n, GDN_DIM), BF16),
            jax.ShapeDtypeStruct((n, GDN_DIM), BF16),
            jax.ShapeDtypeStruct((n, GDN_DIM), BF16),
            jax.ShapeDtypeStruct((n, hh), BF16),
            jax.ShapeDtypeStruct((n // c, GDN_HEAD_DIM, hh), BF16),
            jax.ShapeDtypeStruct((n // c, 1, GDN_DIM), F32),
        ],
        scratch_shapes=[pltpu.VMEM((HALO, 3 * GDN_DIM), F32)],
        compiler_params=params,
        name="gdn_prep",
    )(proj, small, convw, dtb, alog, expand_a, expand_b)

    seq3 = lambda i: (0, i, 0)
    seq4 = lambda i: (0, i, 0, 0)
    o = pl.pallas_call(
        _gdn_scan_kernel,
        grid=(nc,),
        in_specs=[
            pl.BlockSpec((bsz, c, GDN_DIM), seq3),
            pl.BlockSpec((bsz, c, GDN_DIM), seq3),
            pl.BlockSpec((bsz, c, GDN_DIM), seq3),
            pl.BlockSpec((bsz, c, hh), seq3),
            pl.BlockSpec((bsz, 1, GDN_HEAD_DIM, hh), seq4),
            pl.BlockSpec((bsz, 1, 1, GDN_DIM), seq4),
            pl.BlockSpec((bsz, c, GDN_DIM), lambda i: (0, i, ZG_BLK)),
            pl.BlockSpec((1, GDN_HEAD_DIM), lambda i: (0, 0)),
        ],
        out_specs=pl.BlockSpec((bsz, c, GDN_DIM), seq3),
        out_shape=jax.ShapeDtypeStruct((bsz, seqlen, GDN_DIM), BF16),
        scratch_shapes=[pltpu.VMEM((bsz, GDN_HEADS, GDN_HEAD_DIM, GDN_HEAD_DIM), F32)],
        compiler_params=pltpu.CompilerParams(dimension_semantics=("arbitrary",), vmem_limit_bytes=VMEM_LIMIT),
        name="gdn_scan",
    )(u.reshape(bsz, seqlen, GDN_DIM), w.reshape(bsz, seqlen, GDN_DIM), qd.reshape(bsz, seqlen, GDN_DIM),
      aqk.reshape(bsz, seqlen, hh), kdt.reshape(bsz, nc, GDN_HEAD_DIM, hh), gl.reshape(bsz, nc, 1, GDN_DIM),
      proj.reshape(bsz, seqlen, MAIN_COLS), normw)
    return o.reshape(n, GDN_DIM)


def _merge_kernel(y_ref, o_ref, gate_ref, h_ref, bm_ref, wsp_ref, wgp_ref, wout_ref, out_ref):
    d = h_ref.shape[1]
    y_ssd = _dot(y_ref[...], wsp_ref[...])
    y_gdn = _dot(o_ref[...], wgp_ref[...])
    gates = _sigmoid(gate_ref[...].astype(F32) + bm_ref[...])
    merged = gates[:, :d] * y_ssd + gates[:, d:] * y_gdn
    out_ref[...] = h_ref[...] + _dot(merged.astype(BF16), wout_ref[...])


def _merge(y, o, proj, h, b_merge, w_sp, w_gp, w_out, layer, tm):
    n, d = h.shape
    const = lambda i: (0, 0)
    return pl.pallas_call(
        _merge_kernel,
        grid=(n // tm,),
        in_specs=[
            pl.BlockSpec((tm, SSD_D_INNER), lambda i: (i, 0)),
            pl.BlockSpec((tm, GDN_DIM), lambda i: (i, 0)),
            pl.BlockSpec((tm, 2 * d), lambda i: (i, GATE_BLK)),
            pl.BlockSpec((tm, d), lambda i: (i, 0)),
            pl.BlockSpec((1, 2 * d), const),
            pl.BlockSpec((pl.Squeezed(), SSD_D_INNER, d), lambda i: (layer, 0, 0)),
            pl.BlockSpec((pl.Squeezed(), GDN_DIM, d), lambda i: (layer, 0, 0)),
            pl.BlockSpec((pl.Squeezed(), d, d), lambda i: (layer, 0, 0)),
        ],
        out_specs=pl.BlockSpec((tm, d), lambda i: (i, 0)),
        out_shape=jax.ShapeDtypeStruct((n, d), F32),
        compiler_params=pltpu.CompilerParams(
            dimension_semantics=("arbitrary",), vmem_limit_bytes=VMEM_LIMIT),
        name="merge",
    )(y, o, proj, h, b_merge, w_sp, w_gp, w_out)


MOE_PAIRS = MOE_EPG * (MOE_EPG - 1) // 2
MOE_BUCKETS = MOE_GROUPS * MOE_PAIRS
MOE_TILE = 256
RT_BUCKET, RT_WLO, RT_WHI, RT_RANK = 0, 1, 2, 3


def _route(logits):
    t = logits.shape[1]
    neg = -jnp.inf
    big = jnp.int32(LANES)
    gl = logits[RT_G:RT_G + MOE_GROUPS, :]
    grow = lax.broadcasted_iota(jnp.int32, (MOE_GROUPS, t), 0)
    gmax = jnp.max(gl, axis=0, keepdims=True)
    gsel = jnp.min(jnp.where(gl == gmax, grow, big), axis=0, keepdims=True)
    group_w = 1.0 / jnp.sum(jnp.exp(gl - gmax), axis=0, keepdims=True)
    erow = lax.broadcasted_iota(jnp.int32, (MOE_EXPERTS, t), 0)
    el = jnp.where((erow >> (MOE_EPG.bit_length() - 1)) == gsel, logits[RT_E:RT_E + MOE_EXPERTS, :], neg)
    m1 = jnp.max(el, axis=0, keepdims=True)
    i1 = jnp.min(jnp.where(el == m1, erow, big), axis=0, keepdims=True)
    el2 = jnp.where(erow == i1, neg, el)
    m2 = jnp.max(el2, axis=0, keepdims=True)
    i2 = jnp.min(jnp.where(el2 == m2, erow, big), axis=0, keepdims=True)
    z = jnp.sum(jnp.exp(el - m1), axis=0, keepdims=True)
    p1 = 1.0 / z
    p2 = jnp.exp(m2 - m1) / z
    tot = p1 + p2
    w1 = group_w * (p1 / tot)
    w2 = group_w * (p2 / tot)
    l1 = i1 - gsel * MOE_EPG
    l2 = i2 - gsel * MOE_EPG
    lo = jnp.minimum(l1, l2)
    hi = jnp.maximum(l1, l2)
    pair = ((lo * (2 * MOE_EPG - 1 - lo)) >> 1) + (hi - lo - 1)
    bucket = gsel * MOE_PAIRS + pair
    first_lo = l1 < l2
    return bucket, jnp.where(first_lo, w1, w2), jnp.where(first_lo, w2, w1)


def _moe_rmsnorm(x, lnw):
    return x * lax.rsqrt(jnp.mean(x * x, axis=-1, keepdims=True) + EPS) * lnw


def _router_kernel(h_ref, lnw_ref, wr_ref, br_ref, info_ref, counts_ref, run_ref):
    t = h_ref.shape[0]

    @pl.when(pl.program_id(0) == 0)
    def _():
        run_ref[...] = jnp.zeros_like(run_ref)

    hi, lo = _split(_moe_rmsnorm(h_ref[...], lnw_ref[...]))
    whi, wlo = wr_ref[0], wr_ref[1]
    logits = _dot_nt(whi, hi) + _dot_nt(whi, lo) + _dot_nt(wlo, hi) + br_ref[:, 0:1]
    bucket, w_lo, w_hi = _route(logits)
    brow = lax.broadcasted_iota(jnp.int32, (MOE_BUCKET_ROWS, t), 0)
    onehot = jnp.where(brow == bucket, 1.0, 0.0)
    row = lax.broadcasted_iota(jnp.int32, (t, t), 0)
    col = lax.broadcasted_iota(jnp.int32, (t, t), 1)
    earlier = _dot(onehot.astype(BF16), jnp.where(row < col, 1.0, 0.0).astype(BF16))
    rank = jnp.sum(onehot * (earlier + run_ref[:, 0:1]), axis=0, keepdims=True)
    run_ref[...] += jnp.sum(onehot, axis=1, keepdims=True)
    record = jnp.concatenate(
        [bucket.astype(F32), w_lo, w_hi, rank, jnp.zeros((LANES - 4, t), F32)], axis=0)
    info_ref[...] = record.T
    counts_ref[...] = run_ref[...]


def _dispatch_kernel(pos_ref, h_ref, info_ref, xs_in_hbm, xs_hbm, buf_ref, sem):
    del xs_in_hbm
    tm, d = h_ref.shape
    i = pl.program_id(0)
    last = pl.num_programs(0) - 1
    base = i * tm
    slot = i % 2

    def wait_rows(s):
        pltpu.make_async_copy(buf_ref.at[s], xs_hbm.at[pl.ds(0, tm)], sem.at[s]).wait()

    @pl.when(i >= 2)
    def _():
        wait_rows(slot)

    buf_ref[slot, :, :d] = h_ref[...]
    buf_ref[slot, :, d:] = info_ref[...]

    def issue(r, carry):
        pltpu.make_async_copy(buf_ref.at[slot, pl.ds(r, 1)], xs_hbm.at[pl.ds(pos_ref[base + r], 1)],
                              sem.at[slot]).start()
        return carry

    lax.fori_loop(0, tm, issue, 0, unroll=8)

    @pl.when(i == last)
    def _():
        wait_rows(slot)

    @pl.when((i == last) & (i >= 1))
    def _():
        wait_rows(1 - slot)


def _experts_kernel(lo_ref, hi_ref, nt_ref, x_ref, lnw_ref, wg_lo, wu_lo, wd_lo, wg_hi, wu_hi, wd_hi, y_ref):
    live = pl.program_id(0) < nt_ref[0]
    d = y_ref.shape[1]

    @pl.when(live)
    def _():
        xn = _moe_rmsnorm(x_ref[:, :d], lnw_ref[...]).astype(BF16)
        wts = x_ref[:, d + RT_WLO:d + RT_WHI + 1]
        y = None
        for j, (wg, wu, wd) in enumerate(((wg_lo, wu_lo, wd_lo), (wg_hi, wu_hi, wd_hi))):
            hid = _silu(_dot(xn, wg[0])) * _dot(xn, wu[0]) * wts[:, j:j + 1]
            part = _dot(hid.astype(BF16), wd[0])
            y = part if y is None else y + part
        y_ref[...] = x_ref[:, :d] + y

    @pl.when(jnp.logical_not(live))
    def _():
        y_ref[...] = jnp.zeros_like(y_ref)


def _combine_kernel(pos_ref, y_hbm, fnw_ref, out_ref, buf_ref, sem, *, final_norm):
    tm = out_ref.shape[0]
    i = pl.program_id(0)
    slot = i % 2

    def issue_tile(tile, s):
        base = tile * tm

        def issue(r, carry):
            pltpu.make_async_copy(y_hbm.at[pl.ds(pos_ref[base + r], 1)], buf_ref.at[s, pl.ds(r, 1)],
                                  sem.at[s]).start()
            return carry

        lax.fori_loop(0, tm, issue, 0, unroll=8)

    @pl.when(i == 0)
    def _():
        issue_tile(0, 0)

    @pl.when(i + 1 < pl.num_programs(0))
    def _():
        issue_tile(i + 1, 1 - slot)

    pltpu.make_async_copy(y_hbm.at[pl.ds(0, tm)], buf_ref.at[slot], sem.at[slot]).wait()
    y = buf_ref[slot]
    if final_norm:
        y = _moe_rmsnorm(y, fnw_ref[...])
    out_ref[...] = y


def _moe(h, lnw, w_router, b_router, w_gate, w_up, w_down, fnw, layer, tm, final_norm, out_order, slots):
    n, d = h.shape
    dff = w_gate.shape[-1]
    tile = MOE_TILE
    n_tiles = n // tile + MOE_BUCKETS
    p = n_tiles * tile
    const = lambda i: (0, 0)
    info, counts = pl.pallas_call(
        _router_kernel,
        grid=(n // tm,),
        in_specs=[
            pl.BlockSpec((tm, d), lambda i: (i, 0)),
            pl.BlockSpec((1, d), const),
            pl.BlockSpec((2, LANES, d), lambda i: (0, 0, 0)),
            pl.BlockSpec((LANES, LANES), const),
        ],
        out_specs=[pl.BlockSpec((tm, LANES), lambda i: (i, 0)), pl.BlockSpec((MOE_BUCKET_ROWS, LANES), const)],
        out_shape=[jax.ShapeDtypeStruct((n, LANES), F32), jax.ShapeDtypeStruct((MOE_BUCKET_ROWS, LANES), F32)],
        scratch_shapes=[pltpu.VMEM((MOE_BUCKET_ROWS, LANES), F32)],
        compiler_params=pltpu.CompilerParams(dimension_semantics=("arbitrary",), vmem_limit_bytes=VMEM_LIMIT),
        name="moe_router",
    )(h, lnw, w_router, b_router)

    tiles_per = (counts[:MOE_BUCKETS, 0].astype(jnp.int32) + tile - 1) // tile
    tile_end = jnp.cumsum(tiles_per)
    first_slot = ((tile_end - tiles_per) * tile).astype(F32)
    bucket_is = info[:, RT_BUCKET:RT_BUCKET + 1] == jnp.arange(MOE_BUCKETS, dtype=F32)[None, :]
    pos = (jnp.sum(jnp.where(bucket_is, first_slot[None, :], 0.0), axis=1) + info[:, RT_RANK]).astype(jnp.int32)
    n_live = tile_end[-1]
    tile_bucket = jnp.searchsorted(tile_end, jnp.minimum(jnp.arange(n_tiles, dtype=jnp.int32), n_live - 1),
                                   side="right").astype(jnp.int32)
    pairs = [(a, b) for a in range(MOE_EPG) for b in range(a + 1, MOE_EPG)]
    grp = tile_bucket // MOE_PAIRS
    tile_lo = grp * MOE_EPG + jnp.array([a for a, _ in pairs], jnp.int32)[tile_bucket % MOE_PAIRS]
    tile_hi = grp * MOE_EPG + jnp.array([b for _, b in pairs], jnp.int32)[tile_bucket % MOE_PAIRS]

    x_sorted = pl.pallas_call(
        _dispatch_kernel,
        grid_spec=pltpu.PrefetchScalarGridSpec(
            num_scalar_prefetch=1, grid=(n // tile,),
            in_specs=[
                pl.BlockSpec((tile, d), lambda i, pos: (i, 0)),
                pl.BlockSpec((tile, LANES), lambda i, pos: (i, 0)),
                pl.BlockSpec(memory_space=pl.ANY),
            ],
            out_specs=pl.BlockSpec(memory_space=pl.ANY),
            scratch_shapes=[pltpu.VMEM((2, tile, d + LANES), F32), pltpu.SemaphoreType.DMA((2,))]),
        out_shape=jax.ShapeDtypeStruct((p, d + LANES), F32),
        input_output_aliases={3: 0},
        compiler_params=pltpu.CompilerParams(dimension_semantics=("arbitrary",), vmem_limit_bytes=VMEM_LIMIT),
        name="moe_dispatch",
    )(pos, h, info, jnp.zeros((p, d + LANES), F32) if slots is None else slots)

    lo_map = lambda t, lo, hi, nt: (layer, lo[t], 0, 0)
    hi_map = lambda t, lo, hi, nt: (layer, hi[t], 0, 0)
    y_sorted = pl.pallas_call(
        _experts_kernel,
        grid_spec=pltpu.PrefetchScalarGridSpec(
            num_scalar_prefetch=3, grid=(n_tiles,),
            in_specs=[
                pl.BlockSpec((tile, d + LANES), lambda t, lo, hi, nt: (t, 0)),
                pl.BlockSpec((1, d), lambda t, lo, hi, nt: (0, 0)),
                pl.BlockSpec((pl.Squeezed(), 1, d, dff), lo_map),
                pl.BlockSpec((pl.Squeezed(), 1, d, dff), lo_map),
                pl.BlockSpec((pl.Squeezed(), 1, dff, d), lo_map),
                pl.BlockSpec((pl.Squeezed(), 1, d, dff), hi_map),
                pl.BlockSpec((pl.Squeezed(), 1, d, dff), hi_map),
                pl.BlockSpec((pl.Squeezed(), 1, dff, d), hi_map),
            ],
            out_specs=pl.BlockSpec((tile, d), lambda t, lo, hi, nt: (t, 0))),
        out_shape=jax.ShapeDtypeStruct((p, d), F32),
        compiler_params=pltpu.CompilerParams(dimension_semantics=("arbitrary",), vmem_limit_bytes=VMEM_LIMIT),
        name="moe_experts",
    )(tile_lo, tile_hi, n_live.reshape(1).astype(jnp.int32), x_sorted, lnw,
      w_gate, w_up, w_down, w_gate, w_up, w_down)

    out = pl.pallas_call(
        functools.partial(_combine_kernel, final_norm=final_norm),
        grid_spec=pltpu.PrefetchScalarGridSpec(
            num_scalar_prefetch=1, grid=(n // tile,),
            in_specs=[
                pl.BlockSpec(memory_space=pl.ANY),
                pl.BlockSpec((1, d), lambda i, pos: (0, 0)),
            ],
            out_specs=pl.BlockSpec((tile, d), lambda i, pos: (i, 0)),
            scratch_shapes=[pltpu.VMEM((2, tile, d), F32), pltpu.SemaphoreType.DMA((2,))]),
        out_shape=jax.ShapeDtypeStruct((n, d), F32),
        compiler_params=pltpu.CompilerParams(dimension_semantics=("arbitrary",), vmem_limit_bytes=VMEM_LIMIT),
        name="moe_combine",
    )(pos if out_order is None else out_order(pos), y_sorted, fnw)
    return out, x_sorted


def _lane_row(v, offset):
    return jnp.zeros((1, LANES), F32).at[0, offset:offset + v.shape[0]].set(v.astype(F32))


def _expansion(offset, heads, width):
    r = jnp.arange(LANES)[:, None]
    c = jnp.arange(heads * width)[None, :]
    return (r == offset + c // width).astype(BF16)


def _row_tile(n, want):
    t = min(n, want)
    while n % t:
        t //= 2
    return t


def kernel(x, ln1_w, w_in, ssd_conv_w, ssd_conv_b, ssd_dt_bias, ssd_a_log, ssd_d, ssd_norm_w, w_ssd_proj,
           gdn_conv_w, gdn_dt_bias, gdn_a_log, gdn_norm_w, w_gdn_proj, b_merge, w_out, ln2_w,
           moe_w_group, moe_b_group, moe_w_expert, moe_b_expert, moe_w_gate, moe_w_up, moe_w_down,
           final_norm_w):
    bsz, seqlen, d = x.shape
    n = bsz * seqlen
    depth = w_in.shape[0]

    def swap_order(a):
        blocks = a.reshape((n // TIME_BLOCK, SUBLANES, SUBLANES) + a.shape[1:])
        return blocks.swapaxes(1, 2).reshape(a.shape)

    h = swap_order(x.reshape(n, d))

    o1 = SSD_D_INNER
    o2 = o1 + SSD_XBC
    o3 = o2 + SSD_HEADS
    o4 = o3 + 3 * GDN_DIM
    o5 = o4 + GDN_HEADS
    o6 = o5 + GDN_HEADS
    o7 = o6 + GDN_DIM

    e_ssd = _expansion(SM_DT, SSD_HEADS, SSD_HEAD_DIM)
    e_gdn_a = _expansion(SM_A, GDN_HEADS, GDN_HEAD_DIM)
    e_gdn_b = _expansion(SM_BETA, GDN_HEADS, GDN_HEAD_DIM)
    tm_proj = _row_tile(n, 1024)
    tm_merge = _row_tile(n, 512)
    tm_moe = _row_tile(n, 512)

    w_main = jnp.concatenate(
        [w_in[:, :, o1:o2], w_in[:, :, o3:o4], w_in[:, :, :o1], w_in[:, :, o7:], w_in[:, :, o6:o7]],
        axis=2).astype(BF16)
    w_small = jnp.concatenate(
        [w_in[:, :, o2:o3], w_in[:, :, o4:o5], w_in[:, :, o5:o6],
         jnp.zeros((depth, d, LANES - SSD_HEADS - 2 * GDN_HEADS), F32)], axis=2).astype(BF16)
    w_sp, w_gp, w_o = w_ssd_proj.astype(BF16), w_gdn_proj.astype(BF16), w_out.astype(BF16)
    w_gate, w_up, w_down = moe_w_gate.astype(BF16), moe_w_up.astype(BF16), moe_w_down.astype(BF16)

    slots = None
    for i in range(depth):
        proj, small = _in_proj(h, ln1_w[i][None, :], w_main, w_small, i, tm_proj, 1024)

        y = _ssd(proj, small, ssd_conv_w[i], ssd_conv_b[i][None, :],
                 _lane_row(ssd_dt_bias[i], SM_DT), _lane_row(ssd_a_log[i], SM_DT),
                 jnp.repeat(ssd_d[i], SSD_HEAD_DIM)[None, :], ssd_norm_w[i][None, :], e_ssd, bsz, seqlen)
        o = _gdn(proj, small, gdn_conv_w[i], _lane_row(gdn_dt_bias[i], SM_A), _lane_row(gdn_a_log[i], SM_A),
                 gdn_norm_w[i][None, :], e_gdn_a, e_gdn_b, bsz, seqlen)
        h = _merge(y, o, proj, h, b_merge[i][None, :], w_sp, w_gp, w_o, i, tm_merge)

        gap = jnp.zeros((d, RT_E - RT_G - MOE_GROUPS), F32)
        rest = jnp.zeros((d, LANES - RT_E - MOE_EXPERTS), F32)
        w_r = jnp.concatenate([moe_w_group[i], gap, moe_w_expert[i], rest], axis=1).T
        w_r_hi = w_r.astype(BF16)
        w_r_lo = (w_r - w_r_hi.astype(F32)).astype(BF16)
        b_r = jnp.broadcast_to(
            jnp.concatenate([moe_b_group[i], gap[0], moe_b_expert[i], rest[0]])[:, None], (LANES, LANES))
        h, slots = _moe(h, ln2_w[i][None, :], jnp.stack([w_r_hi, w_r_lo]), b_r, w_gate, w_up, w_down,
                        final_norm_w[None, :], i, tm_moe, final_norm=(i == depth - 1),
                        out_order=swap_order if i == depth - 1 else None, slots=slots)
    return h.reshape(bsz, seqlen, d)
```

```python
import functools

import jax
import jax.numpy as jnp
from jax import lax
from jax.experimental import pallas as pl
from jax.experimental.pallas import tpu as pltpu

F32 = jnp.float32
BF16 = jnp.bfloat16

EPS = 1e-6
CONV_K = 4
LANES = 128
SUBLANES = 8
TIME_BLOCK = SUBLANES * SUBLANES
HALO = (CONV_K - 1) * SUBLANES

D_MODEL = 1024
SSD_D_INNER = 2 * D_MODEL
SSD_HEAD_DIM = 64
SSD_HEADS = SSD_D_INNER // SSD_HEAD_DIM
SSD_GROUPS = 4
SSD_STATE = 128
SSD_CHUNK = 128
SSD_XBC = SSD_D_INNER + 2 * SSD_GROUPS * SSD_STATE
GDN_HEADS = 8
GDN_HEAD_DIM = 128
GDN_DIM = GDN_HEADS * GDN_HEAD_DIM
GDN_CHUNK = 64
MOE_GROUPS = 4
MOE_EPG = 4
MOE_EXPERTS = MOE_GROUPS * MOE_EPG
MOE_D_FF = 512

MAIN_COLS = SSD_XBC + 3 * GDN_DIM + SSD_D_INNER + 2 * D_MODEL + GDN_DIM
XBC_BLK, QKV_BLK = 0, 1
ZS_BLK, GATE_BLK = 3, 4
ZG_BLK = 10
SM_DT, SM_A, SM_BETA = 0, SSD_HEADS, SSD_HEADS + GDN_HEADS
RT_G, RT_E = 0, SUBLANES
MOE_BUCKET_ROWS = 32

VMEM_LIMIT = 56 * 1024 * 1024


def _dot(a, b):
    return jnp.dot(a, b, preferred_element_type=F32)


def _dot_nt(a, b):
    return lax.dot_general(a, b, (((1,), (1,)), ((), ())), preferred_element_type=F32)


def _split(x):
    hi = x.astype(BF16)
    lo = (x - hi.astype(F32)).astype(BF16)
    return hi, lo


def _dot_split_lhs(x, w):
    hi, lo = _split(x)
    return _dot(hi, w) + _dot(lo, w)


def _dot_split_rhs(w, x):
    hi, lo = _split(x)
    return _dot(w, hi) + _dot(w, lo)


def _sigmoid(x):
    return 1.0 / (1.0 + jnp.exp(-x))


def _silu(x):
    return x * _sigmoid(x)


def _softplus(x):
    return jnp.maximum(x, 0.0) + jnp.log1p(jnp.exp(-jnp.abs(x)))


def _time_index(r):
    return (r & ~(TIME_BLOCK - 1)) | ((r >> 3) & (SUBLANES - 1)) | ((r & (SUBLANES - 1)) << 3)


def _causal_conv(cur, tail_ref, w, first):
    t, c = cur.shape

    @pl.when(first)
    def _():
        tail_ref[...] = jnp.zeros_like(tail_ref)

    sub = lax.broadcasted_iota(jnp.int32, (SUBLANES, c), 0)
    prev_tail = tail_ref[...]
    outs = []
    for i in range(t // TIME_BLOCK):
        blk = cur[i * TIME_BLOCK:(i + 1) * TIME_BLOCK, :]
        tail = blk[TIME_BLOCK - HALO:, :]
        wrapped = [
            jnp.where(sub == 0,
                      pltpu.roll(prev_tail[j * SUBLANES:(j + 1) * SUBLANES, :], 1, axis=0),
                      pltpu.roll(tail[j * SUBLANES:(j + 1) * SUBLANES, :], 1, axis=0))
            for j in range(CONV_K - 1)]
        ext = jnp.concatenate(wrapped + [blk], axis=0)
        acc = None
        for k in range(CONV_K):
            term = ext[k * SUBLANES:k * SUBLANES + TIME_BLOCK, :] * w[k:k + 1, :]
            acc = term if acc is None else acc + term
        outs.append(acc)
        prev_tail = tail
    tail_ref[...] = prev_tail
    return jnp.concatenate(outs, axis=0)


def _inproj_kernel(x_ref, lnw_ref, w_ref, ws_ref, o_ref, os_ref, xn_ref):
    @pl.when(pl.program_id(1) == 0)
    def _():
        x = x_ref[...]
        xn = x * lax.rsqrt(jnp.mean(x * x, axis=-1, keepdims=True) + EPS) * lnw_ref[...]
        xn_ref[...] = xn.astype(BF16)
        os_ref[...] = _dot(xn_ref[...], ws_ref[...])

    o_ref[...] = _dot(xn_ref[...], w_ref[...]).astype(o_ref.dtype)


def _in_proj(x, lnw, w_main, w_small, layer, tm, tn):
    n, d = x.shape
    nm = w_main.shape[2]
    return pl.pallas_call(
        _inproj_kernel,
        grid=(n // tm, nm // tn),
        in_specs=[
            pl.BlockSpec((tm, d), lambda i, j: (i, 0)),
            pl.BlockSpec((1, d), lambda i, j: (0, 0)),
            pl.BlockSpec((pl.Squeezed(), d, tn), lambda i, j: (layer, 0, j)),
            pl.BlockSpec((pl.Squeezed(), d, LANES), lambda i, j: (layer, 0, 0)),
        ],
        out_specs=[
            pl.BlockSpec((tm, tn), lambda i, j: (i, j)),
            pl.BlockSpec((tm, LANES), lambda i, j: (i, 0)),
        ],
        out_shape=[jax.ShapeDtypeStruct((n, nm), BF16), jax.ShapeDtypeStruct((n, LANES), F32)],
        scratch_shapes=[pltpu.VMEM((tm, d), BF16)],
        compiler_params=pltpu.CompilerParams(
            dimension_semantics=("arbitrary", "arbitrary"), vmem_limit_bytes=VMEM_LIMIT),
        name="in_proj",
    )(x, lnw, w_main, w_small)


SSD_STEP_ROWS = 2 * SSD_CHUNK


def _ssd_kernel(xbc_ref, z_ref, sm_ref, convw_ref, convb_ref, dtb_ref, alog_ref, dexp_ref, normw_ref, e_ref,
                y_ref, halo_ref, state_ref, yacc_ref):
    q = SSD_CHUNK
    p = SSD_HEAD_DIM
    ns = SSD_STATE
    gw = (SSD_HEADS // SSD_GROUPS) * p
    first = pl.program_id(1) == 0

    @pl.when(first)
    def _():
        state_ref[...] = jnp.zeros_like(state_ref)

    conv = _causal_conv(xbc_ref[...].astype(F32), halo_ref, convw_ref[...], first) + convb_ref[...]
    xbc_all = _silu(conv)
    dt_all = _softplus(sm_ref[...] + dtb_ref[...])
    a_neg = -jnp.exp(alog_ref[...])
    row = lax.broadcasted_iota(jnp.int32, (q, q), 0)
    col = lax.broadcasted_iota(jnp.int32, (q, q), 1)
    causal = _time_index(row) >= _time_index(col)
    tri = jnp.where(causal, 1.0, 0.0).astype(BF16)
    lane = lax.broadcasted_iota(jnp.int32, (q, LANES), 1)

    for ci in range(xbc_ref.shape[0] // q):
        rs = slice(ci * q, (ci + 1) * q)
        xs = xbc_all[rs, :SSD_D_INNER]
        bm = xbc_all[rs, SSD_D_INNER:SSD_D_INNER + SSD_GROUPS * ns]
        cm = xbc_all[rs, SSD_D_INNER + SSD_GROUPS * ns:]
        dt = dt_all[rs, :]
        a_cs = _dot_split_rhs(tri, dt * a_neg)
        a_cs_t = a_cs.T
        e_acs = jnp.exp(a_cs)
        dec_end = jnp.exp(a_cs[q - 1:q, :] - a_cs)
        ex = _dot_split_lhs(jnp.concatenate([dt, dt * dec_end, e_acs], axis=0), e_ref[...])
        dt_x, dtdec_x, eacs_x = ex[:q], ex[q:2 * q], ex[2 * q:]

        xdt_b = (xs * dt_x).astype(BF16)
        xdd_b = (xs * dtdec_x).astype(BF16)
        for g in range(SSD_GROUPS):
            bg = bm[:, g * ns:(g + 1) * ns]
            cg_b = cm[:, g * ns:(g + 1) * ns].astype(BF16)
            scores = _dot_nt(cg_b, bg.astype(BF16))
            s_g = state_ref[:, g * gw:(g + 1) * gw]
            yacc_ref[rs, g * gw:(g + 1) * gw] = _dot(cg_b, s_g.astype(BF16)) * eacs_x[:, g * gw:(g + 1) * gw]
            new_g = _dot(bg.T.astype(BF16), xdd_b[:, g * gw:(g + 1) * gw])
            state_ref[:, g * gw:(g + 1) * gw] = s_g * eacs_x[q - 1:q, g * gw:(g + 1) * gw] + new_g
            for j in range(gw // LANES):
                h0 = (g * gw + j * LANES) // p
                c0 = g * gw + j * LANES
                ms = []
                for h in (h0, h0 + 1):
                    seg = a_cs[:, h:h + 1] - a_cs_t[h:h + 1, :]
                    ms.append((scores * jnp.exp(jnp.where(causal, seg, -jnp.inf))).astype(BF16))
                xp = xdt_b[:, c0:c0 + LANES]
                zero = jnp.zeros_like(xp)
                rhs = jnp.concatenate([jnp.where(lane < p, xp, zero), jnp.where(lane >= p, xp, zero)], axis=0)
                yacc_ref[rs, c0:c0 + LANES] += _dot(jnp.concatenate(ms, axis=1), rhs)

        y = yacc_ref[rs, :] + dexp_ref[...] * xs
        y = y * _silu(z_ref[rs, :].astype(F32))
        y = y * lax.rsqrt(jnp.mean(y * y, axis=-1, keepdims=True) + EPS) * normw_ref[...]
        y_ref[rs, :] = y.astype(BF16)


def _ssd(proj, small, convw, convb, dtb, alog, dexp, normw, expand, bsz, seqlen):
    q = SSD_STEP_ROWS
    nc = seqlen // q
    n = bsz * seqlen
    rowmap = lambda b, c: b * nc + c
    const = lambda b, c: (0, 0)
    return pl.pallas_call(
        _ssd_kernel,
        grid=(bsz, nc),
        in_specs=[
            pl.BlockSpec((q, SSD_XBC), lambda b, c: (rowmap(b, c), XBC_BLK)),
            pl.BlockSpec((q, SSD_D_INNER), lambda b, c: (rowmap(b, c), ZS_BLK)),
            pl.BlockSpec((q, LANES), lambda b, c: (rowmap(b, c), 0)),
            pl.BlockSpec((CONV_K, SSD_XBC), const),
            pl.BlockSpec((1, SSD_XBC), const),
            pl.BlockSpec((1, LANES), const),
            pl.BlockSpec((1, LANES), const),
            pl.BlockSpec((1, SSD_D_INNER), const),
            pl.BlockSpec((1, SSD_D_INNER), const),
            pl.BlockSpec((LANES, SSD_D_INNER), const),
        ],
        out_specs=pl.BlockSpec((q, SSD_D_INNER), lambda b, c: (rowmap(b, c), 0)),
        out_shape=jax.ShapeDtypeStruct((n, SSD_D_INNER), BF16),
        scratch_shapes=[
            pltpu.VMEM((HALO, SSD_XBC), F32),
            pltpu.VMEM((SSD_STATE, SSD_D_INNER), F32),
            pltpu.VMEM((q, SSD_D_INNER), F32),
        ],
        compiler_params=pltpu.CompilerParams(
            dimension_semantics=("arbitrary", "arbitrary"), vmem_limit_bytes=VMEM_LIMIT),
        name="ssd",
    )(proj, proj, small, convw, convb, dtb, alog, dexp, normw, expand)


GDN_PREP_ROWS = 4 * GDN_CHUNK
GDN_CHUNK_SHIFT = GDN_CHUNK.bit_length() - 1


def _unit_lower_inverse_many(ms, row, col):
    c = ms[0].shape[0]
    diff = row ^ col
    eye = jnp.where(row == col, 1.0, 0.0)
    ts = [eye - jnp.where((diff >> 1) == 0, m, 0.0) for m in ms]
    shift = 1
    while (1 << shift) < c:
        sel = (diff >> shift) == 1
        tbs = [t.astype(BF16) for t in ts]
        xs = [_dot(tb, jnp.where(sel, m, 0.0).astype(BF16)) for tb, m in zip(tbs, ms)]
        ts = [t - _dot(x.astype(BF16), tb) for t, x, tb in zip(ts, xs, tbs)]
        shift += 1
    return ts


def _gdn_prep_kernel(qkv_ref, sm_ref, convw_ref, dtb_ref, alog_ref, eg_ref, eb_ref,
                     u_ref, w_ref, qd_ref, aqk_ref, kdt_ref, gl_ref, halo_ref):
    c = GDN_CHUNK
    dh = GDN_HEAD_DIM
    t = GDN_PREP_ROWS
    ncb = t // c
    first = pl.program_id(1) == 0
    qkv = _silu(_causal_conv(qkv_ref[...].astype(F32), halo_ref, convw_ref[...], first))

    sm = sm_ref[...]
    lane1 = lax.broadcasted_iota(jnp.int32, (t, LANES), 1)
    is_a = (lane1 >= SM_A) & (lane1 < SM_A + GDN_HEADS)
    log_decay = jnp.where(is_a, -jnp.exp(alog_ref[...]) * _softplus(sm + dtb_ref[...]), 0.0)
    beta = _sigmoid(sm)
    row_t = lax.broadcasted_iota(jnp.int32, (t, t), 0)
    col_t = lax.broadcasted_iota(jnp.int32, (t, t), 1)
    same_chunk = (row_t >> GDN_CHUNK_SHIFT) == (col_t >> GDN_CHUNK_SHIFT)
    tri = jnp.where(same_chunk, jnp.where(_time_index(row_t) >= _time_index(col_t), 1.0, 0.0), 0.0).astype(BF16)
    gc = _dot_split_rhs(tri, log_decay)
    gc_t = gc.T
    g_last = jnp.concatenate(
        [jnp.broadcast_to(gc[(i + 1) * c - 1:(i + 1) * c, :], (c, LANES)) for i in range(ncb)], axis=0)
    ex = _dot_split_lhs(jnp.concatenate([jnp.exp(gc), jnp.exp(g_last - gc)], axis=0), eg_ref[...])
    egc_x, erev_x = ex[:t], ex[t:]
    beta_x = _dot_split_lhs(beta, eb_ref[...])

    row = _time_index(lax.broadcasted_iota(jnp.int32, (c, c), 0))
    col = _time_index(lax.broadcasted_iota(jnp.int32, (c, c), 1))
    incl = row >= col
    strict = row > col

    qs, ks, kbs, rhs = [], [], [], []
    for h in range(GDN_HEADS):
        hs = slice(h * dh, (h + 1) * dh)
        qh = qkv[:, h * dh:(h + 1) * dh]
        kh = qkv[:, GDN_DIM + h * dh:GDN_DIM + (h + 1) * dh]
        vh = qkv[:, 2 * GDN_DIM + h * dh:2 * GDN_DIM + (h + 1) * dh]
        qh = qh * lax.rsqrt(jnp.sum(qh * qh, axis=-1, keepdims=True) + EPS) * (dh ** -0.5)
        kh = kh * lax.rsqrt(jnp.sum(kh * kh, axis=-1, keepdims=True) + EPS)
        kb = kh * beta_x[:, hs]
        qs.append(qh)
        ks.append(kh)
        kbs.append(kb)
        rhs.append(jnp.concatenate([vh * beta_x[:, hs], kb * egc_x[:, hs]], axis=1).astype(BF16))

    items = [(i, h) for i in range(ncb) for h in range(GDN_HEADS)]
    ms, aqks = [], []
    for i, h in items:
        rs = slice(i * c, (i + 1) * c)
        la = SM_A + h
        decay = jnp.exp(jnp.where(incl, gc[rs, la:la + 1] - gc_t[la:la + 1, rs], -jnp.inf))
        nt = _dot_nt(jnp.concatenate([kbs[h][rs], qs[h][rs]], axis=0).astype(BF16), ks[h][rs].astype(BF16))
        ms.append(jnp.where(strict, nt[:c] * decay, 0.0))
        aqks.append((nt[c:] * decay).astype(BF16))
    t_invs = _unit_lower_inverse_many(ms, row, col)
    uws = [_dot(t_inv.astype(BF16), rhs[h][i * c:(i + 1) * c]) for (i, h), t_inv in zip(items, t_invs)]

    def by_chunk(pieces):
        return jnp.concatenate(
            [jnp.concatenate(pieces[i * GDN_HEADS:(i + 1) * GDN_HEADS], axis=1) for i in range(ncb)], axis=0)

    u_ref[...] = by_chunk([uw[:, :dh] for uw in uws]).astype(BF16)
    w_ref[...] = by_chunk([uw[:, dh:] for uw in uws]).astype(BF16)
    aqk_ref[...] = by_chunk(aqks)
    qd_ref[...] = jnp.concatenate(
        [qs[h] * egc_x[:, h * dh:(h + 1) * dh] for h in range(GDN_HEADS)], axis=1).astype(BF16)
    for i in range(ncb):
        rs = slice(i * c, (i + 1) * c)
        kdt_ref[i] = jnp.concatenate(
            [(ks[h][rs] * erev_x[rs, h * dh:(h + 1) * dh]).T for h in range(GDN_HEADS)], axis=1).astype(BF16)
        gl_ref[i] = egc_x[(i + 1) * c - 1:(i + 1) * c, :]


def _gdn_scan_kernel(u_ref, w_ref, qd_ref, aqk_ref, kdt_ref, gl_ref, zg_ref, normw_ref, o_ref, state_ref):
    c = GDN_CHUNK
    dh = GDN_HEAD_DIM

    @pl.when(pl.program_id(0) == 0)
    def _():
        state_ref[...] = jnp.zeros_like(state_ref)

    zero = jnp.zeros((c, dh), BF16)
    for b in range(u_ref.shape[0]):
        outs = []
        for pr in range(GDN_HEADS // 2):
            heads = (2 * pr, 2 * pr + 1)
            ss, q_s, v_news = [], [], []
            for h in heads:
                hs = slice(h * dh, (h + 1) * dh)
                s = state_ref[b, h]
                p = _dot(jnp.concatenate([w_ref[b, :, hs], qd_ref[b, :, hs]], axis=0), s.astype(BF16))
                ss.append(s)
                q_s.append(p[c:])
                v_news.append((u_ref[b, :, hs].astype(F32) - p[:c]).astype(BF16))
            lhs = jnp.concatenate([aqk_ref[b, :, pr * dh:(pr + 1) * dh], kdt_ref[b, 0, :, pr * dh:(pr + 1) * dh]],
                                  axis=0)
            bd = jnp.concatenate([jnp.concatenate([v_news[0], zero], axis=1),
                                  jnp.concatenate([zero, v_news[1]], axis=1)], axis=0)
            r = _dot(lhs, bd)
            for j, h in enumerate(heads):
                hs = slice(h * dh, (h + 1) * dh)
                state_ref[b, h] = ss[j] * gl_ref[b, 0, :, hs] + r[c:, j * dh:(j + 1) * dh]
                o = q_s[j] + r[:c, j * dh:(j + 1) * dh]
                o = o * lax.rsqrt(jnp.mean(o * o, axis=-1, keepdims=True) + EPS) * normw_ref[...]
                outs.append(o * _silu(zg_ref[b, :, hs].astype(F32)))
        o_ref[b] = jnp.concatenate(outs, axis=1).astype(BF16)


def _gdn(proj, small, convw, dtb, alog, normw, expand_a, expand_b, bsz, seqlen):
    c = GDN_CHUNK
    t = GDN_PREP_ROWS
    nc = seqlen // c
    nb = seqlen // t
    n = bsz * seqlen
    hh = GDN_HEADS * c
    rowmap = lambda b, i: b * nb + i
    const = lambda b, i: (0, 0)
    params = pltpu.CompilerParams(dimension_semantics=("arbitrary", "arbitrary"), vmem_limit_bytes=VMEM_LIMIT)
    u, w, qd, aqk, kdt, gl = pl.pallas_call(
        _gdn_prep_kernel,
        grid=(bsz, nb),
        in_specs=[
            pl.BlockSpec((t, 3 * GDN_DIM), lambda b, i: (rowmap(b, i), QKV_BLK)),
            pl.BlockSpec((t, LANES), lambda b, i: (rowmap(b, i), 0)),
            pl.BlockSpec((CONV_K, 3 * GDN_DIM), const),
            pl.BlockSpec((1, LANES), const),
            pl.BlockSpec((1, LANES), const),
            pl.BlockSpec((LANES, GDN_DIM), const),
            pl.BlockSpec((LANES, GDN_DIM), const),
        ],
        out_specs=[
            pl.BlockSpec((t, GDN_DIM), lambda b, i: (rowmap(b, i), 0)),
            pl.BlockSpec((t, GDN_DIM), lambda b, i: (rowmap(b, i), 0)),
            pl.BlockSpec((t, GDN_DIM), lambda b, i: (rowmap(b, i), 0)),
            pl.BlockSpec((t, hh), lambda b, i: (rowmap(b, i), 0)),
            pl.BlockSpec((t // c, GDN_HEAD_DIM, hh), lambda b, i: (rowmap(b, i), 0, 0)),
            pl.BlockSpec((t // c, 1, GDN_DIM), lambda b, i: (rowmap(b, i), 0, 0)),
        ],
        out_shape=[
            jax.ShapeDtypeStruct((n, GDN_DIM), BF16),
            jax.ShapeDtypeStruct((n, GDN_DIM), BF16),
            jax.ShapeDtypeStruct((n, GDN_DIM), BF16),
            jax.ShapeDtypeStruct((n, hh), BF16),
            jax.ShapeDtypeStruct((n // c, GDN_HEAD_DIM, hh), BF16),
            jax.ShapeDtypeStruct((n // c, 1, GDN_DIM), F32),
        ],
        scratch_shapes=[pltpu.VMEM((HALO, 3 * GDN_DIM), F32)],
        compiler_params=params,
        name="gdn_prep",
    )(proj, small, convw, dtb, alog, expand_a, expand_b)

    seq3 = lambda i: (0, i, 0)
    seq4 = lambda i: (0, i, 0, 0)
    o = pl.pallas_call(
        _gdn_scan_kernel,
        grid=(nc,),
        in_specs=[
            pl.BlockSpec((bsz, c, GDN_DIM), seq3),
            pl.BlockSpec((bsz, c, GDN_DIM), seq3),
            pl.BlockSpec((bsz, c, GDN_DIM), seq3),
            pl.BlockSpec((bsz, c, hh), seq3),
            pl.BlockSpec((bsz, 1, GDN_HEAD_DIM, hh), seq4),
            pl.BlockSpec((bsz, 1, 1, GDN_DIM), seq4),
            pl.BlockSpec((bsz, c, GDN_DIM), lambda i: (0, i, ZG_BLK)),
            pl.BlockSpec((1, GDN_HEAD_DIM), lambda i: (0, 0)),
        ],
        out_specs=pl.BlockSpec((bsz, c, GDN_DIM), seq3),
        out_shape=jax.ShapeDtypeStruct((bsz, seqlen, GDN_DIM), BF16),
        scratch_shapes=[pltpu.VMEM((bsz, GDN_HEADS, GDN_HEAD_DIM, GDN_HEAD_DIM), F32)],
        compiler_params=pltpu.CompilerParams(dimension_semantics=("arbitrary",), vmem_limit_bytes=VMEM_LIMIT),
        name="gdn_scan",
    )(u.reshape(bsz, seqlen, GDN_DIM), w.reshape(bsz, seqlen, GDN_DIM), qd.reshape(bsz, seqlen, GDN_DIM),
      aqk.reshape(bsz, seqlen, hh), kdt.reshape(bsz, nc, GDN_HEAD_DIM, hh), gl.reshape(bsz, nc, 1, GDN_DIM),
      proj.reshape(bsz, seqlen, MAIN_COLS), normw)
    return o.reshape(n, GDN_DIM)


def _merge_kernel(y_ref, o_ref, gate_ref, h_ref, bm_ref, wsp_ref, wgp_ref, wout_ref, out_ref):
    d = h_ref.shape[1]
    y_ssd = _dot(y_ref[...], wsp_ref[...])
    y_gdn = _dot(o_ref[...], wgp_ref[...])
    gates = _sigmoid(gate_ref[...].astype(F32) + bm_ref[...])
    merged = gates[:, :d] * y_ssd + gates[:, d:] * y_gdn
    out_ref[...] = h_ref[...] + _dot(merged.astype(BF16), wout_ref[...])


def _merge(y, o, proj, h, b_merge, w_sp, w_gp, w_out, layer, tm):
    n, d = h.shape
    const = lambda i: (0, 0)
    return pl.pallas_call(
        _merge_kernel,
        grid=(n // tm,),
        in_specs=[
            pl.BlockSpec((tm, SSD_D_INNER), lambda i: (i, 0)),
            pl.BlockSpec((tm, GDN_DIM), lambda i: (i, 0)),
            pl.BlockSpec((tm, 2 * d), lambda i: (i, GATE_BLK)),
            pl.BlockSpec((tm, d), lambda i: (i, 0)),
            pl.BlockSpec((1, 2 * d), const),
            pl.BlockSpec((pl.Squeezed(), SSD_D_INNER, d), lambda i: (layer, 0, 0)),
            pl.BlockSpec((pl.Squeezed(), GDN_DIM, d), lambda i: (layer, 0, 0)),
            pl.BlockSpec((pl.Squeezed(), d, d), lambda i: (layer, 0, 0)),
        ],
        out_specs=pl.BlockSpec((tm, d), lambda i: (i, 0)),
        out_shape=jax.ShapeDtypeStruct((n, d), F32),
        compiler_params=pltpu.CompilerParams(
            dimension_semantics=("arbitrary",), vmem_limit_bytes=VMEM_LIMIT),
        name="merge",
    )(y, o, proj, h, b_merge, w_sp, w_gp, w_out)


MOE_PAIRS = MOE_EPG * (MOE_EPG - 1) // 2
MOE_BUCKETS = MOE_GROUPS * MOE_PAIRS
MOE_TILE = 256
RT_BUCKET, RT_WLO, RT_WHI, RT_RANK = 0, 1, 2, 3


def _route(logits):
    t = logits.shape[1]
    neg = -jnp.inf
    big = jnp.int32(LANES)
    gl = logits[RT_G:RT_G + MOE_GROUPS, :]
    grow = lax.broadcasted_iota(jnp.int32, (MOE_GROUPS, t), 0)
    gmax = jnp.max(gl, axis=0, keepdims=True)
    gsel = jnp.min(jnp.where(gl == gmax, grow, big), axis=0, keepdims=True)
    group_w = 1.0 / jnp.sum(jnp.exp(gl - gmax), axis=0, keepdims=True)
    erow = lax.broadcasted_iota(jnp.int32, (MOE_EXPERTS, t), 0)
    el = jnp.where((erow >> (MOE_EPG.bit_length() - 1)) == gsel, logits[RT_E:RT_E + MOE_EXPERTS, :], neg)
    m1 = jnp.max(el, axis=0, keepdims=True)
    i1 = jnp.min(jnp.where(el == m1, erow, big), axis=0, keepdims=True)
    el2 = jnp.where(erow == i1, neg, el)
    m2 = jnp.max(el2, axis=0, keepdims=True)
    i2 = jnp.min(jnp.where(el2 == m2, erow, big), axis=0, keepdims=True)
    z = jnp.sum(jnp.exp(el - m1), axis=0, keepdims=True)
    p1 = 1.0 / z
    p2 = jnp.exp(m2 - m1) / z
    tot = p1 + p2
    w1 = group_w * (p1 / tot)
    w2 = group_w * (p2 / tot)
    l1 = i1 - gsel * MOE_EPG
    l2 = i2 - gsel * MOE_EPG
    lo = jnp.minimum(l1, l2)
    hi = jnp.maximum(l1, l2)
    pair = ((lo * (2 * MOE_EPG - 1 - lo)) >> 1) + (hi - lo - 1)
    bucket = gsel * MOE_PAIRS + pair
    first_lo = l1 < l2
    return bucket, jnp.where(first_lo, w1, w2), jnp.where(first_lo, w2, w1)


def _moe_rmsnorm(x, lnw):
    return x * lax.rsqrt(jnp.mean(x * x, axis=-1, keepdims=True) + EPS) * lnw


def _router_kernel(h_ref, lnw_ref, wr_ref, br_ref, info_ref, counts_ref, run_ref):
    t = h_ref.shape[0]

    @pl.when(pl.program_id(0) == 0)
    def _():
        run_ref[...] = jnp.zeros_like(run_ref)

    hi, lo = _split(_moe_rmsnorm(h_ref[...], lnw_ref[...]))
    whi, wlo = wr_ref[0], wr_ref[1]
    logits = _dot_nt(whi, hi) + _dot_nt(whi, lo) + _dot_nt(wlo, hi) + br_ref[:, 0:1]
    bucket, w_lo, w_hi = _route(logits)
    brow = lax.broadcasted_iota(jnp.int32, (MOE_BUCKET_ROWS, t), 0)
    onehot = jnp.where(brow == bucket, 1.0, 0.0)
    row = lax.broadcasted_iota(jnp.int32, (t, t), 0)
    col = lax.broadcasted_iota(jnp.int32, (t, t), 1)
    earlier = _dot(onehot.astype(BF16), jnp.where(row < col, 1.0, 0.0).astype(BF16))
    rank = jnp.sum(onehot * (earlier + run_ref[:, 0:1]), axis=0, keepdims=True)
    run_ref[...] += jnp.sum(onehot, axis=1, keepdims=True)
    record = jnp.concatenate(
        [bucket.astype(F32), w_lo, w_hi, rank, jnp.zeros((LANES - 4, t), F32)], axis=0)
    info_ref[...] = record.T
    counts_ref[...] = run_ref[...]


def _dispatch_kernel(pos_ref, h_ref, info_ref, xs_in_hbm, xs_hbm, buf_ref, sem):
    del xs_in_hbm
    tm, d = h_ref.shape
    i = pl.program_id(0)
    last = pl.num_programs(0) - 1
    base = i * tm
    slot = i % 2

    def wait_rows(s):
        pltpu.make_async_copy(buf_ref.at[s], xs_hbm.at[pl.ds(0, tm)], sem.at[s]).wait()

    @pl.when(i >= 2)
    def _():
        wait_rows(slot)

    buf_ref[slot, :, :d] = h_ref[...]
    buf_ref[slot, :, d:] = info_ref[...]

    def issue(r, carry):
        pltpu.make_async_copy(buf_ref.at[slot, pl.ds(r, 1)], xs_hbm.at[pl.ds(pos_ref[base + r], 1)],
                              sem.at[slot]).start()
        return carry

    lax.fori_loop(0, tm, issue, 0, unroll=8)

    @pl.when(i == last)
    def _():
        wait_rows(slot)

    @pl.when((i == last) & (i >= 1))
    def _():
        wait_rows(1 - slot)


def _experts_kernel(lo_ref, hi_ref, nt_ref, x_ref, lnw_ref, wg_lo, wu_lo, wd_lo, wg_hi, wu_hi, wd_hi, y_ref):
    live = pl.program_id(0) < nt_ref[0]
    d = y_ref.shape[1]

    @pl.when(live)
    def _():
        xn = _moe_rmsnorm(x_ref[:, :d], lnw_ref[...]).astype(BF16)
        wts = x_ref[:, d + RT_WLO:d + RT_WHI + 1]
        y = None
        for j, (wg, wu, wd) in enumerate(((wg_lo, wu_lo, wd_lo), (wg_hi, wu_hi, wd_hi))):
            hid = _silu(_dot(xn, wg[0])) * _dot(xn, wu[0]) * wts[:, j:j + 1]
            part = _dot(hid.astype(BF16), wd[0])
            y = part if y is None else y + part
        y_ref[...] = x_ref[:, :d] + y

    @pl.when(jnp.logical_not(live))
    def _():
        y_ref[...] = jnp.zeros_like(y_ref)


def _combine_kernel(pos_ref, y_hbm, fnw_ref, out_ref, buf_ref, sem, *, final_norm):
    tm = out_ref.shape[0]
    i = pl.program_id(0)
    slot = i % 2

    def issue_tile(tile, s):
        base = tile * tm

        def issue(r, carry):
            pltpu.make_async_copy(y_hbm.at[pl.ds(pos_ref[base + r], 1)], buf_ref.at[s, pl.ds(r, 1)],
                                  sem.at[s]).start()
            return carry

        lax.fori_loop(0, tm, issue, 0, unroll=8)

    @pl.when(i == 0)
    def _():
        issue_tile(0, 0)

    @pl.when(i + 1 < pl.num_programs(0))
    def _():
        issue_tile(i + 1, 1 - slot)

    pltpu.make_async_copy(y_hbm.at[pl.ds(0, tm)], buf_ref.at[slot], sem.at[slot]).wait()
    y = buf_ref[slot]
    if final_norm:
        y = _moe_rmsnorm(y, fnw_ref[...])
    out_ref[...] = y


def _moe(h, lnw, w_router, b_router, w_gate, w_up, w_down, fnw, layer, tm, final_norm, out_order, slots):
    n, d = h.shape
    dff = w_gate.shape[-1]
    tile = MOE_TILE
    n_tiles = n // tile + MOE_BUCKETS
    p = n_tiles * tile
    const = lambda i: (0, 0)
    info, counts = pl.pallas_call(
        _router_kernel,
        grid=(n // tm,),
        in_specs=[
            pl.BlockSpec((tm, d), lambda i: (i, 0)),
            pl.BlockSpec((1, d), const),
            pl.BlockSpec((2, LANES, d), lambda i: (0, 0, 0)),
            pl.BlockSpec((LANES, LANES), const),
        ],
        out_specs=[pl.BlockSpec((tm, LANES), lambda i: (i, 0)), pl.BlockSpec((MOE_BUCKET_ROWS, LANES), const)],
        out_shape=[jax.ShapeDtypeStruct((n, LANES), F32), jax.ShapeDtypeStruct((MOE_BUCKET_ROWS, LANES), F32)],
        scratch_shapes=[pltpu.VMEM((MOE_BUCKET_ROWS, LANES), F32)],
        compiler_params=pltpu.CompilerParams(dimension_semantics=("arbitrary",), vmem_limit_bytes=VMEM_LIMIT),
        name="moe_router",
    )(h, lnw, w_router, b_router)

    tiles_per = (counts[:MOE_BUCKETS, 0].astype(jnp.int32) + tile - 1) // tile
    tile_end = jnp.cumsum(tiles_per)
    first_slot = ((tile_end - tiles_per) * tile).astype(F32)
    bucket_is = info[:, RT_BUCKET:RT_BUCKET + 1] == jnp.arange(MOE_BUCKETS, dtype=F32)[None, :]
    pos = (jnp.sum(jnp.where(bucket_is, first_slot[None, :], 0.0), axis=1) + info[:, RT_RANK]).astype(jnp.int32)
    n_live = tile_end[-1]
    tile_bucket = jnp.searchsorted(tile_end, jnp.minimum(jnp.arange(n_tiles, dtype=jnp.int32), n_live - 1),
                                   side="right").astype(jnp.int32)
    pairs = [(a, b) for a in range(MOE_EPG) for b in range(a + 1, MOE_EPG)]
    grp = tile_bucket // MOE_PAIRS
    tile_lo = grp * MOE_EPG + jnp.array([a for a, _ in pairs], jnp.int32)[tile_bucket % MOE_PAIRS]
    tile_hi = grp * MOE_EPG + jnp.array([b for _, b in pairs], jnp.int32)[tile_bucket % MOE_PAIRS]

    x_sorted = pl.pallas_call(
        _dispatch_kernel,
        grid_spec=pltpu.PrefetchScalarGridSpec(
            num_scalar_prefetch=1, grid=(n // tile,),
            in_specs=[
                pl.BlockSpec((tile, d), lambda i, pos: (i, 0)),
                pl.BlockSpec((tile, LANES), lambda i, pos: (i, 0)),
                pl.BlockSpec(memory_space=pl.ANY),
            ],
            out_specs=pl.BlockSpec(memory_space=pl.ANY),
            scratch_shapes=[pltpu.VMEM((2, tile, d + LANES), F32), pltpu.SemaphoreType.DMA((2,))]),
        out_shape=jax.ShapeDtypeStruct((p, d + LANES), F32),
        input_output_aliases={3: 0},
        compiler_params=pltpu.CompilerParams(dimension_semantics=("arbitrary",), vmem_limit_bytes=VMEM_LIMIT),
        name="moe_dispatch",
    )(pos, h, info, jnp.zeros((p, d + LANES), F32) if slots is None else slots)

    lo_map = lambda t, lo, hi, nt: (layer, lo[t], 0, 0)
    hi_map = lambda t, lo, hi, nt: (layer, hi[t], 0, 0)
    y_sorted = pl.pallas_call(
        _experts_kernel,
        grid_spec=pltpu.PrefetchScalarGridSpec(
            num_scalar_prefetch=3, grid=(n_tiles,),
            in_specs=[
                pl.BlockSpec((tile, d + LANES), lambda t, lo, hi, nt: (t, 0)),
                pl.BlockSpec((1, d), lambda t, lo, hi, nt: (0, 0)),
                pl.BlockSpec((pl.Squeezed(), 1, d, dff), lo_map),
                pl.BlockSpec((pl.Squeezed(), 1, d, dff), lo_map),
                pl.BlockSpec((pl.Squeezed(), 1, dff, d), lo_map),
                pl.BlockSpec((pl.Squeezed(), 1, d, dff), hi_map),
                pl.BlockSpec((pl.Squeezed(), 1, d, dff), hi_map),
                pl.BlockSpec((pl.Squeezed(), 1, dff, d), hi_map),
            ],
            out_specs=pl.BlockSpec((tile, d), lambda t, lo, hi, nt: (t, 0))),
        out_shape=jax.ShapeDtypeStruct((p, d), F32),
        compiler_params=pltpu.CompilerParams(dimension_semantics=("arbitrary",), vmem_limit_bytes=VMEM_LIMIT),
        name="moe_experts",
    )(tile_lo, tile_hi, n_live.reshape(1).astype(jnp.int32), x_sorted, lnw,
      w_gate, w_up, w_down, w_gate, w_up, w_down)

    out = pl.pallas_call(
        functools.partial(_combine_kernel, final_norm=final_norm),
        grid_spec=pltpu.PrefetchScalarGridSpec(
            num_scalar_prefetch=1, grid=(n // tile,),
            in_specs=[
                pl.BlockSpec(memory_space=pl.ANY),
                pl.BlockSpec((1, d), lambda i, pos: (0, 0)),
            ],
            out_specs=pl.BlockSpec((tile, d), lambda i, pos: (i, 0)),
            scratch_shapes=[pltpu.VMEM((2, tile, d), F32), pltpu.SemaphoreType.DMA((2,))]),
        out_shape=jax.ShapeDtypeStruct((n, d), F32),
        compiler_params=pltpu.CompilerParams(dimension_semantics=("arbitrary",), vmem_limit_bytes=VMEM_LIMIT),
        name="moe_combine",
    )(pos if out_order is None else out_order(pos), y_sorted, fnw)
    return out, x_sorted


def _lane_row(v, offset):
    return jnp.zeros((1, LANES), F32).at[0, offset:offset + v.shape[0]].set(v.astype(F32))


def _expansion(offset, heads, width):
    r = jnp.arange(LANES)[:, None]
    c = jnp.arange(heads * width)[None, :]
    return (r == offset + c // width).astype(BF16)


def _row_tile(n, want):
    t = min(n, want)
    while n % t:
        t //= 2
    return t


def kernel(x, ln1_w, w_in, ssd_conv_w, ssd_conv_b, ssd_dt_bias, ssd_a_log, ssd_d, ssd_norm_w, w_ssd_proj,
           gdn_conv_w, gdn_dt_bias, gdn_a_log, gdn_norm_w, w_gdn_proj, b_merge, w_out, ln2_w,
           moe_w_group, moe_b_group, moe_w_expert, moe_b_expert, moe_w_gate, moe_w_up, moe_w_down,
           final_norm_w):
    bsz, seqlen, d = x.shape
    n = bsz * seqlen
    depth = w_in.shape[0]

    def swap_order(a):
        blocks = a.reshape((n // TIME_BLOCK, SUBLANES, SUBLANES) + a.shape[1:])
        return blocks.swapaxes(1, 2).reshape(a.shape)

    h = swap_order(x.reshape(n, d))

    o1 = SSD_D_INNER
    o2 = o1 + SSD_XBC
    o3 = o2 + SSD_HEADS
    o4 = o3 + 3 * GDN_DIM
    o5 = o4 + GDN_HEADS
    o6 = o5 + GDN_HEADS
    o7 = o6 + GDN_DIM

    e_ssd = _expansion(SM_DT, SSD_HEADS, SSD_HEAD_DIM)
    e_gdn_a = _expansion(SM_A, GDN_HEADS, GDN_HEAD_DIM)
    e_gdn_b = _expansion(SM_BETA, GDN_HEADS, GDN_HEAD_DIM)
    tm_proj = _row_tile(n, 1024)
    tm_merge = _row_tile(n, 512)
    tm_moe = _row_tile(n, 512)

    w_main = jnp.concatenate(
        [w_in[:, :, o1:o2], w_in[:, :, o3:o4], w_in[:, :, :o1], w_in[:, :, o7:], w_in[:, :, o6:o7]],
        axis=2).astype(BF16)
    w_small = jnp.concatenate(
        [w_in[:, :, o2:o3], w_in[:, :, o4:o5], w_in[:, :, o5:o6],
         jnp.zeros((depth, d, LANES - SSD_HEADS - 2 * GDN_HEADS), F32)], axis=2).astype(BF16)
    w_sp, w_gp, w_o = w_ssd_proj.astype(BF16), w_gdn_proj.astype(BF16), w_out.astype(BF16)
    w_gate, w_up, w_down = moe_w_gate.astype(BF16), moe_w_up.astype(BF16), moe_w_down.astype(BF16)

    slots = None
    for i in range(depth):
        proj, small = _in_proj(h, ln1_w[i][None, :], w_main, w_small, i, tm_proj, 1024)

        y = _ssd(proj, small, ssd_conv_w[i], ssd_conv_b[i][None, :],
                 _lane_row(ssd_dt_bias[i], SM_DT), _lane_row(ssd_a_log[i], SM_DT),
                 jnp.repeat(ssd_d[i], SSD_HEAD_DIM)[None, :], ssd_norm_w[i][None, :], e_ssd, bsz, seqlen)
        o = _gdn(proj, small, gdn_conv_w[i], _lane_row(gdn_dt_bias[i], SM_A), _lane_row(gdn_a_log[i], SM_A),
                 gdn_norm_w[i][None, :], e_gdn_a, e_gdn_b, bsz, seqlen)
        h = _merge(y, o, proj, h, b_merge[i][None, :], w_sp, w_gp, w_o, i, tm_merge)

        gap = jnp.zeros((d, RT_E - RT_G - MOE_GROUPS), F32)
        rest = jnp.zeros((d, LANES - RT_E - MOE_EXPERTS), F32)
        w_r = jnp.concatenate([moe_w_group[i], gap, moe_w_expert[i], rest], axis=1).T
        w_r_hi = w_r.astype(BF16)
        w_r_lo = (w_r - w_r_hi.astype(F32)).astype(BF16)
        b_r = jnp.broadcast_to(
            jnp.concatenate([moe_b_group[i], gap[0], moe_b_expert[i], rest[0]])[:, None], (LANES, LANES))
        h, slots = _moe(h, ln2_w[i][None, :], jnp.stack([w_r_hi, w_r_lo]), b_r, w_gate, w_up, w_down,
                        final_norm_w[None, :], i, tm_moe, final_norm=(i == depth - 1),
                        out_order=swap_order if i == depth - 1 else None, slots=slots)
    return h.reshape(bsz, seqlen, d)
```

```python
import functools

import jax
import jax.numpy as jnp
from jax import lax
from jax.experimental import pallas as pl
from jax.experimental.pallas import tpu as pltpu

F32 = jnp.float32
BF16 = jnp.bfloat16

EPS = 1e-6
CONV_K = 4
LANES = 128
SUBLANES = 8
TIME_BLOCK = SUBLANES * SUBLANES
HALO = (CONV_K - 1) * SUBLANES

D_MODEL = 1024
SSD_D_INNER = 2 * D_MODEL
SSD_HEAD_DIM = 64
SSD_HEADS = SSD_D_INNER // SSD_HEAD_DIM
SSD_GROUPS = 4
SSD_STATE = 128
SSD_CHUNK = 128
SSD_XBC = SSD_D_INNER + 2 * SSD_GROUPS * SSD_STATE
GDN_HEADS = 8
GDN_HEAD_DIM = 128
GDN_DIM = GDN_HEADS * GDN_HEAD_DIM
GDN_CHUNK = 64
MOE_GROUPS = 4
MOE_EPG = 4
MOE_EXPERTS = MOE_GROUPS * MOE_EPG
MOE_D_FF = 512

MAIN_COLS = SSD_XBC + 3 * GDN_DIM + SSD_D_INNER + 2 * D_MODEL + GDN_DIM
XBC_BLK, QKV_BLK = 0, 1
ZS_BLK, GATE_BLK = 3, 4
ZG_BLK = 10
SM_DT, SM_A, SM_BETA = 0, SSD_HEADS, SSD_HEADS + GDN_HEADS
RT_G, RT_E = 0, SUBLANES
MOE_BUCKET_ROWS = 32

VMEM_LIMIT = 56 * 1024 * 1024


def _dot(a, b):
    return jnp.dot(a, b, preferred_element_type=F32)


def _dot_nt(a, b):
    return lax.dot_general(a, b, (((1,), (1,)), ((), ())), preferred_element_type=F32)


def _split(x):
    hi = x.astype(BF16)
    lo = (x - hi.astype(F32)).astype(BF16)
    return hi, lo


def _dot_split_lhs(x, w):
    hi, lo = _split(x)
    return _dot(hi, w) + _dot(lo, w)


def _dot_split_rhs(w, x):
    hi, lo = _split(x)
    return _dot(w, hi) + _dot(w, lo)


def _sigmoid(x):
    return 1.0 / (1.0 + jnp.exp(-x))


def _silu(x):
    return x * _sigmoid(x)


def _softplus(x):
    return jnp.maximum(x, 0.0) + jnp.log1p(jnp.exp(-jnp.abs(x)))


def _time_index(r):
    return (r & ~(TIME_BLOCK - 1)) | ((r >> 3) & (SUBLANES - 1)) | ((r & (SUBLANES - 1)) << 3)


def _causal_conv(cur, tail_ref, w, first):
    t, c = cur.shape

    @pl.when(first)
    def _():
        tail_ref[...] = jnp.zeros_like(tail_ref)

    sub = lax.broadcasted_iota(jnp.int32, (SUBLANES, c), 0)
    prev_tail = tail_ref[...]
    outs = []
    for i in range(t // TIME_BLOCK):
        blk = cur[i * TIME_BLOCK:(i + 1) * TIME_BLOCK, :]
        tail = blk[TIME_BLOCK - HALO:, :]
        wrapped = [
            jnp.where(sub == 0,
                      pltpu.roll(prev_tail[j * SUBLANES:(j + 1) * SUBLANES, :], 1, axis=0),
                      pltpu.roll(tail[j * SUBLANES:(j + 1) * SUBLANES, :], 1, axis=0))
            for j in range(CONV_K - 1)]
        ext = jnp.concatenate(wrapped + [blk], axis=0)
        acc = None
        for k in range(CONV_K):
            term = ext[k * SUBLANES:k * SUBLANES + TIME_BLOCK, :] * w[k:k + 1, :]
            acc = term if acc is None else acc + term
        outs.append(acc)
        prev_tail = tail
    tail_ref[...] = prev_tail
    return jnp.concatenate(outs, axis=0)


def _inproj_kernel(x_ref, lnw_ref, wa_ref, wb_ref, ws_ref, o_ref, os_ref, xn_ref, *, na):
    @pl.when(pl.program_id(1) == 0)
    def _():
        x = x_ref[...]
        xn = x * lax.rsqrt(jnp.mean(x * x, axis=-1, keepdims=True) + EPS) * lnw_ref[...]
        xn_ref[...] = xn.astype(BF16)
        os_ref[...] = _dot(xn_ref[...], ws_ref[...])

    from_a = pl.program_id(1) < na

    @pl.when(from_a)
    def _():
        o_ref[...] = _dot(xn_ref[...], wa_ref[...]).astype(o_ref.dtype)

    @pl.when(jnp.logical_not(from_a))
    def _():
        o_ref[...] = _dot(xn_ref[...], wb_ref[...]).astype(o_ref.dtype)


def _in_proj(x, lnw, w_a, w_b, w_small, layer, tm, tn):
    n, d = x.shape
    na = w_a.shape[2] // tn
    nb = w_b.shape[2] // tn
    zs_steps = SSD_D_INNER // tn
    zs_out = ZS_BLK * SSD_D_INNER // tn

    def out_block(j):
        qkv_end = na + 3 * GDN_DIM // tn
        return jnp.where(j < zs_steps, j + zs_out, jnp.where(j < qkv_end, j - zs_steps, j))

    return pl.pallas_call(
        functools.partial(_inproj_kernel, na=na),
        grid=(n // tm, na + nb),
        in_specs=[
            pl.BlockSpec((tm, d), lambda i, j: (i, 0)),
            pl.BlockSpec((1, d), lambda i, j: (0, 0)),
            pl.BlockSpec((pl.Squeezed(), d, tn), lambda i, j: (layer, 0, jnp.minimum(j, na - 1))),
            pl.BlockSpec((pl.Squeezed(), d, tn), lambda i, j: (layer, 0, jnp.maximum(j - na, 0))),
            pl.BlockSpec((pl.Squeezed(), d, LANES), lambda i, j: (layer, 0, 0)),
        ],
        out_specs=[
            pl.BlockSpec((tm, tn), lambda i, j: (i, out_block(j))),
            pl.BlockSpec((tm, LANES), lambda i, j: (i, 0)),
        ],
        out_shape=[jax.ShapeDtypeStruct((n, MAIN_COLS), BF16), jax.ShapeDtypeStruct((n, LANES), F32)],
        scratch_shapes=[pltpu.VMEM((tm, d), BF16)],
        compiler_params=pltpu.CompilerParams(
            dimension_semantics=("arbitrary", "arbitrary"), vmem_limit_bytes=VMEM_LIMIT),
        name="in_proj",
    )(x, lnw, w_a, w_b, w_small)


SSD_STEP_ROWS = 2 * SSD_CHUNK


def _ssd_kernel(xbc_ref, z_ref, sm_ref, convw_ref, convb_ref, dtb_ref, alog_ref, dexp_ref, normw_ref, e_ref,
                y_ref, halo_ref, state_ref, yacc_ref):
    q = SSD_CHUNK
    p = SSD_HEAD_DIM
    ns = SSD_STATE
    gw = (SSD_HEADS // SSD_GROUPS) * p
    first = pl.program_id(1) == 0

    @pl.when(first)
    def _():
        state_ref[...] = jnp.zeros_like(state_ref)

    conv = _causal_conv(xbc_ref[...].astype(F32), halo_ref, convw_ref[...], first) + convb_ref[...]
    xbc_all = _silu(conv)
    dt_all = _softplus(sm_ref[...] + dtb_ref[...])
    a_neg = -jnp.exp(alog_ref[...])
    row = lax.broadcasted_iota(jnp.int32, (q, q), 0)
    col = lax.broadcasted_iota(jnp.int32, (q, q), 1)
    causal = _time_index(row) >= _time_index(col)
    tri = jnp.where(causal, 1.0, 0.0).astype(BF16)
    lane = lax.broadcasted_iota(jnp.int32, (q, LANES), 1)

    for ci in range(xbc_ref.shape[0] // q):
        rs = slice(ci * q, (ci + 1) * q)
        xs = xbc_all[rs, :SSD_D_INNER]
        bm = xbc_all[rs, SSD_D_INNER:SSD_D_INNER + SSD_GROUPS * ns]
        cm = xbc_all[rs, SSD_D_INNER + SSD_GROUPS * ns:]
        dt = dt_all[rs, :]
        a_cs = _dot_split_rhs(tri, dt * a_neg)
        a_cs_t = a_cs.T
        e_acs = jnp.exp(a_cs)
        dec_end = jnp.exp(a_cs[q - 1:q, :] - a_cs)
        ex = _dot_split_lhs(jnp.concatenate([dt, dt * dec_end, e_acs], axis=0), e_ref[...])
        dt_x, dtdec_x, eacs_x = ex[:q], ex[q:2 * q], ex[2 * q:]

        xdt_b = (xs * dt_x).astype(BF16)
        xdd_b = (xs * dtdec_x).astype(BF16)
        for g in range(SSD_GROUPS):
            bg = bm[:, g * ns:(g + 1) * ns]
            cg_b = cm[:, g * ns:(g + 1) * ns].astype(BF16)
            scores = _dot_nt(cg_b, bg.astype(BF16))
            s_g = state_ref[:, g * gw:(g + 1) * gw]
            yacc_ref[rs, g * gw:(g + 1) * gw] = _dot(cg_b, s_g.astype(BF16)) * eacs_x[:, g * gw:(g + 1) * gw]
            new_g = _dot(bg.T.astype(BF16), xdd_b[:, g * gw:(g + 1) * gw])
            state_ref[:, g * gw:(g + 1) * gw] = s_g * eacs_x[q - 1:q, g * gw:(g + 1) * gw] + new_g
            for j in range(gw // LANES):
                h0 = (g * gw + j * LANES) // p
                c0 = g * gw + j * LANES
                ms = []
                for h in (h0, h0 + 1):
                    seg = a_cs[:, h:h + 1] - a_cs_t[h:h + 1, :]
                    ms.append((scores * jnp.exp(jnp.where(causal, seg, -jnp.inf))).astype(BF16))
                xp = xdt_b[:, c0:c0 + LANES]
                zero = jnp.zeros_like(xp)
                rhs = jnp.concatenate([jnp.where(lane < p, xp, zero), jnp.where(lane >= p, xp, zero)], axis=0)
                yacc_ref[rs, c0:c0 + LANES] += _dot(jnp.concatenate(ms, axis=1), rhs)

        y = yacc_ref[rs, :] + dexp_ref[...] * xs
        y = y * _silu(z_ref[rs, :].astype(F32))
        y = y * lax.rsqrt(jnp.mean(y * y, axis=-1, keepdims=True) + EPS) * normw_ref[...]
        y_ref[rs, :] = y.astype(BF16)


def _ssd(proj, small, convw, convb, dtb, alog, dexp, normw, expand, bsz, seqlen):
    q = SSD_STEP_ROWS
    nc = seqlen // q
    n = bsz * seqlen
    rowmap = lambda b, c: b * nc + c
    const = lambda b, c: (0, 0)
    return pl.pallas_call(
        _ssd_kernel,
        grid=(bsz, nc),
        in_specs=[
            pl.BlockSpec((q, SSD_XBC), lambda b, c: (rowmap(b, c), XBC_BLK)),
            pl.BlockSpec((q, SSD_D_INNER), lambda b, c: (rowmap(b, c), ZS_BLK)),
            pl.BlockSpec((q, LANES), lambda b, c: (rowmap(b, c), 0)),
            pl.BlockSpec((CONV_K, SSD_XBC), const),
            pl.BlockSpec((1, SSD_XBC), const),
            pl.BlockSpec((1, LANES), const),
            pl.BlockSpec((1, LANES), const),
            pl.BlockSpec((1, SSD_D_INNER), const),
            pl.BlockSpec((1, SSD_D_INNER), const),
            pl.BlockSpec((LANES, SSD_D_INNER), const),
        ],
        out_specs=pl.BlockSpec((q, SSD_D_INNER), lambda b, c: (rowmap(b, c), 0)),
        out_shape=jax.ShapeDtypeStruct((n, SSD_D_INNER), BF16),
        scratch_shapes=[
            pltpu.VMEM((HALO, SSD_XBC), F32),
            pltpu.VMEM((SSD_STATE, SSD_D_INNER), F32),
            pltpu.VMEM((q, SSD_D_INNER), F32),
        ],
        compiler_params=pltpu.CompilerParams(
            dimension_semantics=("arbitrary", "arbitrary"), vmem_limit_bytes=VMEM_LIMIT),
        name="ssd",
    )(proj, proj, small, convw, convb, dtb, alog, dexp, normw, expand)


GDN_PREP_ROWS = 4 * GDN_CHUNK
GDN_CHUNK_SHIFT = GDN_CHUNK.bit_length() - 1


def _unit_lower_inverse_many(ms, row, col):
    c = ms[0].shape[0]
    diff = row ^ col
    eye = jnp.where(row == col, 1.0, 0.0)
    ts = [eye - jnp.where((diff >> 1) == 0, m, 0.0) for m in ms]
    shift = 1
    while (1 << shift) < c:
        sel = (diff >> shift) == 1
        tbs = [t.astype(BF16) for t in ts]
        xs = [_dot(tb, jnp.where(sel, m, 0.0).astype(BF16)) for tb, m in zip(tbs, ms)]
        ts = [t - _dot(x.astype(BF16), tb) for t, x, tb in zip(ts, xs, tbs)]
        shift += 1
    return ts


def _gdn_prep_kernel(qkv_ref, sm_ref, convw_ref, dtb_ref, alog_ref, eg_ref, eb_ref,
                     u_ref, w_ref, qd_ref, aqk_ref, kdt_ref, gl_ref, halo_ref):
    c = GDN_CHUNK
    dh = GDN_HEAD_DIM
    t = GDN_PREP_ROWS
    ncb = t // c
    first = pl.program_id(1) == 0
    qkv = _silu(_causal_conv(qkv_ref[...].astype(F32), halo_ref, convw_ref[...], first))

    sm = sm_ref[...]
    lane1 = lax.broadcasted_iota(jnp.int32, (t, LANES), 1)
    is_a = (lane1 >= SM_A) & (lane1 < SM_A + GDN_HEADS)
    log_decay = jnp.where(is_a, -jnp.exp(alog_ref[...]) * _softplus(sm + dtb_ref[...]), 0.0)
    beta = _sigmoid(sm)
    row_t = lax.broadcasted_iota(jnp.int32, (t, t), 0)
    col_t = lax.broadcasted_iota(jnp.int32, (t, t), 1)
    same_chunk = (row_t >> GDN_CHUNK_SHIFT) == (col_t >> GDN_CHUNK_SHIFT)
    tri = jnp.where(same_chunk, jnp.where(_time_index(row_t) >= _time_index(col_t), 1.0, 0.0), 0.0).astype(BF16)
    gc = _dot_split_rhs(tri, log_decay)
    gc_t = gc.T
    g_last = jnp.concatenate(
        [jnp.broadcast_to(gc[(i + 1) * c - 1:(i + 1) * c, :], (c, LANES)) for i in range(ncb)], axis=0)
    ex = _dot_split_lhs(jnp.concatenate([jnp.exp(gc), jnp.exp(g_last - gc)], axis=0), eg_ref[...])
    egc_x, erev_x = ex[:t], ex[t:]
    beta_x = _dot_split_lhs(beta, eb_ref[...])

    row = _time_index(lax.broadcasted_iota(jnp.int32, (c, c), 0))
    col = _time_index(lax.broadcasted_iota(jnp.int32, (c, c), 1))
    incl = row >= col
    strict = row > col

    qs, ks, kbs, rhs = [], [], [], []
    for h in range(GDN_HEADS):
        hs = slice(h * dh, (h + 1) * dh)
        qh = qkv[:, h * dh:(h + 1) * dh]
        kh = qkv[:, GDN_DIM + h * dh:GDN_DIM + (h + 1) * dh]
        vh = qkv[:, 2 * GDN_DIM + h * dh:2 * GDN_DIM + (h + 1) * dh]
        qh = qh * lax.rsqrt(jnp.sum(qh * qh, axis=-1, keepdims=True) + EPS) * (dh ** -0.5)
        kh = kh * lax.rsqrt(jnp.sum(kh * kh, axis=-1, keepdims=True) + EPS)
        kb = kh * beta_x[:, hs]
        qs.append(qh)
        ks.append(kh)
        kbs.append(kb)
        rhs.append(jnp.concatenate([vh * beta_x[:, hs], kb * egc_x[:, hs]], axis=1).astype(BF16))

    items = [(i, h) for i in range(ncb) for h in range(GDN_HEADS)]
    ms, aqks = [], []
    for i, h in items:
        rs = slice(i * c, (i + 1) * c)
        la = SM_A + h
        decay = jnp.exp(jnp.where(incl, gc[rs, la:la + 1] - gc_t[la:la + 1, rs], -jnp.inf))
        nt = _dot_nt(jnp.concatenate([kbs[h][rs], qs[h][rs]], axis=0).astype(BF16), ks[h][rs].astype(BF16))
        ms.append(jnp.where(strict, nt[:c] * decay, 0.0))
        aqks.append((nt[c:] * decay).astype(BF16))
    t_invs = _unit_lower_inverse_many(ms, row, col)
    uws = [_dot(t_inv.astype(BF16), rhs[h][i * c:(i + 1) * c]) for (i, h), t_inv in zip(items, t_invs)]

    def by_chunk(pieces):
        return jnp.concatenate(
            [jnp.concatenate(pieces[i * GDN_HEADS:(i + 1) * GDN_HEADS], axis=1) for i in range(ncb)], axis=0)

    u_ref[...] = by_chunk([uw[:, :dh] for uw in uws]).astype(BF16)
    w_ref[...] = by_chunk([uw[:, dh:] for uw in uws]).astype(BF16)
    aqk_ref[...] = by_chunk(aqks)
    qd_ref[...] = jnp.concatenate(
        [qs[h] * egc_x[:, h * dh:(h + 1) * dh] for h in range(GDN_HEADS)], axis=1).astype(BF16)
    for i in range(ncb):
        rs = slice(i * c, (i + 1) * c)
        kdt_ref[i] = jnp.concatenate(
            [(ks[h][rs] * erev_x[rs, h * dh:(h + 1) * dh]).T for h in range(GDN_HEADS)], axis=1).astype(BF16)
        gl_ref[i] = egc_x[(i + 1) * c - 1:(i + 1) * c, :]


def _gdn_scan_kernel(u_ref, w_ref, qd_ref, aqk_ref, kdt_ref, gl_ref, zg_ref, normw_ref, o_ref, state_ref):
    c = GDN_CHUNK
    dh = GDN_HEAD_DIM

    @pl.when(pl.program_id(0) == 0)
    def _():
        state_ref[...] = jnp.zeros_like(state_ref)

    zero = jnp.zeros((c, dh), BF16)
    for b in range(u_ref.shape[0]):
        ss, ps = [], []
        for h in range(GDN_HEADS):
            hs = slice(h * dh, (h + 1) * dh)
            s = state_ref[b, h]
            ss.append(s)
            ps.append(_dot(jnp.concatenate([w_ref[b, :, hs], qd_ref[b, :, hs]], axis=0), s.astype(BF16)))
        v_news = [(u_ref[b, :, h * dh:(h + 1) * dh].astype(F32) - ps[h][:c]).astype(BF16)
                  for h in range(GDN_HEADS)]
        rs = []
        for pr in range(GDN_HEADS // 2):
            lhs = jnp.concatenate([aqk_ref[b, :, pr * dh:(pr + 1) * dh], kdt_ref[b, 0, :, pr * dh:(pr + 1) * dh]],
                                  axis=0)
            bd = jnp.concatenate([jnp.concatenate([v_news[2 * pr], zero], axis=1),
                                  jnp.concatenate([zero, v_news[2 * pr + 1]], axis=1)], axis=0)
            rs.append(_dot(lhs, bd))
        outs = []
        for h in range(GDN_HEADS):
            hs = slice(h * dh, (h + 1) * dh)
            r = rs[h // 2][:, (h % 2) * dh:(h % 2 + 1) * dh]
            state_ref[b, h] = ss[h] * gl_ref[b, 0, :, hs] + r[c:]
            o = ps[h][c:] + r[:c]
            o = o * lax.rsqrt(jnp.mean(o * o, axis=-1, keepdims=True) + EPS) * normw_ref[...]
            outs.append(o * _silu(zg_ref[b, :, hs].astype(F32)))
        o_ref[b] = jnp.concatenate(outs, axis=1).astype(BF16)


def _gdn(proj, small, convw, dtb, alog, normw, expand_a, expand_b, bsz, seqlen):
    c = GDN_CHUNK
    t = GDN_PREP_ROWS
    nc = seqlen // c
    nb = seqlen // t
    n = bsz * seqlen
    hh = GDN_HEADS * c
    rowmap = lambda b, i: b * nb + i
    const = lambda b, i: (0, 0)
    params = pltpu.CompilerParams(dimension_semantics=("arbitrary", "arbitrary"), vmem_limit_bytes=VMEM_LIMIT)
    u, w, qd, aqk, kdt, gl = pl.pallas_call(
        _gdn_prep_kernel,
        grid=(bsz, nb),
        in_specs=[
            pl.BlockSpec((t, 3 * GDN_DIM), lambda b, i: (rowmap(b, i), QKV_BLK)),
            pl.BlockSpec((t, LANES), lambda b, i: (rowmap(b, i), 0)),
            pl.BlockSpec((CONV_K, 3 * GDN_DIM), const),
            pl.BlockSpec((1, LANES), const),
            pl.BlockSpec((1, LANES), const),
            pl.BlockSpec((LANES, GDN_DIM), const),
            pl.BlockSpec((LANES, GDN_DIM), const),
        ],
        out_specs=[
            pl.BlockSpec((t, GDN_DIM), lambda b, i: (rowmap(b, i), 0)),
            pl.BlockSpec((t, GDN_DIM), lambda b, i: (rowmap(b, i), 0)),
            pl.BlockSpec((t, GDN_DIM), lambda b, i: (rowmap(b, i), 0)),
            pl.BlockSpec((t, hh), lambda b, i: (rowmap(b, i), 0)),
            pl.BlockSpec((t // c, GDN_HEAD_DIM, hh), lambda b, i: (rowmap(b, i), 0, 0)),
            pl.BlockSpec((t // c, 1, GDN_DIM), lambda b, i: (rowmap(b, i), 0, 0)),
        ],
        out_shape=[
            jax.ShapeDtypeStruct((n, GDN_DIM), BF16),
            jax.ShapeDtypeStruct((n, GDN_DIM), BF16),
            jax.ShapeDtypeStruct((n, GDN_DIM), BF16),
            jax.ShapeDtypeStruct((n, hh), BF16),
            jax.ShapeDtypeStruct((n // c, GDN_HEAD_DIM, hh), BF16),
            jax.ShapeDtypeStruct((n // c, 1, GDN_DIM), F32),
        ],
        scratch_shapes=[pltpu.VMEM((HALO, 3 * GDN_DIM), F32)],
        compiler_params=params,
        name="gdn_prep",
    )(proj, small, convw, dtb, alog, expand_a, expand_b)

    seq3 = lambda i: (0, i, 0)
    seq4 = lambda i: (0, i, 0, 0)
    o = pl.pallas_call(
        _gdn_scan_kernel,
        grid=(nc,),
        in_specs=[
            pl.BlockSpec((bsz, c, GDN_DIM), seq3),
            pl.BlockSpec((bsz, c, GDN_DIM), seq3),
            pl.BlockSpec((bsz, c, GDN_DIM), seq3),
            pl.BlockSpec((bsz, c, hh), seq3),
            pl.BlockSpec((bsz, 1, GDN_HEAD_DIM, hh), seq4),
            pl.BlockSpec((bsz, 1, 1, GDN_DIM), seq4),
            pl.BlockSpec((bsz, c, GDN_DIM), lambda i: (0, i, ZG_BLK)),
            pl.BlockSpec((1, GDN_HEAD_DIM), lambda i: (0, 0)),
        ],
        out_specs=pl.BlockSpec((bsz, c, GDN_DIM), seq3),
        out_shape=jax.ShapeDtypeStruct((bsz, seqlen, GDN_DIM), BF16),
        scratch_shapes=[pltpu.VMEM((bsz, GDN_HEADS, GDN_HEAD_DIM, GDN_HEAD_DIM), F32)],
        compiler_params=pltpu.CompilerParams(dimension_semantics=("arbitrary",), vmem_limit_bytes=VMEM_LIMIT),
        name="gdn_scan",
    )(u.reshape(bsz, seqlen, GDN_DIM), w.reshape(bsz, seqlen, GDN_DIM), qd.reshape(bsz, seqlen, GDN_DIM),
      aqk.reshape(bsz, seqlen, hh), kdt.reshape(bsz, nc, GDN_HEAD_DIM, hh), gl.reshape(bsz, nc, 1, GDN_DIM),
      proj.reshape(bsz, seqlen, MAIN_COLS), normw)
    return o.reshape(n, GDN_DIM)


def _merge_kernel(y_ref, o_ref, gate_ref, h_ref, bm_ref, wsp_ref, wgp_ref, wout_ref, out_ref):
    d = h_ref.shape[1]
    y_ssd = _dot(y_ref[...], wsp_ref[...])
    y_gdn = _dot(o_ref[...], wgp_ref[...])
    gates = _sigmoid(gate_ref[...].astype(F32) + bm_ref[...])
    merged = gates[:, :d] * y_ssd + gates[:, d:] * y_gdn
    out_ref[...] = h_ref[...] + _dot(merged.astype(BF16), wout_ref[...])


def _merge(y, o, proj, h, b_merge, w_sp, w_gp, w_out, layer, tm):
    n, d = h.shape
    const = lambda i: (0, 0)
    return pl.pallas_call(
        _merge_kernel,
        grid=(n // tm,),
        in_specs=[
            pl.BlockSpec((tm, SSD_D_INNER), lambda i: (i, 0)),
            pl.BlockSpec((tm, GDN_DIM), lambda i: (i, 0)),
            pl.BlockSpec((tm, 2 * d), lambda i: (i, GATE_BLK)),
            pl.BlockSpec((tm, d), lambda i: (i, 0)),
            pl.BlockSpec((1, 2 * d), const),
            pl.BlockSpec((pl.Squeezed(), SSD_D_INNER, d), lambda i: (layer, 0, 0)),
            pl.BlockSpec((pl.Squeezed(), GDN_DIM, d), lambda i: (layer, 0, 0)),
            pl.BlockSpec((pl.Squeezed(), d, d), lambda i: (layer, 0, 0)),
        ],
        out_specs=pl.BlockSpec((tm, d), lambda i: (i, 0)),
        out_shape=jax.ShapeDtypeStruct((n, d), F32),
        compiler_params=pltpu.CompilerParams(
            dimension_semantics=("arbitrary",), vmem_limit_bytes=VMEM_LIMIT),
        name="merge",
    )(y, o, proj, h, b_merge, w_sp, w_gp, w_out)


MOE_PAIRS = MOE_EPG * (MOE_EPG - 1) // 2
MOE_BUCKETS = MOE_GROUPS * MOE_PAIRS
MOE_TILE = 256
RT_BUCKET, RT_WLO, RT_WHI, RT_RANK = 0, 1, 2, 3


def _route(logits):
    t = logits.shape[1]
    neg = -jnp.inf
    big = jnp.int32(LANES)
    gl = logits[RT_G:RT_G + MOE_GROUPS, :]
    grow = lax.broadcasted_iota(jnp.int32, (MOE_GROUPS, t), 0)
    gmax = jnp.max(gl, axis=0, keepdims=True)
    gsel = jnp.min(jnp.where(gl == gmax, grow, big), axis=0, keepdims=True)
    group_w = 1.0 / jnp.sum(jnp.exp(gl - gmax), axis=0, keepdims=True)
    erow = lax.broadcasted_iota(jnp.int32, (MOE_EXPERTS, t), 0)
    el = jnp.where((erow >> (MOE_EPG.bit_length() - 1)) == gsel, logits[RT_E:RT_E + MOE_EXPERTS, :], neg)
    m1 = jnp.max(el, axis=0, keepdims=True)
    i1 = jnp.min(jnp.where(el == m1, erow, big), axis=0, keepdims=True)
    el2 = jnp.where(erow == i1, neg, el)
    m2 = jnp.max(el2, axis=0, keepdims=True)
    i2 = jnp.min(jnp.where(el2 == m2, erow, big), axis=0, keepdims=True)
    z = jnp.sum(jnp.exp(el - m1), axis=0, keepdims=True)
    p1 = 1.0 / z
    p2 = jnp.exp(m2 - m1) / z
    tot = p1 + p2
    w1 = group_w * (p1 / tot)
    w2 = group_w * (p2 / tot)
    l1 = i1 - gsel * MOE_EPG
    l2 = i2 - gsel * MOE_EPG
    lo = jnp.minimum(l1, l2)
    hi = jnp.maximum(l1, l2)
    pair = ((lo * (2 * MOE_EPG - 1 - lo)) >> 1) + (hi - lo - 1)
    bucket = gsel * MOE_PAIRS + pair
    first_lo = l1 < l2
    return bucket, jnp.where(first_lo, w1, w2), jnp.where(first_lo, w2, w1)


def _moe_rmsnorm(x, lnw):
    return x * lax.rsqrt(jnp.mean(x * x, axis=-1, keepdims=True) + EPS) * lnw


def _router_kernel(h_ref, lnw_ref, wr_ref, br_ref, info_ref, counts_ref, run_ref):
    t = h_ref.shape[0]

    @pl.when(pl.program_id(0) == 0)
    def _():
        run_ref[...] = jnp.zeros_like(run_ref)

    hi, lo = _split(_moe_rmsnorm(h_ref[...], lnw_ref[...]))
    whi, wlo = wr_ref[0], wr_ref[1]
    logits = _dot_nt(whi, hi) + _dot_nt(whi, lo) + _dot_nt(wlo, hi) + br_ref[:, 0:1]
    bucket, w_lo, w_hi = _route(logits)
    brow = lax.broadcasted_iota(jnp.int32, (MOE_BUCKET_ROWS, t), 0)
    onehot = jnp.where(brow == bucket, 1.0, 0.0)
    row = lax.broadcasted_iota(jnp.int32, (t, t), 0)
    col = lax.broadcasted_iota(jnp.int32, (t, t), 1)
    earlier = _dot(onehot.astype(BF16), jnp.where(row < col, 1.0, 0.0).astype(BF16))
    rank = jnp.sum(onehot * (earlier + run_ref[:, 0:1]), axis=0, keepdims=True)
    run_ref[...] += jnp.sum(onehot, axis=1, keepdims=True)
    record = jnp.concatenate(
        [bucket.astype(F32), w_lo, w_hi, rank, jnp.zeros((LANES - 4, t), F32)], axis=0)
    info_ref[...] = record.T
    counts_ref[...] = run_ref[...]


def _dispatch_kernel(pos_ref, h_ref, info_ref, xs_in_hbm, xs_hbm, buf_ref, sem):
    del xs_in_hbm
    tm, d = h_ref.shape
    i = pl.program_id(0)
    last = pl.num_programs(0) - 1
    base = i * tm
    slot = i % 2

    def wait_rows(s):
        pltpu.make_async_copy(buf_ref.at[s], xs_hbm.at[pl.ds(0, tm)], sem.at[s]).wait()

    @pl.when(i >= 2)
    def _():
        wait_rows(slot)

    buf_ref[slot, :, :d] = h_ref[...]
    buf_ref[slot, :, d:] = info_ref[...]

    def issue(r, carry):
        pltpu.make_async_copy(buf_ref.at[slot, pl.ds(r, 1)], xs_hbm.at[pl.ds(pos_ref[base + r], 1)],
                              sem.at[slot]).start()
        return carry

    lax.fori_loop(0, tm, issue, 0, unroll=8)

    @pl.when(i == last)
    def _():
        wait_rows(slot)

    @pl.when((i == last) & (i >= 1))
    def _():
        wait_rows(1 - slot)


def _experts_kernel(lo_ref, hi_ref, nt_ref, x_ref, lnw_ref, wg_lo, wu_lo, wd_lo, wg_hi, wu_hi, wd_hi, y_ref):
    live = pl.program_id(0) < nt_ref[0]
    d = y_ref.shape[1]

    @pl.when(live)
    def _():
        xn = _moe_rmsnorm(x_ref[:, :d], lnw_ref[...]).astype(BF16)
        wts = x_ref[:, d + RT_WLO:d + RT_WHI + 1]
        y = None
        for j, (wg, wu, wd) in enumerate(((wg_lo, wu_lo, wd_lo), (wg_hi, wu_hi, wd_hi))):
            hid = _silu(_dot(xn, wg[0])) * _dot(xn, wu[0]) * wts[:, j:j + 1]
            part = _dot(hid.astype(BF16), wd[0])
            y = part if y is None else y + part
        y_ref[...] = x_ref[:, :d] + y

    @pl.when(jnp.logical_not(live))
    def _():
        y_ref[...] = jnp.zeros_like(y_ref)


def _combine_kernel(pos_ref, y_hbm, fnw_ref, out_ref, buf_ref, sem, *, final_norm):
    tm = out_ref.shape[0]
    i = pl.program_id(0)
    slot = i % 2

    def issue_tile(tile, s):
        base = tile * tm

        def issue(r, carry):
            pltpu.make_async_copy(y_hbm.at[pl.ds(pos_ref[base + r], 1)], buf_ref.at[s, pl.ds(r, 1)],
                                  sem.at[s]).start()
            return carry

        lax.fori_loop(0, tm, issue, 0, unroll=8)

    @pl.when(i == 0)
    def _():
        issue_tile(0, 0)

    @pl.when(i + 1 < pl.num_programs(0))
    def _():
        issue_tile(i + 1, 1 - slot)

    pltpu.make_async_copy(y_hbm.at[pl.ds(0, tm)], buf_ref.at[slot], sem.at[slot]).wait()
    y = buf_ref[slot]
    if final_norm:
        y = _moe_rmsnorm(y, fnw_ref[...])
    out_ref[...] = y


def _moe(h, lnw, w_router, b_router, w_gate, w_up, w_down, fnw, layer, tm, final_norm, out_order, slots):
    n, d = h.shape
    dff = w_gate.shape[-1]
    tile = MOE_TILE
    n_tiles = n // tile + MOE_BUCKETS
    p = n_tiles * tile
    const = lambda i: (0, 0)
    info, counts = pl.pallas_call(
        _router_kernel,
        grid=(n // tm,),
        in_specs=[
            pl.BlockSpec((tm, d), lambda i: (i, 0)),
            pl.BlockSpec((1, d), const),
            pl.BlockSpec((2, LANES, d), lambda i: (0, 0, 0)),
            pl.BlockSpec((LANES, LANES), const),
        ],
        out_specs=[pl.BlockSpec((tm, LANES), lambda i: (i, 0)), pl.BlockSpec((MOE_BUCKET_ROWS, LANES), const)],
        out_shape=[jax.ShapeDtypeStruct((n, LANES), F32), jax.ShapeDtypeStruct((MOE_BUCKET_ROWS, LANES), F32)],
        scratch_shapes=[pltpu.VMEM((MOE_BUCKET_ROWS, LANES), F32)],
        compiler_params=pltpu.CompilerParams(dimension_semantics=("arbitrary",), vmem_limit_bytes=VMEM_LIMIT),
        name="moe_router",
    )(h, lnw, w_router, b_router)

    tiles_per = (counts[:MOE_BUCKETS, 0].astype(jnp.int32) + tile - 1) // tile
    tile_end = jnp.cumsum(tiles_per)
    first_slot = ((tile_end - tiles_per) * tile).astype(F32)
    bucket_is = info[:, RT_BUCKET:RT_BUCKET + 1] == jnp.arange(MOE_BUCKETS, dtype=F32)[None, :]
    pos = (jnp.sum(jnp.where(bucket_is, first_slot[None, :], 0.0), axis=1) + info[:, RT_RANK]).astype(jnp.int32)
    n_live = tile_end[-1]
    tile_bucket = jnp.searchsorted(tile_end, jnp.minimum(jnp.arange(n_tiles, dtype=jnp.int32), n_live - 1),
                                   side="right").astype(jnp.int32)
    pairs = [(a, b) for a in range(MOE_EPG) for b in range(a + 1, MOE_EPG)]
    grp = tile_bucket // MOE_PAIRS
    tile_lo = grp * MOE_EPG + jnp.array([a for a, _ in pairs], jnp.int32)[tile_bucket % MOE_PAIRS]
    tile_hi = grp * MOE_EPG + jnp.array([b for _, b in pairs], jnp.int32)[tile_bucket % MOE_PAIRS]

    x_sorted = pl.pallas_call(
        _dispatch_kernel,
        grid_spec=pltpu.PrefetchScalarGridSpec(
            num_scalar_prefetch=1, grid=(n // tile,),
            in_specs=[
                pl.BlockSpec((tile, d), lambda i, pos: (i, 0)),
                pl.BlockSpec((tile, LANES), lambda i, pos: (i, 0)),
                pl.BlockSpec(memory_space=pl.ANY),
            ],
            out_specs=pl.BlockSpec(memory_space=pl.ANY),
            scratch_shapes=[pltpu.VMEM((2, tile, d + LANES), F32), pltpu.SemaphoreType.DMA((2,))]),
        out_shape=jax.ShapeDtypeStruct((p, d + LANES), F32),
        input_output_aliases={3: 0},
        compiler_params=pltpu.CompilerParams(dimension_semantics=("arbitrary",), vmem_limit_bytes=VMEM_LIMIT),
        name="moe_dispatch",
    )(pos, h, info, jnp.zeros((p, d + LANES), F32) if slots is None else slots)

    lo_map = lambda t, lo, hi, nt: (layer, lo[t], 0, 0)
    hi_map = lambda t, lo, hi, nt: (layer, hi[t], 0, 0)
    y_sorted = pl.pallas_call(
        _experts_kernel,
        grid_spec=pltpu.PrefetchScalarGridSpec(
            num_scalar_prefetch=3, grid=(n_tiles,),
            in_specs=[
                pl.BlockSpec((tile, d + LANES), lambda t, lo, hi, nt: (t, 0)),
                pl.BlockSpec((1, d), lambda t, lo, hi, nt: (0, 0)),
                pl.BlockSpec((pl.Squeezed(), 1, d, dff), lo_map),
                pl.BlockSpec((pl.Squeezed(), 1, d, dff), lo_map),
                pl.BlockSpec((pl.Squeezed(), 1, dff, d), lo_map),
                pl.BlockSpec((pl.Squeezed(), 1, d, dff), hi_map),
                pl.BlockSpec((pl.Squeezed(), 1, d, dff), hi_map),
                pl.BlockSpec((pl.Squeezed(), 1, dff, d), hi_map),
            ],
            out_specs=pl.BlockSpec((tile, d), lambda t, lo, hi, nt: (t, 0))),
        out_shape=jax.ShapeDtypeStruct((p, d), F32),
        compiler_params=pltpu.CompilerParams(dimension_semantics=("arbitrary",), vmem_limit_bytes=VMEM_LIMIT),
        name="moe_experts",
    )(tile_lo, tile_hi, n_live.reshape(1).astype(jnp.int32), x_sorted, lnw,
      w_gate, w_up, w_down, w_gate, w_up, w_down)

    out = pl.pallas_call(
        functools.partial(_combine_kernel, final_norm=final_norm),
        grid_spec=pltpu.PrefetchScalarGridSpec(
            num_scalar_prefetch=1, grid=(n // tile,),
            in_specs=[
                pl.BlockSpec(memory_space=pl.ANY),
                pl.BlockSpec((1, d), lambda i, pos: (0, 0)),
            ],
            out_specs=pl.BlockSpec((tile, d), lambda i, pos: (i, 0)),
            scratch_shapes=[pltpu.VMEM((2, tile, d), F32), pltpu.SemaphoreType.DMA((2,))]),
        out_shape=jax.ShapeDtypeStruct((n, d), F32),
        compiler_params=pltpu.CompilerParams(dimension_semantics=("arbitrary",), vmem_limit_bytes=VMEM_LIMIT),
        name="moe_combine",
    )(pos if out_order is None else out_order(pos), y_sorted, fnw)
    return out, x_sorted


def _lane_row(v, offset):
    return jnp.zeros((1, LANES), F32).at[0, offset:offset + v.shape[0]].set(v.astype(F32))


def _expansion(offset, heads, width):
    r = jnp.arange(LANES)[:, None]
    c = jnp.arange(heads * width)[None, :]
    return (r == offset + c // width).astype(BF16)


def _row_tile(n, want):
    t = min(n, want)
    while n % t:
        t //= 2
    return t


def kernel(x, ln1_w, w_in, ssd_conv_w, ssd_conv_b, ssd_dt_bias, ssd_a_log, ssd_d, ssd_norm_w, w_ssd_proj,
           gdn_conv_w, gdn_dt_bias, gdn_a_log, gdn_norm_w, w_gdn_proj, b_merge, w_out, ln2_w,
           moe_w_group, moe_b_group, moe_w_expert, moe_b_expert, moe_w_gate, moe_w_up, moe_w_down,
           final_norm_w):
    bsz, seqlen, d = x.shape
    n = bsz * seqlen
    depth = w_in.shape[0]

    def swap_order(a):
        blocks = a.reshape((n // TIME_BLOCK, SUBLANES, SUBLANES) + a.shape[1:])
        return blocks.swapaxes(1, 2).reshape(a.shape)

    h = swap_order(x.reshape(n, d))

    o1 = SSD_D_INNER
    o2 = o1 + SSD_XBC
    o3 = o2 + SSD_HEADS
    o4 = o3 + 3 * GDN_DIM
    o5 = o4 + GDN_HEADS
    o6 = o5 + GDN_HEADS
    o7 = o6 + GDN_DIM

    e_ssd = _expansion(SM_DT, SSD_HEADS, SSD_HEAD_DIM)
    e_gdn_a = _expansion(SM_A, GDN_HEADS, GDN_HEAD_DIM)
    e_gdn_b = _expansion(SM_BETA, GDN_HEADS, GDN_HEAD_DIM)
    tm_proj = _row_tile(n, 1024)
    tm_merge = _row_tile(n, 512)
    tm_moe = _row_tile(n, 512)

    w_a = w_in[:, :, :o2].astype(BF16)
    w_b = jnp.concatenate([w_in[:, :, o3:o4], w_in[:, :, o7:], w_in[:, :, o6:o7]], axis=2).astype(BF16)
    w_small = jnp.concatenate(
        [w_in[:, :, o2:o3], w_in[:, :, o4:o5], w_in[:, :, o5:o6],
         jnp.zeros((depth, d, LANES - SSD_HEADS - 2 * GDN_HEADS), F32)], axis=2).astype(BF16)
    w_sp, w_gp, w_o = w_ssd_proj.astype(BF16), w_gdn_proj.astype(BF16), w_out.astype(BF16)
    w_gate, w_up, w_down = moe_w_gate.astype(BF16), moe_w_up.astype(BF16), moe_w_down.astype(BF16)

    slots = None
    for i in range(depth):
        proj, small = _in_proj(h, ln1_w[i][None, :], w_a, w_b, w_small, i, tm_proj, 1024)

        y = _ssd(proj, small, ssd_conv_w[i], ssd_conv_b[i][None, :],
                 _lane_row(ssd_dt_bias[i], SM_DT), _lane_row(ssd_a_log[i], SM_DT),
                 jnp.repeat(ssd_d[i], SSD_HEAD_DIM)[None, :], ssd_norm_w[i][None, :], e_ssd, bsz, seqlen)
        o = _gdn(proj, small, gdn_conv_w[i], _lane_row(gdn_dt_bias[i], SM_A), _lane_row(gdn_a_log[i], SM_A),
                 gdn_norm_w[i][None, :], e_gdn_a, e_gdn_b, bsz, seqlen)
        h = _merge(y, o, proj, h, b_merge[i][None, :], w_sp, w_gp, w_o, i, tm_merge)

        gap = jnp.zeros((d, RT_E - RT_G - MOE_GROUPS), F32)
        rest = jnp.zeros((d, LANES - RT_E - MOE_EXPERTS), F32)
        w_r = jnp.concatenate([moe_w_group[i], gap, moe_w_expert[i], rest], axis=1).T
        w_r_hi = w_r.astype(BF16)
        w_r_lo = (w_r - w_r_hi.astype(F32)).astype(BF16)
        b_r = jnp.broadcast_to(
            jnp.concatenate([moe_b_group[i], gap[0], moe_b_expert[i], rest[0]])[:, None], (LANES, LANES))
        h, slots = _moe(h, ln2_w[i][None, :], jnp.stack([w_r_hi, w_r_lo]), b_r, w_gate, w_up, w_down,
                        final_norm_w[None, :], i, tm_moe, final_norm=(i == depth - 1),
                        out_order=swap_order if i == depth - 1 else None, slots=slots)
    return h.reshape(bsz, seqlen, d)
```

```python
import functools

import jax
import jax.numpy as jnp
from jax import lax
from jax.experimental import pallas as pl
from jax.experimental.pallas import tpu as pltpu

F32 = jnp.float32
BF16 = jnp.bfloat16

EPS = 1e-6
CONV_K = 4
LANES = 128
SUBLANES = 8
TIME_BLOCK = SUBLANES * SUBLANES
HALO = (CONV_K - 1) * SUBLANES

D_MODEL = 1024
SSD_D_INNER = 2 * D_MODEL
SSD_HEAD_DIM = 64
SSD_HEADS = SSD_D_INNER // SSD_HEAD_DIM
SSD_GROUPS = 4
SSD_STATE = 128
SSD_CHUNK = 128
SSD_XBC = SSD_D_INNER + 2 * SSD_GROUPS * SSD_STATE
GDN_HEADS = 8
GDN_HEAD_DIM = 128
GDN_DIM = GDN_HEADS * GDN_HEAD_DIM
GDN_CHUNK = 64
MOE_GROUPS = 4
MOE_EPG = 4
MOE_EXPERTS = MOE_GROUPS * MOE_EPG
MOE_D_FF = 512

MAIN_COLS = SSD_XBC + 3 * GDN_DIM + SSD_D_INNER + 2 * D_MODEL + GDN_DIM
XBC_BLK, QKV_BLK = 0, 1
ZS_BLK, GATE_BLK = 3, 4
ZG_BLK = 10
SM_DT, SM_A, SM_BETA = 0, SSD_HEADS, SSD_HEADS + GDN_HEADS
RT_G, RT_E = 0, SUBLANES
MOE_BUCKET_ROWS = 32

VMEM_LIMIT = 56 * 1024 * 1024


def _dot(a, b):
    return jnp.dot(a, b, preferred_element_type=F32)


def _dot_nt(a, b):
    return lax.dot_general(a, b, (((1,), (1,)), ((), ())), preferred_element_type=F32)


def _split(x):
    hi = x.astype(BF16)
    lo = (x - hi.astype(F32)).astype(BF16)
    return hi, lo


def _dot_split_lhs(x, w):
    hi, lo = _split(x)
    return _dot(hi, w) + _dot(lo, w)


def _dot_split_rhs(w, x):
    hi, lo = _split(x)
    return _dot(w, hi) + _dot(w, lo)


def _sigmoid(x):
    return 1.0 / (1.0 + jnp.exp(-x))


def _silu(x):
    return x * _sigmoid(x)


def _softplus(x):
    return jnp.maximum(x, 0.0) + jnp.log1p(jnp.exp(-jnp.abs(x)))


def _time_index(r):
    return (r & ~(TIME_BLOCK - 1)) | ((r >> 3) & (SUBLANES - 1)) | ((r & (SUBLANES - 1)) << 3)


def _causal_conv(cur, tail_ref, w, first):
    t, c = cur.shape

    @pl.when(first)
    def _():
        tail_ref[...] = jnp.zeros_like(tail_ref)

    sub = lax.broadcasted_iota(jnp.int32, (SUBLANES, c), 0)
    prev_tail = tail_ref[...]
    outs = []
    for i in range(t // TIME_BLOCK):
        blk = cur[i * TIME_BLOCK:(i + 1) * TIME_BLOCK, :]
        tail = blk[TIME_BLOCK - HALO:, :]
        wrapped = [
            jnp.where(sub == 0,
                      pltpu.roll(prev_tail[j * SUBLANES:(j + 1) * SUBLANES, :], 1, axis=0),
                      pltpu.roll(tail[j * SUBLANES:(j + 1) * SUBLANES, :], 1, axis=0))
            for j in range(CONV_K - 1)]
        ext = jnp.concatenate(wrapped + [blk], axis=0)
        acc = None
        for k in range(CONV_K):
            term = ext[k * SUBLANES:k * SUBLANES + TIME_BLOCK, :] * w[k:k + 1, :]
            acc = term if acc is None else acc + term
        outs.append(acc)
        prev_tail = tail
    tail_ref[...] = prev_tail
    return jnp.concatenate(outs, axis=0)


def _inproj_kernel(x_ref, lnw_ref, w_ref, ws_ref, o_ref, os_ref, xn_ref):
    @pl.when(pl.program_id(1) == 0)
    def _():
        x = x_ref[...]
        xn = x * lax.rsqrt(jnp.mean(x * x, axis=-1, keepdims=True) + EPS) * lnw_ref[...]
        xn_ref[...] = xn.astype(BF16)
        os_ref[...] = _dot(xn_ref[...], ws_ref[...])

    o_ref[...] = _dot(xn_ref[...], w_ref[...]).astype(o_ref.dtype)


def _in_proj(x, lnw, w_main, w_small, layer, tm, tn):
    n, d = x.shape
    nm = w_main.shape[2]
    return pl.pallas_call(
        _inproj_kernel,
        grid=(n // tm, nm // tn),
        in_specs=[
            pl.BlockSpec((tm, d), lambda i, j: (i, 0)),
            pl.BlockSpec((1, d), lambda i, j: (0, 0)),
            pl.BlockSpec((pl.Squeezed(), d, tn), lambda i, j: (layer, 0, j)),
            pl.BlockSpec((pl.Squeezed(), d, LANES), lambda i, j: (layer, 0, 0)),
        ],
        out_specs=[
            pl.BlockSpec((tm, tn), lambda i, j: (i, j)),
            pl.BlockSpec((tm, LANES), lambda i, j: (i, 0)),
        ],
        out_shape=[jax.ShapeDtypeStruct((n, nm), BF16), jax.ShapeDtypeStruct((n, LANES), F32)],
        scratch_shapes=[pltpu.VMEM((tm, d), BF16)],
        compiler_params=pltpu.CompilerParams(
            dimension_semantics=("arbitrary", "arbitrary"), vmem_limit_bytes=VMEM_LIMIT),
        name="in_proj",
    )(x, lnw, w_main, w_small)


SSD_STEP_ROWS = 2 * SSD_CHUNK


def _ssd_kernel(xbc_ref, z_ref, sm_ref, convw_ref, convb_ref, dtb_ref, alog_ref, dexp_ref, normw_ref, e_ref,
                y_ref, halo_ref, state_ref, yacc_ref):
    q = SSD_CHUNK
    p = SSD_HEAD_DIM
    ns = SSD_STATE
    gw = (SSD_HEADS // SSD_GROUPS) * p
    first = pl.program_id(1) == 0

    @pl.when(first)
    def _():
        state_ref[...] = jnp.zeros_like(state_ref)

    conv = _causal_conv(xbc_ref[...].astype(F32), halo_ref, convw_ref[...], first) + convb_ref[...]
    xbc_all = _silu(conv)
    dt_all = _softplus(sm_ref[...] + dtb_ref[...])
    a_neg = -jnp.exp(alog_ref[...])
    row = lax.broadcasted_iota(jnp.int32, (q, q), 0)
    col = lax.broadcasted_iota(jnp.int32, (q, q), 1)
    causal = _time_index(row) >= _time_index(col)
    tri = jnp.where(causal, 1.0, 0.0).astype(BF16)
    lane = lax.broadcasted_iota(jnp.int32, (q, LANES), 1)

    for ci in range(xbc_ref.shape[0] // q):
        rs = slice(ci * q, (ci + 1) * q)
        xs = xbc_all[rs, :SSD_D_INNER]
        bm = xbc_all[rs, SSD_D_INNER:SSD_D_INNER + SSD_GROUPS * ns]
        cm = xbc_all[rs, SSD_D_INNER + SSD_GROUPS * ns:]
        dt = dt_all[rs, :]
        a_cs = _dot_split_rhs(tri, dt * a_neg)
        a_cs_t = a_cs.T
        e_acs = jnp.exp(a_cs)
        dec_end = jnp.exp(a_cs[q - 1:q, :] - a_cs)
        ex = _dot_split_lhs(jnp.concatenate([dt, dt * dec_end, e_acs], axis=0), e_ref[...])
        dt_x, dtdec_x, eacs_x = ex[:q], ex[q:2 * q], ex[2 * q:]

        xdt_b = (xs * dt_x).astype(BF16)
        xdd_b = (xs * dtdec_x).astype(BF16)
        for g in range(SSD_GROUPS):
            bg = bm[:, g * ns:(g + 1) * ns]
            cg_b = cm[:, g * ns:(g + 1) * ns].astype(BF16)
            scores = _dot_nt(cg_b, bg.astype(BF16))
            s_g = state_ref[:, g * gw:(g + 1) * gw]
            yacc_ref[rs, g * gw:(g + 1) * gw] = _dot(cg_b, s_g.astype(BF16)) * eacs_x[:, g * gw:(g + 1) * gw]
            new_g = _dot(bg.T.astype(BF16), xdd_b[:, g * gw:(g + 1) * gw])
            state_ref[:, g * gw:(g + 1) * gw] = s_g * eacs_x[q - 1:q, g * gw:(g + 1) * gw] + new_g
            for j in range(gw // LANES):
                h0 = (g * gw + j * LANES) // p
                c0 = g * gw + j * LANES
                ms = []
                for h in (h0, h0 + 1):
                    seg = a_cs[:, h:h + 1] - a_cs_t[h:h + 1, :]
                    ms.append((scores * jnp.exp(jnp.where(causal, seg, -jnp.inf))).astype(BF16))
                xp = xdt_b[:, c0:c0 + LANES]
                zero = jnp.zeros_like(xp)
                rhs = jnp.concatenate([jnp.where(lane < p, xp, zero), jnp.where(lane >= p, xp, zero)], axis=0)
                yacc_ref[rs, c0:c0 + LANES] += _dot(jnp.concatenate(ms, axis=1), rhs)

        y = yacc_ref[rs, :] + dexp_ref[...] * xs
        y = y * _silu(z_ref[rs, :].astype(F32))
        y = y * lax.rsqrt(jnp.mean(y * y, axis=-1, keepdims=True) + EPS) * normw_ref[...]
        y_ref[rs, :] = y.astype(BF16)


def _ssd(proj, small, convw, convb, dtb, alog, dexp, normw, expand, bsz, seqlen):
    q = SSD_STEP_ROWS
    nc = seqlen // q
    n = bsz * seqlen
    rowmap = lambda b, c: b * nc + c
    const = lambda b, c: (0, 0)
    return pl.pallas_call(
        _ssd_kernel,
        grid=(bsz, nc),
        in_specs=[
            pl.BlockSpec((q, SSD_XBC), lambda b, c: (rowmap(b, c), XBC_BLK)),
            pl.BlockSpec((q, SSD_D_INNER), lambda b, c: (rowmap(b, c), ZS_BLK)),
            pl.BlockSpec((q, LANES), lambda b, c: (rowmap(b, c), 0)),
            pl.BlockSpec((CONV_K, SSD_XBC), const),
            pl.BlockSpec((1, SSD_XBC), const),
            pl.BlockSpec((1, LANES), const),
            pl.BlockSpec((1, LANES), const),
            pl.BlockSpec((1, SSD_D_INNER), const),
            pl.BlockSpec((1, SSD_D_INNER), const),
            pl.BlockSpec((LANES, SSD_D_INNER), const),
        ],
        out_specs=pl.BlockSpec((q, SSD_D_INNER), lambda b, c: (rowmap(b, c), 0)),
        out_shape=jax.ShapeDtypeStruct((n, SSD_D_INNER), BF16),
        scratch_shapes=[
            pltpu.VMEM((HALO, SSD_XBC), F32),
            pltpu.VMEM((SSD_STATE, SSD_D_INNER), F32),
            pltpu.VMEM((q, SSD_D_INNER), F32),
        ],
        compiler_params=pltpu.CompilerParams(
            dimension_semantics=("arbitrary", "arbitrary"), vmem_limit_bytes=VMEM_LIMIT),
        name="ssd",
    )(proj, proj, small, convw, convb, dtb, alog, dexp, normw, expand)


GDN_PREP_ROWS = 4 * GDN_CHUNK
GDN_CHUNK_SHIFT = GDN_CHUNK.bit_length() - 1


def _unit_lower_inverse_many(ms, row, col):
    c = ms[0].shape[0]
    diff = row ^ col
    eye = jnp.where(row == col, 1.0, 0.0)
    ts = [eye - jnp.where((diff >> 1) == 0, m, 0.0) for m in ms]
    shift = 1
    while (1 << shift) < c:
        sel = (diff >> shift) == 1
        tbs = [t.astype(BF16) for t in ts]
        xs = [_dot(tb, jnp.where(sel, m, 0.0).astype(BF16)) for tb, m in zip(tbs, ms)]
        ts = [t - _dot(x.astype(BF16), tb) for t, x, tb in zip(ts, xs, tbs)]
        shift += 1
    return ts


def _gdn_prep_kernel(qkv_ref, sm_ref, convw_ref, dtb_ref, alog_ref, eg_ref, eb_ref,
                     u_ref, w_ref, qd_ref, aqk_ref, kdt_ref, gl_ref, halo_ref):
    c = GDN_CHUNK
    dh = GDN_HEAD_DIM
    t = GDN_PREP_ROWS
    ncb = t // c
    first = pl.program_id(1) == 0
    qkv = _silu(_causal_conv(qkv_ref[...].astype(F32), halo_ref, convw_ref[...], first))

    sm = sm_ref[...]
    lane1 = lax.broadcasted_iota(jnp.int32, (t, LANES), 1)
    is_a = (lane1 >= SM_A) & (lane1 < SM_A + GDN_HEADS)
    log_decay = jnp.where(is_a, -jnp.exp(alog_ref[...]) * _softplus(sm + dtb_ref[...]), 0.0)
    beta = _sigmoid(sm)
    row_t = lax.broadcasted_iota(jnp.int32, (t, t), 0)
    col_t = lax.broadcasted_iota(jnp.int32, (t, t), 1)
    same_chunk = (row_t >> GDN_CHUNK_SHIFT) == (col_t >> GDN_CHUNK_SHIFT)
    tri = jnp.where(same_chunk, jnp.where(_time_index(row_t) >= _time_index(col_t), 1.0, 0.0), 0.0).astype(BF16)
    gc = _dot_split_rhs(tri, log_decay)
    gc_t = gc.T
    g_last = jnp.concatenate(
        [jnp.broadcast_to(gc[(i + 1) * c - 1:(i + 1) * c, :], (c, LANES)) for i in range(ncb)], axis=0)
    ex = _dot_split_lhs(jnp.concatenate([jnp.exp(gc), jnp.exp(g_last - gc)], axis=0), eg_ref[...])
    egc_x, erev_x = ex[:t], ex[t:]
    beta_x = _dot_split_lhs(beta, eb_ref[...])

    row = _time_index(lax.broadcasted_iota(jnp.int32, (c, c), 0))
    col = _time_index(lax.broadcasted_iota(jnp.int32, (c, c), 1))
    incl = row >= col
    strict = row > col

    qs, ks, kbs, rhs = [], [], [], []
    for h in range(GDN_HEADS):
        hs = slice(h * dh, (h + 1) * dh)
        qh = qkv[:, h * dh:(h + 1) * dh]
        kh = qkv[:, GDN_DIM + h * dh:GDN_DIM + (h + 1) * dh]
        vh = qkv[:, 2 * GDN_DIM + h * dh:2 * GDN_DIM + (h + 1) * dh]
        qh = qh * lax.rsqrt(jnp.sum(qh * qh, axis=-1, keepdims=True) + EPS) * (dh ** -0.5)
        kh = kh * lax.rsqrt(jnp.sum(kh * kh, axis=-1, keepdims=True) + EPS)
        kb = kh * beta_x[:, hs]
        qs.append(qh)
        ks.append(kh)
        kbs.append(kb)
        rhs.append(jnp.concatenate([vh * beta_x[:, hs], kb * egc_x[:, hs]], axis=1).astype(BF16))

    items = [(i, h) for i in range(ncb) for h in range(GDN_HEADS)]
    ms, aqks = [], []
    for i, h in items:
        rs = slice(i * c, (i + 1) * c)
        la = SM_A + h
        decay = jnp.exp(jnp.where(incl, gc[rs, la:la + 1] - gc_t[la:la + 1, rs], -jnp.inf))
        nt = _dot_nt(jnp.concatenate([kbs[h][rs], qs[h][rs]], axis=0).astype(BF16), ks[h][rs].astype(BF16))
        ms.append(jnp.where(strict, nt[:c] * decay, 0.0))
        aqks.append((nt[c:] * decay).astype(BF16))
    t_invs = _unit_lower_inverse_many(ms, row, col)
    uws = [_dot(t_inv.astype(BF16), rhs[h][i * c:(i + 1) * c]) for (i, h), t_inv in zip(items, t_invs)]

    def by_chunk(pieces):
        return jnp.concatenate(
            [jnp.concatenate(pieces[i * GDN_HEADS:(i + 1) * GDN_HEADS], axis=1) for i in range(ncb)], axis=0)

    u_ref[...] = by_chunk([uw[:, :dh] for uw in uws]).astype(BF16)
    w_ref[...] = by_chunk([uw[:, dh:] for uw in uws]).astype(BF16)
    aqk_ref[...] = by_chunk(aqks)
    qd_ref[...] = jnp.concatenate(
        [qs[h] * egc_x[:, h * dh:(h + 1) * dh] for h in range(GDN_HEADS)], axis=1).astype(BF16)
    for i in range(ncb):
        rs = slice(i * c, (i + 1) * c)
        kdt_ref[i] = jnp.concatenate(
            [(ks[h][rs] * erev_x[rs, h * dh:(h + 1) * dh]).T for h in range(GDN_HEADS)], axis=1).astype(BF16)
        gl_ref[i] = egc_x[(i + 1) * c - 1:(i + 1) * c, :]


def _gdn_scan_kernel(u_ref, w_ref, qd_ref, aqk_ref, kdt_ref, gl_ref, zg_ref, normw_ref, o_ref, state_ref):
    c = GDN_CHUNK
    dh = GDN_HEAD_DIM

    @pl.when(pl.program_id(0) == 0)
    def _():
        state_ref[...] = jnp.zeros_like(state_ref)

    zero = jnp.zeros((c, dh), BF16)
    for b in range(u_ref.shape[0]):
        ss, ps = [], []
        for h in range(GDN_HEADS):
            hs = slice(h * dh, (h + 1) * dh)
            s = state_ref[b, h]
            ss.append(s)
            ps.append(_dot(jnp.concatenate([w_ref[b, :, hs], qd_ref[b, :, hs]], axis=0), s.astype(BF16)))
        v_news = [(u_ref[b, :, h * dh:(h + 1) * dh].astype(F32) - ps[h][:c]).astype(BF16)
                  for h in range(GDN_HEADS)]
        rs = []
        for pr in range(GDN_HEADS // 2):
            lhs = jnp.concatenate([aqk_ref[b, :, pr * dh:(pr + 1) * dh], kdt_ref[b, 0, :, pr * dh:(pr + 1) * dh]],
                                  axis=0)
            bd = jnp.concatenate([jnp.concatenate([v_news[2 * pr], zero], axis=1),
                                  jnp.concatenate([zero, v_news[2 * pr + 1]], axis=1)], axis=0)
            rs.append(_dot(lhs, bd))
        outs = []
        for h in range(GDN_HEADS):
            hs = slice(h * dh, (h + 1) * dh)
            r = rs[h // 2][:, (h % 2) * dh:(h % 2 + 1) * dh]
            state_ref[b, h] = ss[h] * gl_ref[b, 0, :, hs] + r[c:]
            o = ps[h][c:] + r[:c]
            o = o * lax.rsqrt(jnp.mean(o * o, axis=-1, keepdims=True) + EPS) * normw_ref[...]
            outs.append(o * _silu(zg_ref[b, :, hs].astype(F32)))
        o_ref[b] = jnp.concatenate(outs, axis=1).astype(BF16)


def _gdn(proj, small, convw, dtb, alog, normw, expand_a, expand_b, bsz, seqlen):
    c = GDN_CHUNK
    t = GDN_PREP_ROWS
    nc = seqlen // c
    nb = seqlen // t
    n = bsz * seqlen
    hh = GDN_HEADS * c
    rowmap = lambda b, i: b * nb + i
    const = lambda b, i: (0, 0)
    params = pltpu.CompilerParams(dimension_semantics=("arbitrary", "arbitrary"), vmem_limit_bytes=VMEM_LIMIT)
    u, w, qd, aqk, kdt, gl = pl.pallas_call(
        _gdn_prep_kernel,
        grid=(bsz, nb),
        in_specs=[
            pl.BlockSpec((t, 3 * GDN_DIM), lambda b, i: (rowmap(b, i), QKV_BLK)),
            pl.BlockSpec((t, LANES), lambda b, i: (rowmap(b, i), 0)),
            pl.BlockSpec((CONV_K, 3 * GDN_DIM), const),
            pl.BlockSpec((1, LANES), const),
            pl.BlockSpec((1, LANES), const),
            pl.BlockSpec((LANES, GDN_DIM), const),
            pl.BlockSpec((LANES, GDN_DIM), const),
        ],
        out_specs=[
            pl.BlockSpec((t, GDN_DIM), lambda b, i: (rowmap(b, i), 0)),
            pl.BlockSpec((t, GDN_DIM), lambda b, i: (rowmap(b, i), 0)),
            pl.BlockSpec((t, GDN_DIM), lambda b, i: (rowmap(b, i), 0)),
            pl.BlockSpec((t, hh), lambda b, i: (rowmap(b, i), 0)),
            pl.BlockSpec((t // c, GDN_HEAD_DIM, hh), lambda b, i: (rowmap(b, i), 0, 0)),
            pl.BlockSpec((t // c, 1, GDN_DIM), lambda b, i: (rowmap(b, i), 0, 0)),
        ],
        out_shape=[
            jax.ShapeDtypeStruct((n, GDN_DIM), BF16),
            jax.ShapeDtypeStruct((n, GDN_DIM), BF16),
            jax.ShapeDtypeStruct((n, GDN_DIM), BF16),
            jax.ShapeDtypeStruct((n, hh), BF16),
            jax.ShapeDtypeStruct((n // c, GDN_HEAD_DIM, hh), BF16),
            jax.ShapeDtypeStruct((n // c, 1, GDN_DIM), F32),
        ],
        scratch_shapes=[pltpu.VMEM((HALO, 3 * GDN_DIM), F32)],
        compiler_params=params,
        name="gdn_prep",
    )(proj, small, convw, dtb, alog, expand_a, expand_b)

    seq3 = lambda i: (0, i, 0)
    seq4 = lambda i: (0, i, 0, 0)
    o = pl.pallas_call(
        _gdn_scan_kernel,
        grid=(nc,),
        in_specs=[
            pl.BlockSpec((bsz, c, GDN_DIM), seq3),
            pl.BlockSpec((bsz, c, GDN_DIM), seq3),
            pl.BlockSpec((bsz, c, GDN_DIM), seq3),
            pl.BlockSpec((bsz, c, hh), seq3),
            pl.BlockSpec((bsz, 1, GDN_HEAD_DIM, hh), seq4),
            pl.BlockSpec((bsz, 1, 1, GDN_DIM), seq4),
            pl.BlockSpec((bsz, c, GDN_DIM), lambda i: (0, i, ZG_BLK)),
            pl.BlockSpec((1, GDN_HEAD_DIM), lambda i: (0, 0)),
        ],
        out_specs=pl.BlockSpec((bsz, c, GDN_DIM), seq3),
        out_shape=jax.ShapeDtypeStruct((bsz, seqlen, GDN_DIM), BF16),
        scratch_shapes=[pltpu.VMEM((bsz, GDN_HEADS, GDN_HEAD_DIM, GDN_HEAD_DIM), F32)],
        compiler_params=pltpu.CompilerParams(dimension_semantics=("arbitrary",), vmem_limit_bytes=VMEM_LIMIT),
        name="gdn_scan",
    )(u.reshape(bsz, seqlen, GDN_DIM), w.reshape(bsz, seqlen, GDN_DIM), qd.reshape(bsz, seqlen, GDN_DIM),
      aqk.reshape(bsz, seqlen, hh), kdt.reshape(bsz, nc, GDN_HEAD_DIM, hh), gl.reshape(bsz, nc, 1, GDN_DIM),
      proj.reshape(bsz, seqlen, MAIN_COLS), normw)
    return o.reshape(n, GDN_DIM)


def _merge_kernel(y_ref, o_ref, gate_ref, h_ref, bm_ref, wsp_ref, wgp_ref, wout_ref, out_ref):
    d = h_ref.shape[1]
    y_ssd = _dot(y_ref[...], wsp_ref[...])
    y_gdn = _dot(o_ref[...], wgp_ref[...])
    gates = _sigmoid(gate_ref[...].astype(F32) + bm_ref[...])
    merged = gates[:, :d] * y_ssd + gates[:, d:] * y_gdn
    out_ref[...] = h_ref[...] + _dot(merged.astype(BF16), wout_ref[...])


def _merge(y, o, proj, h, b_merge, w_sp, w_gp, w_out, layer, tm):
    n, d = h.shape
    const = lambda i: (0, 0)
    return pl.pallas_call(
        _merge_kernel,
        grid=(n // tm,),
        in_specs=[
            pl.BlockSpec((tm, SSD_D_INNER), lambda i: (i, 0)),
            pl.BlockSpec((tm, GDN_DIM), lambda i: (i, 0)),
            pl.BlockSpec((tm, 2 * d), lambda i: (i, GATE_BLK)),
            pl.BlockSpec((tm, d), lambda i: (i, 0)),
            pl.BlockSpec((1, 2 * d), const),
            pl.BlockSpec((pl.Squeezed(), SSD_D_INNER, d), lambda i: (layer, 0, 0)),
            pl.BlockSpec((pl.Squeezed(), GDN_DIM, d), lambda i: (layer, 0, 0)),
            pl.BlockSpec((pl.Squeezed(), d, d), lambda i: (layer, 0, 0)),
        ],
        out_specs=pl.BlockSpec((tm, d), lambda i: (i, 0)),
        out_shape=jax.ShapeDtypeStruct((n, d), F32),
        compiler_params=pltpu.CompilerParams(
            dimension_semantics=("arbitrary",), vmem_limit_bytes=VMEM_LIMIT),
        name="merge",
    )(y, o, proj, h, b_merge, w_sp, w_gp, w_out)


MOE_PAIRS = MOE_EPG * (MOE_EPG - 1) // 2
MOE_BUCKETS = MOE_GROUPS * MOE_PAIRS
MOE_TILE = 256
RT_BUCKET, RT_WLO, RT_WHI, RT_RANK = 0, 1, 2, 3


def _route(logits):
    t = logits.shape[1]
    neg = -jnp.inf
    big = jnp.int32(LANES)
    gl = logits[RT_G:RT_G + MOE_GROUPS, :]
    grow = lax.broadcasted_iota(jnp.int32, (MOE_GROUPS, t), 0)
    gmax = jnp.max(gl, axis=0, keepdims=True)
    gsel = jnp.min(jnp.where(gl == gmax, grow, big), axis=0, keepdims=True)
    group_w = 1.0 / jnp.sum(jnp.exp(gl - gmax), axis=0, keepdims=True)
    erow = lax.broadcasted_iota(jnp.int32, (MOE_EXPERTS, t), 0)
    el = jnp.where((erow >> (MOE_EPG.bit_length() - 1)) == gsel, logits[RT_E:RT_E + MOE_EXPERTS, :], neg)
    m1 = jnp.max(el, axis=0, keepdims=True)
    i1 = jnp.min(jnp.where(el == m1, erow, big), axis=0, keepdims=True)
    el2 = jnp.where(erow == i1, neg, el)
    m2 = jnp.max(el2, axis=0, keepdims=True)
    i2 = jnp.min(jnp.where(el2 == m2, erow, big), axis=0, keepdims=True)
    z = jnp.sum(jnp.exp(el - m1), axis=0, keepdims=True)
    p1 = 1.0 / z
    p2 = jnp.exp(m2 - m1) / z
    tot = p1 + p2
    w1 = group_w * (p1 / tot)
    w2 = group_w * (p2 / tot)
    l1 = i1 - gsel * MOE_EPG
    l2 = i2 - gsel * MOE_EPG
    lo = jnp.minimum(l1, l2)
    hi = jnp.maximum(l1, l2)
    pair = ((lo * (2 * MOE_EPG - 1 - lo)) >> 1) + (hi - lo - 1)
    bucket = gsel * MOE_PAIRS + pair
    first_lo = l1 < l2
    return bucket, jnp.where(first_lo, w1, w2), jnp.where(first_lo, w2, w1)


def _moe_rmsnorm(x, lnw):
    return x * lax.rsqrt(jnp.mean(x * x, axis=-1, keepdims=True) + EPS) * lnw


def _router_kernel(h_ref, lnw_ref, wr_ref, br_ref, info_ref, counts_ref, run_ref):
    t = h_ref.shape[0]

    @pl.when(pl.program_id(0) == 0)
    def _():
        run_ref[...] = jnp.zeros_like(run_ref)

    hi, lo = _split(_moe_rmsnorm(h_ref[...], lnw_ref[...]))
    whi, wlo = wr_ref[0], wr_ref[1]
    logits = _dot_nt(whi, hi) + _dot_nt(whi, lo) + _dot_nt(wlo, hi) + br_ref[:, 0:1]
    bucket, w_lo, w_hi = _route(logits)
    brow = lax.broadcasted_iota(jnp.int32, (MOE_BUCKET_ROWS, t), 0)
    onehot = jnp.where(brow == bucket, 1.0, 0.0)
    row = lax.broadcasted_iota(jnp.int32, (t, t), 0)
    col = lax.broadcasted_iota(jnp.int32, (t, t), 1)
    earlier = _dot(onehot.astype(BF16), jnp.where(row < col, 1.0, 0.0).astype(BF16))
    rank = jnp.sum(onehot * (earlier + run_ref[:, 0:1]), axis=0, keepdims=True)
    run_ref[...] += jnp.sum(onehot, axis=1, keepdims=True)
    record = jnp.concatenate(
        [bucket.astype(F32), w_lo, w_hi, rank, jnp.zeros((LANES - 4, t), F32)], axis=0)
    info_ref[...] = record.T
    counts_ref[...] = run_ref[...]


def _dispatch_kernel(pos_ref, h_ref, info_ref, xs_in_hbm, xs_hbm, buf_ref, sem):
    del xs_in_hbm
    tm, d = h_ref.shape
    i = pl.program_id(0)
    last = pl.num_programs(0) - 1
    base = i * tm
    slot = i % 2

    def wait_rows(s):
        pltpu.make_async_copy(buf_ref.at[s], xs_hbm.at[pl.ds(0, tm)], sem.at[s]).wait()

    @pl.when(i >= 2)
    def _():
        wait_rows(slot)

    buf_ref[slot, :, :d] = h_ref[...]
    buf_ref[slot, :, d:] = info_ref[...]

    def issue(r, carry):
        pltpu.make_async_copy(buf_ref.at[slot, pl.ds(r, 1)], xs_hbm.at[pl.ds(pos_ref[base + r], 1)],
                              sem.at[slot]).start()
        return carry

    lax.fori_loop(0, tm, issue, 0, unroll=8)

    @pl.when(i == last)
    def _():
        wait_rows(slot)

    @pl.when((i == last) & (i >= 1))
    def _():
        wait_rows(1 - slot)


def _experts_kernel(lo_ref, hi_ref, nt_ref, x_ref, lnw_ref, wg_lo, wu_lo, wd_lo, wg_hi, wu_hi, wd_hi, y_ref):
    live = pl.program_id(0) < nt_ref[0]
    d = y_ref.shape[1]

    @pl.when(live)
    def _():
        xn = _moe_rmsnorm(x_ref[:, :d], lnw_ref[...]).astype(BF16)
        wts = x_ref[:, d + RT_WLO:d + RT_WHI + 1]
        y = None
        for j, (wg, wu, wd) in enumerate(((wg_lo, wu_lo, wd_lo), (wg_hi, wu_hi, wd_hi))):
            hid = _silu(_dot(xn, wg[0])) * _dot(xn, wu[0]) * wts[:, j:j + 1]
            part = _dot(hid.astype(BF16), wd[0])
            y = part if y is None else y + part
        y_ref[...] = x_ref[:, :d] + y

    @pl.when(jnp.logical_not(live))
    def _():
        y_ref[...] = jnp.zeros_like(y_ref)


def _combine_kernel(pos_ref, y_hbm, fnw_ref, out_ref, buf_ref, sem, *, final_norm):
    tm = out_ref.shape[0]
    i = pl.program_id(0)
    slot = i % 2

    def issue_tile(tile, s):
        base = tile * tm

        def issue(r, carry):
            pltpu.make_async_copy(y_hbm.at[pl.ds(pos_ref[base + r], 1)], buf_ref.at[s, pl.ds(r, 1)],
                                  sem.at[s]).start()
            return carry

        lax.fori_loop(0, tm, issue, 0, unroll=8)

    @pl.when(i == 0)
    def _():
        issue_tile(0, 0)

    @pl.when(i + 1 < pl.num_programs(0))
    def _():
        issue_tile(i + 1, 1 - slot)

    pltpu.make_async_copy(y_hbm.at[pl.ds(0, tm)], buf_ref.at[slot], sem.at[slot]).wait()
    y = buf_ref[slot]
    if final_norm:
        y = _moe_rmsnorm(y, fnw_ref[...])
    out_ref[...] = y


def _moe(h, lnw, w_router, b_router, w_gate, w_up, w_down, fnw, layer, tm, final_norm, out_order, slots):
    n, d = h.shape
    dff = w_gate.shape[-1]
    tile = MOE_TILE
    n_tiles = n // tile + MOE_BUCKETS
    p = n_tiles * tile
    const = lambda i: (0, 0)
    info, counts = pl.pallas_call(
        _router_kernel,
        grid=(n // tm,),
        in_specs=[
            pl.BlockSpec((tm, d), lambda i: (i, 0)),
            pl.BlockSpec((1, d), const),
            pl.BlockSpec((2, LANES, d), lambda i: (0, 0, 0)),
            pl.BlockSpec((LANES, LANES), const),
        ],
        out_specs=[pl.BlockSpec((tm, LANES), lambda i: (i, 0)), pl.BlockSpec((MOE_BUCKET_ROWS, LANES), const)],
        out_shape=[jax.ShapeDtypeStruct((n, LANES), F32), jax.ShapeDtypeStruct((MOE_BUCKET_ROWS, LANES), F32)],
        scratch_shapes=[pltpu.VMEM((MOE_BUCKET_ROWS, LANES), F32)],
        compiler_params=pltpu.CompilerParams(dimension_semantics=("arbitrary",), vmem_limit_bytes=VMEM_LIMIT),
        name="moe_router",
    )(h, lnw, w_router, b_router)

    tiles_per = (counts[:MOE_BUCKETS, 0].astype(jnp.int32) + tile - 1) // tile
    tile_end = jnp.cumsum(tiles_per)
    first_slot = ((tile_end - tiles_per) * tile).astype(F32)
    bucket_is = info[:, RT_BUCKET:RT_BUCKET + 1] == jnp.arange(MOE_BUCKETS, dtype=F32)[None, :]
    pos = (jnp.sum(jnp.where(bucket_is, first_slot[None, :], 0.0), axis=1) + info[:, RT_RANK]).astype(jnp.int32)
    n_live = tile_end[-1]
    tile_bucket = jnp.searchsorted(tile_end, jnp.minimum(jnp.arange(n_tiles, dtype=jnp.int32), n_live - 1),
                                   side="right").astype(jnp.int32)
    pairs = [(a, b) for a in range(MOE_EPG) for b in range(a + 1, MOE_EPG)]
    grp = tile_bucket // MOE_PAIRS
    tile_lo = grp * MOE_EPG + jnp.array([a for a, _ in pairs], jnp.int32)[tile_bucket % MOE_PAIRS]
    tile_hi = grp * MOE_EPG + jnp.array([b for _, b in pairs], jnp.int32)[tile_bucket % MOE_PAIRS]

    x_sorted = pl.pallas_call(
        _dispatch_kernel,
        grid_spec=pltpu.PrefetchScalarGridSpec(
            num_scalar_prefetch=1, grid=(n // tile,),
            in_specs=[
                pl.BlockSpec((tile, d), lambda i, pos: (i, 0)),
                pl.BlockSpec((tile, LANES), lambda i, pos: (i, 0)),
                pl.BlockSpec(memory_space=pl.ANY),
            ],
            out_specs=pl.BlockSpec(memory_space=pl.ANY),
            scratch_shapes=[pltpu.VMEM((2, tile, d + LANES), F32), pltpu.SemaphoreType.DMA((2,))]),
        out_shape=jax.ShapeDtypeStruct((p, d + LANES), F32),
        input_output_aliases={3: 0},
        compiler_params=pltpu.CompilerParams(dimension_semantics=("arbitrary",), vmem_limit_bytes=VMEM_LIMIT),
        name="moe_dispatch",
    )(pos, h, info, jnp.zeros((p, d + LANES), F32) if slots is None else slots)

    lo_map = lambda t, lo, hi, nt: (layer, lo[t], 0, 0)
    hi_map = lambda t, lo, hi, nt: (layer, hi[t], 0, 0)
    y_sorted = pl.pallas_call(
        _experts_kernel,
        grid_spec=pltpu.PrefetchScalarGridSpec(
            num_scalar_prefetch=3, grid=(n_tiles,),
            in_specs=[
                pl.BlockSpec((tile, d + LANES), lambda t, lo, hi, nt: (t, 0)),
                pl.BlockSpec((1, d), lambda t, lo, hi, nt: (0, 0)),
                pl.BlockSpec((pl.Squeezed(), 1, d, dff), lo_map),
                pl.BlockSpec((pl.Squeezed(), 1, d, dff), lo_map),
                pl.BlockSpec((pl.Squeezed(), 1, dff, d), lo_map),
                pl.BlockSpec((pl.Squeezed(), 1, d, dff), hi_map),
                pl.BlockSpec((pl.Squeezed(), 1, d, dff), hi_map),
                pl.BlockSpec((pl.Squeezed(), 1, dff, d), hi_map),
            ],
            out_specs=pl.BlockSpec((tile, d), lambda t, lo, hi, nt: (t, 0))),
        out_shape=jax.ShapeDtypeStruct((p, d), F32),
        compiler_params=pltpu.CompilerParams(dimension_semantics=("arbitrary",), vmem_limit_bytes=VMEM_LIMIT),
        name="moe_experts",
    )(tile_lo, tile_hi, n_live.reshape(1).astype(jnp.int32), x_sorted, lnw,
      w_gate, w_up, w_down, w_gate, w_up, w_down)

    out = pl.pallas_call(
        functools.partial(_combine_kernel, final_norm=final_norm),
        grid_spec=pltpu.PrefetchScalarGridSpec(
            num_scalar_prefetch=1, grid=(n // tile,),
            in_specs=[
                pl.BlockSpec(memory_space=pl.ANY),
                pl.BlockSpec((1, d), lambda i, pos: (0, 0)),
            ],
            out_specs=pl.BlockSpec((tile, d), lambda i, pos: (i, 0)),
            scratch_shapes=[pltpu.VMEM((2, tile, d), F32), pltpu.SemaphoreType.DMA((2,))]),
        out_shape=jax.ShapeDtypeStruct((n, d), F32),
        compiler_params=pltpu.CompilerParams(dimension_semantics=("arbitrary",), vmem_limit_bytes=VMEM_LIMIT),
        name="moe_combine",
    )(pos if out_order is None else out_order(pos), y_sorted, fnw)
    return out, x_sorted


def _lane_row(v, offset):
    return jnp.zeros((1, LANES), F32).at[0, offset:offset + v.shape[0]].set(v.astype(F32))


def _expansion(offset, heads, width):
    r = jnp.arange(LANES)[:, None]
    c = jnp.arange(heads * width)[None, :]
    return (r == offset + c // width).astype(BF16)


def _row_tile(n, want):
    t = min(n, want)
    while n % t:
        t //= 2
    return t


def kernel(x, ln1_w, w_in, ssd_conv_w, ssd_conv_b, ssd_dt_bias, ssd_a_log, ssd_d, ssd_norm_w, w_ssd_proj,
           gdn_conv_w, gdn_dt_bias, gdn_a_log, gdn_norm_w, w_gdn_proj, b_merge, w_out, ln2_w,
           moe_w_group, moe_b_group, moe_w_expert, moe_b_expert, moe_w_gate, moe_w_up, moe_w_down,
           final_norm_w):
    bsz, seqlen, d = x.shape
    n = bsz * seqlen
    depth = w_in.shape[0]

    def swap_order(a):
        blocks = a.reshape((n // TIME_BLOCK, SUBLANES, SUBLANES) + a.shape[1:])
        return blocks.swapaxes(1, 2).reshape(a.shape)

    h = swap_order(x.reshape(n, d))

    o1 = SSD_D_INNER
    o2 = o1 + SSD_XBC
    o3 = o2 + SSD_HEADS
    o4 = o3 + 3 * GDN_DIM
    o5 = o4 + GDN_HEADS
    o6 = o5 + GDN_HEADS
    o7 = o6 + GDN_DIM

    e_ssd = _expansion(SM_DT, SSD_HEADS, SSD_HEAD_DIM)
    e_gdn_a = _expansion(SM_A, GDN_HEADS, GDN_HEAD_DIM)
    e_gdn_b = _expansion(SM_BETA, GDN_HEADS, GDN_HEAD_DIM)
    tm_proj = _row_tile(n, 1024)
    tm_merge = _row_tile(n, 512)
    tm_moe = _row_tile(n, 512)

    w_main = jnp.concatenate(
        [w_in[:, :, o1:o2], w_in[:, :, o3:o4], w_in[:, :, :o1], w_in[:, :, o7:], w_in[:, :, o6:o7]],
        axis=2).astype(BF16)
    w_small = jnp.concatenate(
        [w_in[:, :, o2:o3], w_in[:, :, o4:o5], w_in[:, :, o5:o6],
         jnp.zeros((depth, d, LANES - SSD_HEADS - 2 * GDN_HEADS), F32)], axis=2).astype(BF16)
    w_sp, w_gp, w_o = w_ssd_proj.astype(BF16), w_gdn_proj.astype(BF16), w_out.astype(BF16)
    w_gate, w_up, w_down = moe_w_gate.astype(BF16), moe_w_up.astype(BF16), moe_w_down.astype(BF16)

    slots = None
    for i in range(depth):
        proj, small = _in_proj(h, ln1_w[i][None, :], w_main, w_small, i, tm_proj, MAIN_COLS // 4)

        y = _ssd(proj, small, ssd_conv_w[i], ssd_conv_b[i][None, :],
                 _lane_row(ssd_dt_bias[i], SM_DT), _lane_row(ssd_a_log[i], SM_DT),
                 jnp.repeat(ssd_d[i], SSD_HEAD_DIM)[None, :], ssd_norm_w[i][None, :], e_ssd, bsz, seqlen)
        o = _gdn(proj, small, gdn_conv_w[i], _lane_row(gdn_dt_bias[i], SM_A), _lane_row(gdn_a_log[i], SM_A),
                 gdn_norm_w[i][None, :], e_gdn_a, e_gdn_b, bsz, seqlen)
        h = _merge(y, o, proj, h, b_merge[i][None, :], w_sp, w_gp, w_o, i, tm_merge)

        gap = jnp.zeros((d, RT_E - RT_G - MOE_GROUPS), F32)
        rest = jnp.zeros((d, LANES - RT_E - MOE_EXPERTS), F32)
        w_r = jnp.concatenate([moe_w_group[i], gap, moe_w_expert[i], rest], axis=1).T
        w_r_hi = w_r.astype(BF16)
        w_r_lo = (w_r - w_r_hi.astype(F32)).astype(BF16)
        b_r = jnp.broadcast_to(
            jnp.concatenate([moe_b_group[i], gap[0], moe_b_expert[i], rest[0]])[:, None], (LANES, LANES))
        h, slots = _moe(h, ln2_w[i][None, :], jnp.stack([w_r_hi, w_r_lo]), b_r, w_gate, w_up, w_down,
                        final_norm_w[None, :], i, tm_moe, final_norm=(i == depth - 1),
                        out_order=swap_order if i == depth - 1 else None, slots=slots)
    return h.reshape(bsz, seqlen, d)
```

```python
import functools

import jax
import jax.numpy as jnp
from jax import lax
from jax.experimental import pallas as pl
from jax.experimental.pallas import tpu as pltpu

F32 = jnp.float32
BF16 = jnp.bfloat16

EPS = 1e-6
CONV_K = 4
LANES = 128
SUBLANES = 8
TIME_BLOCK = SUBLANES * SUBLANES
HALO = (CONV_K - 1) * SUBLANES

D_MODEL = 1024
SSD_D_INNER = 2 * D_MODEL
SSD_HEAD_DIM = 64
SSD_HEADS = SSD_D_INNER // SSD_HEAD_DIM
SSD_GROUPS = 4
SSD_STATE = 128
SSD_CHUNK = 128
SSD_XBC = SSD_D_INNER + 2 * SSD_GROUPS * SSD_STATE
GDN_HEADS = 8
GDN_HEAD_DIM = 128
GDN_DIM = GDN_HEADS * GDN_HEAD_DIM
GDN_CHUNK = 64
MOE_GROUPS = 4
MOE_EPG = 4
MOE_EXPERTS = MOE_GROUPS * MOE_EPG
MOE_D_FF = 512

MAIN_COLS = SSD_XBC + 3 * GDN_DIM + SSD_D_INNER + 2 * D_MODEL + GDN_DIM
XBC_BLK, QKV_BLK = 0, 1
ZS_BLK, GATE_BLK = 3, 4
ZG_BLK = 10
SM_DT, SM_A, SM_BETA = 0, SSD_HEADS, SSD_HEADS + GDN_HEADS
RT_G, RT_E = 0, SUBLANES
MOE_BUCKET_ROWS = 32

VMEM_LIMIT = 56 * 1024 * 1024


def _dot(a, b):
    return jnp.dot(a, b, preferred_element_type=F32)


def _dot_nt(a, b):
    return lax.dot_general(a, b, (((1,), (1,)), ((), ())), preferred_element_type=F32)


def _split(x):
    hi = x.astype(BF16)
    lo = (x - hi.astype(F32)).astype(BF16)
    return hi, lo


def _dot_split_lhs(x, w):
    hi, lo = _split(x)
    return _dot(hi, w) + _dot(lo, w)


def _dot_split_rhs(w, x):
    hi, lo = _split(x)
    return _dot(w, hi) + _dot(w, lo)


def _sigmoid(x):
    return 1.0 / (1.0 + jnp.exp(-x))


def _silu(x):
    return x * _sigmoid(x)


def _softplus(x):
    return jnp.maximum(x, 0.0) + jnp.log1p(jnp.exp(-jnp.abs(x)))


def _time_index(r):
    return (r & ~(TIME_BLOCK - 1)) | ((r >> 3) & (SUBLANES - 1)) | ((r & (SUBLANES - 1)) << 3)


def _causal_conv(cur, tail_ref, w, first):
    t, c = cur.shape

    @pl.when(first)
    def _():
        tail_ref[...] = jnp.zeros_like(tail_ref)

    sub = lax.broadcasted_iota(jnp.int32, (SUBLANES, c), 0)
    prev_tail = tail_ref[...]
    outs = []
    for i in range(t // TIME_BLOCK):
        blk = cur[i * TIME_BLOCK:(i + 1) * TIME_BLOCK, :]
        tail = blk[TIME_BLOCK - HALO:, :]
        wrapped = [
            jnp.where(sub == 0,
                      pltpu.roll(prev_tail[j * SUBLANES:(j + 1) * SUBLANES, :], 1, axis=0),
                      pltpu.roll(tail[j * SUBLANES:(j + 1) * SUBLANES, :], 1, axis=0))
            for j in range(CONV_K - 1)]
        ext = jnp.concatenate(wrapped + [blk], axis=0)
        acc = None
        for k in range(CONV_K):
            term = ext[k * SUBLANES:k * SUBLANES + TIME_BLOCK, :] * w[k:k + 1, :]
            acc = term if acc is None else acc + term
        outs.append(acc)
        prev_tail = tail
    tail_ref[...] = prev_tail
    return jnp.concatenate(outs, axis=0)


def _inproj_kernel(x_ref, lnw_ref, w_ref, ws_ref, o_ref, os_ref, xn_ref):
    @pl.when(pl.program_id(1) == 0)
    def _():
        x = x_ref[...]
        xn = x * lax.rsqrt(jnp.mean(x * x, axis=-1, keepdims=True) + EPS) * lnw_ref[...]
        xn_ref[...] = xn.astype(BF16)
        os_ref[...] = _dot(xn_ref[...], ws_ref[...])

    o_ref[...] = _dot(xn_ref[...], w_ref[...]).astype(o_ref.dtype)


def _in_proj(x, lnw, w_main, w_small, layer, tm, tn):
    n, d = x.shape
    nm = w_main.shape[2]
    return pl.pallas_call(
        _inproj_kernel,
        grid=(n // tm, nm // tn),
        in_specs=[
            pl.BlockSpec((tm, d), lambda i, j: (i, 0)),
            pl.BlockSpec((1, d), lambda i, j: (0, 0)),
            pl.BlockSpec((pl.Squeezed(), d, tn), lambda i, j: (layer, 0, j)),
            pl.BlockSpec((pl.Squeezed(), d, LANES), lambda i, j: (layer, 0, 0)),
        ],
        out_specs=[
            pl.BlockSpec((tm, tn), lambda i, j: (i, j)),
            pl.BlockSpec((tm, LANES), lambda i, j: (i, 0)),
        ],
        out_shape=[jax.ShapeDtypeStruct((n, nm), BF16), jax.ShapeDtypeStruct((n, LANES), F32)],
        scratch_shapes=[pltpu.VMEM((tm, d), BF16)],
        compiler_params=pltpu.CompilerParams(
            dimension_semantics=("arbitrary", "arbitrary"), vmem_limit_bytes=VMEM_LIMIT),
        name="in_proj",
    )(x, lnw, w_main, w_small)


SSD_STEP_ROWS = 2 * SSD_CHUNK


def _ssd_kernel(xbc_ref, z_ref, sm_ref, convw_ref, convb_ref, dtb_ref, alog_ref, dexp_ref, normw_ref, e_ref,
                y_ref, halo_ref, state_ref, yacc_ref):
    q = SSD_CHUNK
    p = SSD_HEAD_DIM
    ns = SSD_STATE
    gw = (SSD_HEADS // SSD_GROUPS) * p
    first = pl.program_id(1) == 0

    @pl.when(first)
    def _():
        state_ref[...] = jnp.zeros_like(state_ref)

    conv = _causal_conv(xbc_ref[...].astype(F32), halo_ref, convw_ref[...], first) + convb_ref[...]
    xbc_all = _silu(conv)
    dt_all = _softplus(sm_ref[...] + dtb_ref[...])
    a_neg = -jnp.exp(alog_ref[...])
    row = lax.broadcasted_iota(jnp.int32, (q, q), 0)
    col = lax.broadcasted_iota(jnp.int32, (q, q), 1)
    causal = _time_index(row) >= _time_index(col)
    tri = jnp.where(causal, 1.0, 0.0).astype(BF16)
    lane = lax.broadcasted_iota(jnp.int32, (q, LANES), 1)

    for ci in range(xbc_ref.shape[0] // q):
        rs = slice(ci * q, (ci + 1) * q)
        xs = xbc_all[rs, :SSD_D_INNER]
        bm = xbc_all[rs, SSD_D_INNER:SSD_D_INNER + SSD_GROUPS * ns]
        cm = xbc_all[rs, SSD_D_INNER + SSD_GROUPS * ns:]
        dt = dt_all[rs, :]
        a_cs = _dot_split_rhs(tri, dt * a_neg)
        a_cs_t = a_cs.T
        e_acs = jnp.exp(a_cs)
        dec_end = jnp.exp(a_cs[q - 1:q, :] - a_cs)
        ex = _dot_split_lhs(jnp.concatenate([dt, dt * dec_end, e_acs], axis=0), e_ref[...])
        dt_x, dtdec_x, eacs_x = ex[:q], ex[q:2 * q], ex[2 * q:]

        xdt_b = (xs * dt_x).astype(BF16)
        xdd_b = (xs * dtdec_x).astype(BF16)
        for g in range(SSD_GROUPS):
            bg = bm[:, g * ns:(g + 1) * ns]
            cg_b = cm[:, g * ns:(g + 1) * ns].astype(BF16)
            scores = _dot_nt(cg_b, bg.astype(BF16))
            s_g = state_ref[:, g * gw:(g + 1) * gw]
            yacc_ref[rs, g * gw:(g + 1) * gw] = _dot(cg_b, s_g.astype(BF16)) * eacs_x[:, g * gw:(g + 1) * gw]
            new_g = _dot(bg.T.astype(BF16), xdd_b[:, g * gw:(g + 1) * gw])
            state_ref[:, g * gw:(g + 1) * gw] = s_g * eacs_x[q - 1:q, g * gw:(g + 1) * gw] + new_g
            for j in range(gw // LANES):
                h0 = (g * gw + j * LANES) // p
                c0 = g * gw + j * LANES
                ms = []
                for h in (h0, h0 + 1):
                    seg = a_cs[:, h:h + 1] - a_cs_t[h:h + 1, :]
                    ms.append((scores * jnp.exp(jnp.where(causal, seg, -jnp.inf))).astype(BF16))
                xp = xdt_b[:, c0:c0 + LANES]
                zero = jnp.zeros_like(xp)
                rhs = jnp.concatenate([jnp.where(lane < p, xp, zero), jnp.where(lane >= p, xp, zero)], axis=0)
                yacc_ref[rs, c0:c0 + LANES] += _dot(jnp.concatenate(ms, axis=1), rhs)

        y = yacc_ref[rs, :] + dexp_ref[...] * xs
        y = y * _silu(z_ref[rs, :].astype(F32))
        y = y * lax.rsqrt(jnp.mean(y * y, axis=-1, keepdims=True) + EPS) * normw_ref[...]
        y_ref[rs, :] = y.astype(BF16)


def _ssd(proj, small, convw, convb, dtb, alog, dexp, normw, expand, bsz, seqlen):
    q = SSD_STEP_ROWS
    nc = seqlen // q
    n = bsz * seqlen
    rowmap = lambda b, c: b * nc + c
    const = lambda b, c: (0, 0)
    return pl.pallas_call(
        _ssd_kernel,
        grid=(bsz, nc),
        in_specs=[
            pl.BlockSpec((q, SSD_XBC), lambda b, c: (rowmap(b, c), XBC_BLK)),
            pl.BlockSpec((q, SSD_D_INNER), lambda b, c: (rowmap(b, c), ZS_BLK)),
            pl.BlockSpec((q, LANES), lambda b, c: (rowmap(b, c), 0)),
            pl.BlockSpec((CONV_K, SSD_XBC), const),
            pl.BlockSpec((1, SSD_XBC), const),
            pl.BlockSpec((1, LANES), const),
            pl.BlockSpec((1, LANES), const),
            pl.BlockSpec((1, SSD_D_INNER), const),
            pl.BlockSpec((1, SSD_D_INNER), const),
            pl.BlockSpec((LANES, SSD_D_INNER), const),
        ],
        out_specs=pl.BlockSpec((q, SSD_D_INNER), lambda b, c: (rowmap(b, c), 0)),
        out_shape=jax.ShapeDtypeStruct((n, SSD_D_INNER), BF16),
        scratch_shapes=[
            pltpu.VMEM((HALO, SSD_XBC), F32),
            pltpu.VMEM((SSD_STATE, SSD_D_INNER), F32),
            pltpu.VMEM((q, SSD_D_INNER), F32),
        ],
        compiler_params=pltpu.CompilerParams(
            dimension_semantics=("arbitrary", "arbitrary"), vmem_limit_bytes=VMEM_LIMIT),
        name="ssd",
    )(proj, proj, small, convw, convb, dtb, alog, dexp, normw, expand)


GDN_PREP_ROWS = 4 * GDN_CHUNK
GDN_CHUNK_SHIFT = GDN_CHUNK.bit_length() - 1


def _unit_lower_inverse_many(ms, row, col):
    c = ms[0].shape[0]
    diff = row ^ col
    eye = jnp.where(row == col, 1.0, 0.0)
    ts = [eye - jnp.where((diff >> 1) == 0, m, 0.0) for m in ms]
    shift = 1
    while (1 << shift) < c:
        sel = (diff >> shift) == 1
        tbs = [t.astype(BF16) for t in ts]
        xs = [_dot(tb, jnp.where(sel, m, 0.0).astype(BF16)) for tb, m in zip(tbs, ms)]
        ts = [t - _dot(x.astype(BF16), tb) for t, x, tb in zip(ts, xs, tbs)]
        shift += 1
    return ts


def _gdn_prep_kernel(qkv_ref, sm_ref, convw_ref, dtb_ref, alog_ref, eg_ref, eb_ref,
                     u_ref, w_ref, qd_ref, aqk_ref, kdt_ref, gl_ref, halo_ref):
    c = GDN_CHUNK
    dh = GDN_HEAD_DIM
    t = GDN_PREP_ROWS
    ncb = t // c
    first = pl.program_id(1) == 0
    qkv = _silu(_causal_conv(qkv_ref[...].astype(F32), halo_ref, convw_ref[...], first))

    sm = sm_ref[...]
    lane1 = lax.broadcasted_iota(jnp.int32, (t, LANES), 1)
    is_a = (lane1 >= SM_A) & (lane1 < SM_A + GDN_HEADS)
    log_decay = jnp.where(is_a, -jnp.exp(alog_ref[...]) * _softplus(sm + dtb_ref[...]), 0.0)
    beta = _sigmoid(sm)
    row_t = lax.broadcasted_iota(jnp.int32, (t, t), 0)
    col_t = lax.broadcasted_iota(jnp.int32, (t, t), 1)
    same_chunk = (row_t >> GDN_CHUNK_SHIFT) == (col_t >> GDN_CHUNK_SHIFT)
    tri = jnp.where(same_chunk, jnp.where(_time_index(row_t) >= _time_index(col_t), 1.0, 0.0), 0.0).astype(BF16)
    gc = _dot_split_rhs(tri, log_decay)
    gc_t = gc.T
    g_last = jnp.concatenate(
        [jnp.broadcast_to(gc[(i + 1) * c - 1:(i + 1) * c, :], (c, LANES)) for i in range(ncb)], axis=0)
    ex = _dot_split_lhs(jnp.concatenate([jnp.exp(gc), jnp.exp(g_last - gc)], axis=0), eg_ref[...])
    egc_x, erev_x = ex[:t], ex[t:]
    beta_x = _dot_split_lhs(beta, eb_ref[...])

    row = _time_index(lax.broadcasted_iota(jnp.int32, (c, c), 0))
    col = _time_index(lax.broadcasted_iota(jnp.int32, (c, c), 1))
    incl = row >= col
    strict = row > col

    qs, ks, kbs, rhs = [], [], [], []
    for h in range(GDN_HEADS):
        hs = slice(h * dh, (h + 1) * dh)
        qh = qkv[:, h * dh:(h + 1) * dh]
        kh = qkv[:, GDN_DIM + h * dh:GDN_DIM + (h + 1) * dh]
        vh = qkv[:, 2 * GDN_DIM + h * dh:2 * GDN_DIM + (h + 1) * dh]
        qh = qh * lax.rsqrt(jnp.sum(qh * qh, axis=-1, keepdims=True) + EPS) * (dh ** -0.5)
        kh = kh * lax.rsqrt(jnp.sum(kh * kh, axis=-1, keepdims=True) + EPS)
        kb = kh * beta_x[:, hs]
        qs.append(qh)
        ks.append(kh)
        kbs.append(kb)
        rhs.append(jnp.concatenate([vh * beta_x[:, hs], kb * egc_x[:, hs]], axis=1).astype(BF16))

    items = [(i, h) for i in range(ncb) for h in range(GDN_HEADS)]
    ms, aqks = [], []
    for i, h in items:
        rs = slice(i * c, (i + 1) * c)
        la = SM_A + h
        decay = jnp.exp(jnp.where(incl, gc[rs, la:la + 1] - gc_t[la:la + 1, rs], -jnp.inf))
        nt = _dot_nt(jnp.concatenate([kbs[h][rs], qs[h][rs]], axis=0).astype(BF16), ks[h][rs].astype(BF16))
        ms.append(jnp.where(strict, nt[:c] * decay, 0.0))
        aqks.append((nt[c:] * decay).astype(BF16))
    t_invs = _unit_lower_inverse_many(ms, row, col)
    uws = [_dot(t_inv.astype(BF16), rhs[h][i * c:(i + 1) * c]) for (i, h), t_inv in zip(items, t_invs)]

    def by_chunk(pieces):
        return jnp.concatenate(
            [jnp.concatenate(pieces[i * GDN_HEADS:(i + 1) * GDN_HEADS], axis=1) for i in range(ncb)], axis=0)

    u_ref[...] = by_chunk([uw[:, :dh] for uw in uws]).astype(BF16)
    w_ref[...] = by_chunk([uw[:, dh:] for uw in uws]).astype(BF16)
    aqk_ref[...] = by_chunk(aqks)
    qd_ref[...] = jnp.concatenate(
        [qs[h] * egc_x[:, h * dh:(h + 1) * dh] for h in range(GDN_HEADS)], axis=1).astype(BF16)
    for i in range(ncb):
        rs = slice(i * c, (i + 1) * c)
        kdt_ref[i] = jnp.concatenate(
            [(ks[h][rs] * erev_x[rs, h * dh:(h + 1) * dh]).T for h in range(GDN_HEADS)], axis=1).astype(BF16)
        gl_ref[i] = egc_x[(i + 1) * c - 1:(i + 1) * c, :]


def _gdn_scan_kernel(u_ref, w_ref, qd_ref, aqk_ref, kdt_ref, gl_ref, zg_ref, normw_ref, o_ref, state_ref):
    c = GDN_CHUNK
    dh = GDN_HEAD_DIM

    @pl.when(pl.program_id(0) == 0)
    def _():
        state_ref[...] = jnp.zeros_like(state_ref)

    zero = jnp.zeros((c, dh), BF16)
    for b in range(u_ref.shape[0]):
        ss, ps = [], []
        for h in range(GDN_HEADS):
            hs = slice(h * dh, (h + 1) * dh)
            s = state_ref[b, h]
            ss.append(s)
            ps.append(_dot(jnp.concatenate([w_ref[b, :, hs], qd_ref[b, :, hs]], axis=0), s.astype(BF16)))
        v_news = [(u_ref[b, :, h * dh:(h + 1) * dh].astype(F32) - ps[h][:c]).astype(BF16)
                  for h in range(GDN_HEADS)]
        rs = []
        for pr in range(GDN_HEADS // 2):
            lhs = jnp.concatenate([aqk_ref[b, :, pr * dh:(pr + 1) * dh], kdt_ref[b, 0, :, pr * dh:(pr + 1) * dh]],
                                  axis=0)
            bd = jnp.concatenate([jnp.concatenate([v_news[2 * pr], zero], axis=1),
                                  jnp.concatenate([zero, v_news[2 * pr + 1]], axis=1)], axis=0)
            rs.append(_dot(lhs, bd))
        outs = []
        for h in range(GDN_HEADS):
            hs = slice(h * dh, (h + 1) * dh)
            r = rs[h // 2][:, (h % 2) * dh:(h % 2 + 1) * dh]
            state_ref[b, h] = ss[h] * gl_ref[b, 0, :, hs] + r[c:]
            o = ps[h][c:] + r[:c]
            o = o * lax.rsqrt(jnp.mean(o * o, axis=-1, keepdims=True) + EPS) * normw_ref[...]
            outs.append(o * _silu(zg_ref[b, :, hs].astype(F32)))
        o_ref[b] = jnp.concatenate(outs, axis=1).astype(BF16)


def _gdn(proj, small, convw, dtb, alog, normw, expand_a, expand_b, bsz, seqlen):
    c = GDN_CHUNK
    t = GDN_PREP_ROWS
    nc = seqlen // c
    nb = seqlen // t
    n = bsz * seqlen
    hh = GDN_HEADS * c
    rowmap = lambda b, i: b * nb + i
    const = lambda b, i: (0, 0)
    params = pltpu.CompilerParams(dimension_semantics=("arbitrary", "arbitrary"), vmem_limit_bytes=VMEM_LIMIT)
    u, w, qd, aqk, kdt, gl = pl.pallas_call(
        _gdn_prep_kernel,
        grid=(bsz, nb),
        in_specs=[
            pl.BlockSpec((t, 3 * GDN_DIM), lambda b, i: (rowmap(b, i), QKV_BLK)),
            pl.BlockSpec((t, LANES), lambda b, i: (rowmap(b, i), 0)),
            pl.BlockSpec((CONV_K, 3 * GDN_DIM), const),
            pl.BlockSpec((1, LANES), const),
            pl.BlockSpec((1, LANES), const),
            pl.BlockSpec((LANES, GDN_DIM), const),
            pl.BlockSpec((LANES, GDN_DIM), const),
        ],
        out_specs=[
            pl.BlockSpec((t, GDN_DIM), lambda b, i: (rowmap(b, i), 0)),
            pl.BlockSpec((t, GDN_DIM), lambda b, i: (rowmap(b, i), 0)),
            pl.BlockSpec((t, GDN_DIM), lambda b, i: (rowmap(b, i), 0)),
            pl.BlockSpec((t, hh), lambda b, i: (rowmap(b, i), 0)),
            pl.BlockSpec((t // c, GDN_HEAD_DIM, hh), lambda b, i: (rowmap(b, i), 0, 0)),
            pl.BlockSpec((t // c, 1, GDN_DIM), lambda b, i: (rowmap(b, i), 0, 0)),
        ],
        out_shape=[
            jax.ShapeDtypeStruct((n, GDN_DIM), BF16),
            jax.ShapeDtypeStruct((n, GDN_DIM), BF16),
            jax.ShapeDtypeStruct((n, GDN_DIM), BF16),
            jax.ShapeDtypeStruct((n, hh), BF16),
            jax.ShapeDtypeStruct((n // c, GDN_HEAD_DIM, hh), BF16),
            jax.ShapeDtypeStruct((n // c, 1, GDN_DIM), F32),
        ],
        scratch_shapes=[pltpu.VMEM((HALO, 3 * GDN_DIM), F32)],
        compiler_params=params,
        name="gdn_prep",
    )(proj, small, convw, dtb, alog, expand_a, expand_b)

    seq3 = lambda i: (0, i, 0)
    seq4 = lambda i: (0, i, 0, 0)
    o = pl.pallas_call(
        _gdn_scan_kernel,
        grid=(nc,),
        in_specs=[
            pl.BlockSpec((bsz, c, GDN_DIM), seq3),
            pl.BlockSpec((bsz, c, GDN_DIM), seq3),
            pl.BlockSpec((bsz, c, GDN_DIM), seq3),
            pl.BlockSpec((bsz, c, hh), seq3),
            pl.BlockSpec((bsz, 1, GDN_HEAD_DIM, hh), seq4),
            pl.BlockSpec((bsz, 1, 1, GDN_DIM), seq4),
            pl.BlockSpec((bsz, c, GDN_DIM), lambda i: (0, i, ZG_BLK)),
            pl.BlockSpec((1, GDN_HEAD_DIM), lambda i: (0, 0)),
        ],
        out_specs=pl.BlockSpec((bsz, c, GDN_DIM), seq3),
        out_shape=jax.ShapeDtypeStruct((bsz, seqlen, GDN_DIM), BF16),
        scratch_shapes=[pltpu.VMEM((bsz, GDN_HEADS, GDN_HEAD_DIM, GDN_HEAD_DIM), F32)],
        compiler_params=pltpu.CompilerParams(dimension_semantics=("arbitrary",), vmem_limit_bytes=VMEM_LIMIT),
        name="gdn_scan",
    )(u.reshape(bsz, seqlen, GDN_DIM), w.reshape(bsz, seqlen, GDN_DIM), qd.reshape(bsz, seqlen, GDN_DIM),
      aqk.reshape(bsz, seqlen, hh), kdt.reshape(bsz, nc, GDN_HEAD_DIM, hh), gl.reshape(bsz, nc, 1, GDN_DIM),
      proj.reshape(bsz, seqlen, MAIN_COLS), normw)
    return o.reshape(n, GDN_DIM)


def _merge_kernel(y_ref, o_ref, gate_ref, h_ref, bm_ref, wsp_ref, wgp_ref, wout_ref, out_ref):
    d = h_ref.shape[1]
    y_ssd = _dot(y_ref[...], wsp_ref[...])
    y_gdn = _dot(o_ref[...], wgp_ref[...])
    gates = _sigmoid(gate_ref[...].astype(F32) + bm_ref[...])
    merged = gates[:, :d] * y_ssd + gates[:, d:] * y_gdn
    out_ref[...] = h_ref[...] + _dot(merged.astype(BF16), wout_ref[...])


def _merge(y, o, proj, h, b_merge, w_sp, w_gp, w_out, layer, tm):
    n, d = h.shape
    const = lambda i: (0, 0)
    return pl.pallas_call(
        _merge_kernel,
        grid=(n // tm,),
        in_specs=[
            pl.BlockSpec((tm, SSD_D_INNER), lambda i: (i, 0)),
            pl.BlockSpec((tm, GDN_DIM), lambda i: (i, 0)),
            pl.BlockSpec((tm, 2 * d), lambda i: (i, GATE_BLK)),
            pl.BlockSpec((tm, d), lambda i: (i, 0)),
            pl.BlockSpec((1, 2 * d), const),
            pl.BlockSpec((pl.Squeezed(), SSD_D_INNER, d), lambda i: (layer, 0, 0)),
            pl.BlockSpec((pl.Squeezed(), GDN_DIM, d), lambda i: (layer, 0, 0)),
            pl.BlockSpec((pl.Squeezed(), d, d), lambda i: (layer, 0, 0)),
        ],
        out_specs=pl.BlockSpec((tm, d), lambda i: (i, 0)),
        out_shape=jax.ShapeDtypeStruct((n, d), F32),
        compiler_params=pltpu.CompilerParams(
            dimension_semantics=("arbitrary",), vmem_limit_bytes=VMEM_LIMIT),
        name="merge",
    )(y, o, proj, h, b_merge, w_sp, w_gp, w_out)


MOE_PAIRS = MOE_EPG * (MOE_EPG - 1) // 2
MOE_BUCKETS = MOE_GROUPS * MOE_PAIRS
MOE_TILE = 256
ROW_DMA_UNROLL = 32
RT_BUCKET, RT_WLO, RT_WHI, RT_RANK = 0, 1, 2, 3


def _route(logits):
    t = logits.shape[1]
    neg = -jnp.inf
    big = jnp.int32(LANES)
    gl = logits[RT_G:RT_G + MOE_GROUPS, :]
    grow = lax.broadcasted_iota(jnp.int32, (MOE_GROUPS, t), 0)
    gmax = jnp.max(gl, axis=0, keepdims=True)
    gsel = jnp.min(jnp.where(gl == gmax, grow, big), axis=0, keepdims=True)
    group_w = 1.0 / jnp.sum(jnp.exp(gl - gmax), axis=0, keepdims=True)
    erow = lax.broadcasted_iota(jnp.int32, (MOE_EXPERTS, t), 0)
    el = jnp.where((erow >> (MOE_EPG.bit_length() - 1)) == gsel, logits[RT_E:RT_E + MOE_EXPERTS, :], neg)
    m1 = jnp.max(el, axis=0, keepdims=True)
    i1 = jnp.min(jnp.where(el == m1, erow, big), axis=0, keepdims=True)
    el2 = jnp.where(erow == i1, neg, el)
    m2 = jnp.max(el2, axis=0, keepdims=True)
    i2 = jnp.min(jnp.where(el2 == m2, erow, big), axis=0, keepdims=True)
    z = jnp.sum(jnp.exp(el - m1), axis=0, keepdims=True)
    p1 = 1.0 / z
    p2 = jnp.exp(m2 - m1) / z
    tot = p1 + p2
    w1 = group_w * (p1 / tot)
    w2 = group_w * (p2 / tot)
    l1 = i1 - gsel * MOE_EPG
    l2 = i2 - gsel * MOE_EPG
    lo = jnp.minimum(l1, l2)
    hi = jnp.maximum(l1, l2)
    pair = ((lo * (2 * MOE_EPG - 1 - lo)) >> 1) + (hi - lo - 1)
    bucket = gsel * MOE_PAIRS + pair
    first_lo = l1 < l2
    return bucket, jnp.where(first_lo, w1, w2), jnp.where(first_lo, w2, w1)


def _moe_rmsnorm(x, lnw):
    return x * lax.rsqrt(jnp.mean(x * x, axis=-1, keepdims=True) + EPS) * lnw


def _router_kernel(h_ref, lnw_ref, wr_ref, br_ref, info_ref, counts_ref, run_ref):
    t = h_ref.shape[0]

    @pl.when(pl.program_id(0) == 0)
    def _():
        run_ref[...] = jnp.zeros_like(run_ref)

    hi, lo = _split(_moe_rmsnorm(h_ref[...], lnw_ref[...]))
    whi, wlo = wr_ref[0], wr_ref[1]
    logits = _dot_nt(whi, hi) + _dot_nt(whi, lo) + _dot_nt(wlo, hi) + br_ref[:, 0:1]
    bucket, w_lo, w_hi = _route(logits)
    brow = lax.broadcasted_iota(jnp.int32, (MOE_BUCKET_ROWS, t), 0)
    onehot = jnp.where(brow == bucket, 1.0, 0.0)
    row = lax.broadcasted_iota(jnp.int32, (t, t), 0)
    col = lax.broadcasted_iota(jnp.int32, (t, t), 1)
    earlier = _dot(onehot.astype(BF16), jnp.where(row < col, 1.0, 0.0).astype(BF16))
    rank = jnp.sum(onehot * (earlier + run_ref[:, 0:1]), axis=0, keepdims=True)
    run_ref[...] += jnp.sum(onehot, axis=1, keepdims=True)
    record = jnp.concatenate(
        [bucket.astype(F32), w_lo, w_hi, rank, jnp.zeros((LANES - 4, t), F32)], axis=0)
    info_ref[...] = record.T
    counts_ref[...] = run_ref[...]


def _dispatch_kernel(pos_ref, h_ref, info_ref, xs_in_hbm, xs_hbm, buf_ref, sem):
    del xs_in_hbm
    tm, d = h_ref.shape
    i = pl.program_id(0)
    last = pl.num_programs(0) - 1
    base = i * tm
    slot = i % 2

    def wait_rows(s):
        pltpu.make_async_copy(buf_ref.at[s], xs_hbm.at[pl.ds(0, tm)], sem.at[s]).wait()

    @pl.when(i >= 2)
    def _():
        wait_rows(slot)

    buf_ref[slot, :, :d] = h_ref[...]
    buf_ref[slot, :, d:] = info_ref[...]

    def issue(r, carry):
        pltpu.make_async_copy(buf_ref.at[slot, pl.ds(r, 1)], xs_hbm.at[pl.ds(pos_ref[base + r], 1)],
                              sem.at[slot]).start()
        return carry

    lax.fori_loop(0, tm, issue, 0, unroll=ROW_DMA_UNROLL)

    @pl.when(i == last)
    def _():
        wait_rows(slot)

    @pl.when((i == last) & (i >= 1))
    def _():
        wait_rows(1 - slot)


def _experts_kernel(lo_ref, hi_ref, nt_ref, x_ref, lnw_ref, wg_lo, wu_lo, wd_lo, wg_hi, wu_hi, wd_hi, y_ref):
    live = pl.program_id(0) < nt_ref[0]
    d = y_ref.shape[1]

    @pl.when(live)
    def _():
        xn = _moe_rmsnorm(x_ref[:, :d], lnw_ref[...]).astype(BF16)
        wts = x_ref[:, d + RT_WLO:d + RT_WHI + 1]
        y = None
        for j, (wg, wu, wd) in enumerate(((wg_lo, wu_lo, wd_lo), (wg_hi, wu_hi, wd_hi))):
            hid = _silu(_dot(xn, wg[0])) * _dot(xn, wu[0]) * wts[:, j:j + 1]
            part = _dot(hid.astype(BF16), wd[0])
            y = part if y is None else y + part
        y_ref[...] = x_ref[:, :d] + y

    @pl.when(jnp.logical_not(live))
    def _():
        y_ref[...] = jnp.zeros_like(y_ref)


def _combine_kernel(pos_ref, y_hbm, fnw_ref, out_ref, buf_ref, sem, *, final_norm):
    tm = out_ref.shape[0]
    i = pl.program_id(0)
    slot = i % 2

    def issue_tile(tile, s):
        base = tile * tm

        def issue(r, carry):
            pltpu.make_async_copy(y_hbm.at[pl.ds(pos_ref[base + r], 1)], buf_ref.at[s, pl.ds(r, 1)],
                                  sem.at[s]).start()
            return carry

        lax.fori_loop(0, tm, issue, 0, unroll=ROW_DMA_UNROLL)

    @pl.when(i == 0)
    def _():
        issue_tile(0, 0)

    @pl.when(i + 1 < pl.num_programs(0))
    def _():
        issue_tile(i + 1, 1 - slot)

    pltpu.make_async_copy(y_hbm.at[pl.ds(0, tm)], buf_ref.at[slot], sem.at[slot]).wait()
    y = buf_ref[slot]
    if final_norm:
        y = _moe_rmsnorm(y, fnw_ref[...])
    out_ref[...] = y


def _moe(h, lnw, w_router, b_router, w_gate, w_up, w_down, fnw, layer, tm, final_norm, out_order, slots):
    n, d = h.shape
    dff = w_gate.shape[-1]
    tile = MOE_TILE
    n_tiles = n // tile + MOE_BUCKETS
    p = n_tiles * tile
    const = lambda i: (0, 0)
    info, counts = pl.pallas_call(
        _router_kernel,
        grid=(n // tm,),
        in_specs=[
            pl.BlockSpec((tm, d), lambda i: (i, 0)),
            pl.BlockSpec((1, d), const),
            pl.BlockSpec((2, LANES, d), lambda i: (0, 0, 0)),
            pl.BlockSpec((LANES, LANES), const),
        ],
        out_specs=[pl.BlockSpec((tm, LANES), lambda i: (i, 0)), pl.BlockSpec((MOE_BUCKET_ROWS, LANES), const)],
        out_shape=[jax.ShapeDtypeStruct((n, LANES), F32), jax.ShapeDtypeStruct((MOE_BUCKET_ROWS, LANES), F32)],
        scratch_shapes=[pltpu.VMEM((MOE_BUCKET_ROWS, LANES), F32)],
        compiler_params=pltpu.CompilerParams(dimension_semantics=("arbitrary",), vmem_limit_bytes=VMEM_LIMIT),
        name="moe_router",
    )(h, lnw, w_router, b_router)

    tiles_per = (counts[:MOE_BUCKETS, 0].astype(jnp.int32) + tile - 1) // tile
    tile_end = jnp.cumsum(tiles_per)
    first_slot = ((tile_end - tiles_per) * tile).astype(F32)
    bucket_is = info[:, RT_BUCKET:RT_BUCKET + 1] == jnp.arange(MOE_BUCKETS, dtype=F32)[None, :]
    pos = (jnp.sum(jnp.where(bucket_is, first_slot[None, :], 0.0), axis=1) + info[:, RT_RANK]).astype(jnp.int32)
    n_live = tile_end[-1]
    tile_bucket = jnp.searchsorted(tile_end, jnp.minimum(jnp.arange(n_tiles, dtype=jnp.int32), n_live - 1),
                                   side="right").astype(jnp.int32)
    pairs = [(a, b) for a in range(MOE_EPG) for b in range(a + 1, MOE_EPG)]
    grp = tile_bucket // MOE_PAIRS
    tile_lo = grp * MOE_EPG + jnp.array([a for a, _ in pairs], jnp.int32)[tile_bucket % MOE_PAIRS]
    tile_hi = grp * MOE_EPG + jnp.array([b for _, b in pairs], jnp.int32)[tile_bucket % MOE_PAIRS]

    x_sorted = pl.pallas_call(
        _dispatch_kernel,
        grid_spec=pltpu.PrefetchScalarGridSpec(
            num_scalar_prefetch=1, grid=(n // tile,),
            in_specs=[
                pl.BlockSpec((tile, d), lambda i, pos: (i, 0)),
                pl.BlockSpec((tile, LANES), lambda i, pos: (i, 0)),
                pl.BlockSpec(memory_space=pl.ANY),
            ],
            out_specs=pl.BlockSpec(memory_space=pl.ANY),
            scratch_shapes=[pltpu.VMEM((2, tile, d + LANES), F32), pltpu.SemaphoreType.DMA((2,))]),
        out_shape=jax.ShapeDtypeStruct((p, d + LANES), F32),
        input_output_aliases={3: 0},
        compiler_params=pltpu.CompilerParams(dimension_semantics=("arbitrary",), vmem_limit_bytes=VMEM_LIMIT),
        name="moe_dispatch",
    )(pos, h, info, jnp.zeros((p, d + LANES), F32) if slots is None else slots)

    lo_map = lambda t, lo, hi, nt: (layer, lo[t], 0, 0)
    hi_map = lambda t, lo, hi, nt: (layer, hi[t], 0, 0)
    y_sorted = pl.pallas_call(
        _experts_kernel,
        grid_spec=pltpu.PrefetchScalarGridSpec(
            num_scalar_prefetch=3, grid=(n_tiles,),
            in_specs=[
                pl.BlockSpec((tile, d + LANES), lambda t, lo, hi, nt: (t, 0)),
                pl.BlockSpec((1, d), lambda t, lo, hi, nt: (0, 0)),
                pl.BlockSpec((pl.Squeezed(), 1, d, dff), lo_map),
                pl.BlockSpec((pl.Squeezed(), 1, d, dff), lo_map),
                pl.BlockSpec((pl.Squeezed(), 1, dff, d), lo_map),
                pl.BlockSpec((pl.Squeezed(), 1, d, dff), hi_map),
                pl.BlockSpec((pl.Squeezed(), 1, d, dff), hi_map),
                pl.BlockSpec((pl.Squeezed(), 1, dff, d), hi_map),
            ],
            out_specs=pl.BlockSpec((tile, d), lambda t, lo, hi, nt: (t, 0))),
        out_shape=jax.ShapeDtypeStruct((p, d), F32),
        compiler_params=pltpu.CompilerParams(dimension_semantics=("arbitrary",), vmem_limit_bytes=VMEM_LIMIT),
        name="moe_experts",
    )(tile_lo, tile_hi, n_live.reshape(1).astype(jnp.int32), x_sorted, lnw,
      w_gate, w_up, w_down, w_gate, w_up, w_down)

    out = pl.pallas_call(
        functools.partial(_combine_kernel, final_norm=final_norm),
        grid_spec=pltpu.PrefetchScalarGridSpec(
            num_scalar_prefetch=1, grid=(n // tile,),
            in_specs=[
                pl.BlockSpec(memory_space=pl.ANY),
                pl.BlockSpec((1, d), lambda i, pos: (0, 0)),
            ],
            out_specs=pl.BlockSpec((tile, d), lambda i, pos: (i, 0)),
            scratch_shapes=[pltpu.VMEM((2, tile, d), F32), pltpu.SemaphoreType.DMA((2,))]),
        out_shape=jax.ShapeDtypeStruct((n, d), F32),
        compiler_params=pltpu.CompilerParams(dimension_semantics=("arbitrary",), vmem_limit_bytes=VMEM_LIMIT),
        name="moe_combine",
    )(pos if out_order is None else out_order(pos), y_sorted, fnw)
    return out, x_sorted


def _lane_row(v, offset):
    return jnp.zeros((1, LANES), F32).at[0, offset:offset + v.shape[0]].set(v.astype(F32))


def _expansion(offset, heads, width):
    r = jnp.arange(LANES)[:, None]
    c = jnp.arange(heads * width)[None, :]
    return (r == offset + c // width).astype(BF16)


def _row_tile(n, want):
    t = min(n, want)
    while n % t:
        t //= 2
    return t


def kernel(x, ln1_w, w_in, ssd_conv_w, ssd_conv_b, ssd_dt_bias, ssd_a_log, ssd_d, ssd_norm_w, w_ssd_proj,
           gdn_conv_w, gdn_dt_bias, gdn_a_log, gdn_norm_w, w_gdn_proj, b_merge, w_out, ln2_w,
           moe_w_group, moe_b_group, moe_w_expert, moe_b_expert, moe_w_gate, moe_w_up, moe_w_down,
           final_norm_w):
    bsz, seqlen, d = x.shape
    n = bsz * seqlen
    depth = w_in.shape[0]

    def swap_order(a):
        blocks = a.reshape((n // TIME_BLOCK, SUBLANES, SUBLANES) + a.shape[1:])
        return blocks.swapaxes(1, 2).reshape(a.shape)

    h = swap_order(x.reshape(n, d))

    o1 = SSD_D_INNER
    o2 = o1 + SSD_XBC
    o3 = o2 + SSD_HEADS
    o4 = o3 + 3 * GDN_DIM
    o5 = o4 + GDN_HEADS
    o6 = o5 + GDN_HEADS
    o7 = o6 + GDN_DIM

    e_ssd = _expansion(SM_DT, SSD_HEADS, SSD_HEAD_DIM)
    e_gdn_a = _expansion(SM_A, GDN_HEADS, GDN_HEAD_DIM)
    e_gdn_b = _expansion(SM_BETA, GDN_HEADS, GDN_HEAD_DIM)
    tm_proj = _row_tile(n, 1024)
    tm_merge = _row_tile(n, 512)
    tm_moe = _row_tile(n, 512)

    w_main = jnp.concatenate(
        [w_in[:, :, o1:o2], w_in[:, :, o3:o4], w_in[:, :, :o1], w_in[:, :, o7:], w_in[:, :, o6:o7]],
        axis=2).astype(BF16)
    w_small = jnp.concatenate(
        [w_in[:, :, o2:o3], w_in[:, :, o4:o5], w_in[:, :, o5:o6],
         jnp.zeros((depth, d, LANES - SSD_HEADS - 2 * GDN_HEADS), F32)], axis=2).astype(BF16)
    w_sp, w_gp, w_o = w_ssd_proj.astype(BF16), w_gdn_proj.astype(BF16), w_out.astype(BF16)
    w_gate, w_up, w_down = moe_w_gate.astype(BF16), moe_w_up.astype(BF16), moe_w_down.astype(BF16)

    slots = None
    for i in range(depth):
        proj, small = _in_proj(h, ln1_w[i][None, :], w_main, w_small, i, tm_proj, MAIN_COLS // 4)

        y = _ssd(proj, small, ssd_conv_w[i], ssd_conv_b[i][None, :],
                 _lane_row(ssd_dt_bias[i], SM_DT), _lane_row(ssd_a_log[i], SM_DT),
                 jnp.repeat(ssd_d[i], SSD_HEAD_DIM)[None, :], ssd_norm_w[i][None, :], e_ssd, bsz, seqlen)
        o = _gdn(proj, small, gdn_conv_w[i], _lane_row(gdn_dt_bias[i], SM_A), _lane_row(gdn_a_log[i], SM_A),
                 gdn_norm_w[i][None, :], e_gdn_a, e_gdn_b, bsz, seqlen)
        h = _merge(y, o, proj, h, b_merge[i][None, :], w_sp, w_gp, w_o, i, tm_merge)

        gap = jnp.zeros((d, RT_E - RT_G - MOE_GROUPS), F32)
        rest = jnp.zeros((d, LANES - RT_E - MOE_EXPERTS), F32)
        w_r = jnp.concatenate([moe_w_group[i], gap, moe_w_expert[i], rest], axis=1).T
        w_r_hi = w_r.astype(BF16)
        w_r_lo = (w_r - w_r_hi.astype(F32)).astype(BF16)
        b_r = jnp.broadcast_to(
            jnp.concatenate([moe_b_group[i], gap[0], moe_b_expert[i], rest[0]])[:, None], (LANES, LANES))
        h, slots = _moe(h, ln2_w[i][None, :], jnp.stack([w_r_hi, w_r_lo]), b_r, w_gate, w_up, w_down,
                        final_norm_w[None, :], i, tm_moe, final_norm=(i == depth - 1),
                        out_order=swap_order if i == depth - 1 else None, slots=slots)
    return h.reshape(bsz, seqlen, d)
```

```python
import functools

import jax
import jax.numpy as jnp
from jax import lax
from jax.experimental import pallas as pl
from jax.experimental.pallas import tpu as pltpu

F32 = jnp.float32
BF16 = jnp.bfloat16

EPS = 1e-6
CONV_K = 4
LANES = 128
SUBLANES = 8
TIME_BLOCK = SUBLANES * SUBLANES
HALO = (CONV_K - 1) * SUBLANES

D_MODEL = 1024
SSD_D_INNER = 2 * D_MODEL
SSD_HEAD_DIM = 64
SSD_HEADS = SSD_D_INNER // SSD_HEAD_DIM
SSD_GROUPS = 4
SSD_STATE = 128
SSD_CHUNK = 128
SSD_XBC = SSD_D_INNER + 2 * SSD_GROUPS * SSD_STATE
GDN_HEADS = 8
GDN_HEAD_DIM = 128
GDN_DIM = GDN_HEADS * GDN_HEAD_DIM
GDN_CHUNK = 64
MOE_GROUPS = 4
MOE_EPG = 4
MOE_EXPERTS = MOE_GROUPS * MOE_EPG
MOE_D_FF = 512

MAIN_COLS = SSD_XBC + 3 * GDN_DIM + SSD_D_INNER + 2 * D_MODEL + GDN_DIM
XBC_BLK, QKV_BLK = 0, 1
ZS_BLK, GATE_BLK = 3, 4
ZG_BLK = 10
SM_DT, SM_A, SM_BETA = 0, SSD_HEADS, SSD_HEADS + GDN_HEADS
RT_G, RT_E = 0, SUBLANES
MOE_BUCKET_ROWS = 32

VMEM_LIMIT = 56 * 1024 * 1024


def _dot(a, b):
    return jnp.dot(a, b, preferred_element_type=F32)


def _dot_nt(a, b):
    return lax.dot_general(a, b, (((1,), (1,)), ((), ())), preferred_element_type=F32)


def _split(x):
    hi = x.astype(BF16)
    lo = (x - hi.astype(F32)).astype(BF16)
    return hi, lo


def _dot_split_lhs(x, w):
    hi, lo = _split(x)
    return _dot(hi, w) + _dot(lo, w)


def _dot_split_rhs(w, x):
    hi, lo = _split(x)
    return _dot(w, hi) + _dot(w, lo)


def _sigmoid(x):
    return 1.0 / (1.0 + jnp.exp(-x))


def _silu(x):
    return x * _sigmoid(x)


def _softplus(x):
    return jnp.maximum(x, 0.0) + jnp.log1p(jnp.exp(-jnp.abs(x)))


def _time_index(r):
    return (r & ~(TIME_BLOCK - 1)) | ((r >> 3) & (SUBLANES - 1)) | ((r & (SUBLANES - 1)) << 3)


def _causal_conv(cur, tail_ref, w, first):
    t, c = cur.shape

    @pl.when(first)
    def _():
        tail_ref[...] = jnp.zeros_like(tail_ref)

    sub = lax.broadcasted_iota(jnp.int32, (SUBLANES, c), 0)
    prev_tail = tail_ref[...]
    outs = []
    for i in range(t // TIME_BLOCK):
        blk = cur[i * TIME_BLOCK:(i + 1) * TIME_BLOCK, :]
        tail = blk[TIME_BLOCK - HALO:, :]
        wrapped = [
            jnp.where(sub == 0,
                      pltpu.roll(prev_tail[j * SUBLANES:(j + 1) * SUBLANES, :], 1, axis=0),
                      pltpu.roll(tail[j * SUBLANES:(j + 1) * SUBLANES, :], 1, axis=0))
            for j in range(CONV_K - 1)]
        ext = jnp.concatenate(wrapped + [blk], axis=0)
        acc = None
        for k in range(CONV_K):
            term = ext[k * SUBLANES:k * SUBLANES + TIME_BLOCK, :] * w[k:k + 1, :]
            acc = term if acc is None else acc + term
        outs.append(acc)
        prev_tail = tail
    tail_ref[...] = prev_tail
    return jnp.concatenate(outs, axis=0)


def _inproj_kernel(x_ref, lnw_ref, w_ref, ws_ref, o_ref, os_ref, xn_ref):
    @pl.when(pl.program_id(1) == 0)
    def _():
        x = x_ref[...]
        xn = x * lax.rsqrt(jnp.mean(x * x, axis=-1, keepdims=True) + EPS) * lnw_ref[...]
        xn_ref[...] = xn.astype(BF16)
        os_ref[...] = _dot(xn_ref[...], ws_ref[...])

    o_ref[...] = _dot(xn_ref[...], w_ref[...]).astype(o_ref.dtype)


def _in_proj(x, lnw, w_main, w_small, layer, tm, tn):
    n, d = x.shape
    nm = w_main.shape[2]
    return pl.pallas_call(
        _inproj_kernel,
        grid=(n // tm, nm // tn),
        in_specs=[
            pl.BlockSpec((tm, d), lambda i, j: (i, 0)),
            pl.BlockSpec((1, d), lambda i, j: (0, 0)),
            pl.BlockSpec((pl.Squeezed(), d, tn), lambda i, j: (layer, 0, j)),
            pl.BlockSpec((pl.Squeezed(), d, LANES), lambda i, j: (layer, 0, 0)),
        ],
        out_specs=[
            pl.BlockSpec((tm, tn), lambda i, j: (i, j)),
            pl.BlockSpec((tm, LANES), lambda i, j: (i, 0)),
        ],
        out_shape=[jax.ShapeDtypeStruct((n, nm), BF16), jax.ShapeDtypeStruct((n, LANES), F32)],
        scratch_shapes=[pltpu.VMEM((tm, d), BF16)],
        compiler_params=pltpu.CompilerParams(
            dimension_semantics=("arbitrary", "arbitrary"), vmem_limit_bytes=VMEM_LIMIT),
        name="in_proj",
    )(x, lnw, w_main, w_small)


SSD_STEP_ROWS = 2 * SSD_CHUNK


def _ssd_kernel(xbc_ref, z_ref, sm_ref, convw_ref, convb_ref, dtb_ref, alog_ref, dexp_ref, normw_ref, e_ref,
                y_ref, halo_ref, state_ref, yacc_ref):
    q = SSD_CHUNK
    p = SSD_HEAD_DIM
    ns = SSD_STATE
    gw = (SSD_HEADS // SSD_GROUPS) * p
    first = pl.program_id(1) == 0

    @pl.when(first)
    def _():
        state_ref[...] = jnp.zeros_like(state_ref)

    conv = _causal_conv(xbc_ref[...].astype(F32), halo_ref, convw_ref[...], first) + convb_ref[...]
    xbc_all = _silu(conv)
    dt_all = _softplus(sm_ref[...] + dtb_ref[...])
    a_neg = -jnp.exp(alog_ref[...])
    row = lax.broadcasted_iota(jnp.int32, (q, q), 0)
    col = lax.broadcasted_iota(jnp.int32, (q, q), 1)
    causal = _time_index(row) >= _time_index(col)
    tri = jnp.where(causal, 1.0, 0.0).astype(BF16)
    lane = lax.broadcasted_iota(jnp.int32, (q, LANES), 1)

    for ci in range(xbc_ref.shape[0] // q):
        rs = slice(ci * q, (ci + 1) * q)
        xs = xbc_all[rs, :SSD_D_INNER]
        bm = xbc_all[rs, SSD_D_INNER:SSD_D_INNER + SSD_GROUPS * ns]
        cm = xbc_all[rs, SSD_D_INNER + SSD_GROUPS * ns:]
        dt = dt_all[rs, :]
        a_cs = _dot_split_rhs(tri, dt * a_neg)
        a_cs_t = a_cs.T
        e_acs = jnp.exp(a_cs)
        dec_end = jnp.exp(a_cs[q - 1:q, :] - a_cs)
        ex = _dot_split_lhs(jnp.concatenate([dt, dt * dec_end, e_acs], axis=0), e_ref[...])
        dt_x, dtdec_x, eacs_x = ex[:q], ex[q:2 * q], ex[2 * q:]

        xdt_b = (xs * dt_x).astype(BF16)
        xdd_b = (xs * dtdec_x).astype(BF16)
        for g in range(SSD_GROUPS):
            bg = bm[:, g * ns:(g + 1) * ns]
            cg_b = cm[:, g * ns:(g + 1) * ns].astype(BF16)
            scores = _dot_nt(cg_b, bg.astype(BF16))
            s_g = state_ref[:, g * gw:(g + 1) * gw]
            yacc_ref[rs, g * gw:(g + 1) * gw] = _dot(cg_b, s_g.astype(BF16)) * eacs_x[:, g * gw:(g + 1) * gw]
            new_g = _dot(bg.T.astype(BF16), xdd_b[:, g * gw:(g + 1) * gw])
            state_ref[:, g * gw:(g + 1) * gw] = s_g * eacs_x[q - 1:q, g * gw:(g + 1) * gw] + new_g
            for j in range(gw // LANES):
                h0 = (g * gw + j * LANES) // p
                c0 = g * gw + j * LANES
                ms = []
                for h in (h0, h0 + 1):
                    seg = a_cs[:, h:h + 1] - a_cs_t[h:h + 1, :]
                    ms.append((scores * jnp.exp(jnp.where(causal, seg, -jnp.inf))).astype(BF16))
                xp = xdt_b[:, c0:c0 + LANES]
                zero = jnp.zeros_like(xp)
                rhs = jnp.concatenate([jnp.where(lane < p, xp, zero), jnp.where(lane >= p, xp, zero)], axis=0)
                yacc_ref[rs, c0:c0 + LANES] += _dot(jnp.concatenate(ms, axis=1), rhs)

        y = yacc_ref[rs, :] + dexp_ref[...] * xs
        y = y * _silu(z_ref[rs, :].astype(F32))
        y = y * lax.rsqrt(jnp.mean(y * y, axis=-1, keepdims=True) + EPS) * normw_ref[...]
        y_ref[rs, :] = y.astype(BF16)


def _ssd(proj, small, convw, convb, dtb, alog, dexp, normw, expand, bsz, seqlen):
    q = SSD_STEP_ROWS
    nc = seqlen // q
    n = bsz * seqlen
    rowmap = lambda b, c: b * nc + c
    const = lambda b, c: (0, 0)
    return pl.pallas_call(
        _ssd_kernel,
        grid=(bsz, nc),
        in_specs=[
            pl.BlockSpec((q, SSD_XBC), lambda b, c: (rowmap(b, c), XBC_BLK)),
            pl.BlockSpec((q, SSD_D_INNER), lambda b, c: (rowmap(b, c), ZS_BLK)),
            pl.BlockSpec((q, LANES), lambda b, c: (rowmap(b, c), 0)),
            pl.BlockSpec((CONV_K, SSD_XBC), const),
            pl.BlockSpec((1, SSD_XBC), const),
            pl.BlockSpec((1, LANES), const),
            pl.BlockSpec((1, LANES), const),
            pl.BlockSpec((1, SSD_D_INNER), const),
            pl.BlockSpec((1, SSD_D_INNER), const),
            pl.BlockSpec((LANES, SSD_D_INNER), const),
        ],
        out_specs=pl.BlockSpec((q, SSD_D_INNER), lambda b, c: (rowmap(b, c), 0)),
        out_shape=jax.ShapeDtypeStruct((n, SSD_D_INNER), BF16),
        scratch_shapes=[
            pltpu.VMEM((HALO, SSD_XBC), F32),
            pltpu.VMEM((SSD_STATE, SSD_D_INNER), F32),
            pltpu.VMEM((q, SSD_D_INNER), F32),
        ],
        compiler_params=pltpu.CompilerParams(
            dimension_semantics=("arbitrary", "arbitrary"), vmem_limit_bytes=VMEM_LIMIT),
        name="ssd",
    )(proj, proj, small, convw, convb, dtb, alog, dexp, normw, expand)


GDN_PREP_ROWS = 4 * GDN_CHUNK
GDN_CHUNK_SHIFT = GDN_CHUNK.bit_length() - 1


def _unit_lower_inverse_many(ms, row, col):
    c = ms[0].shape[0]
    diff = row ^ col
    eye = jnp.where(row == col, 1.0, 0.0)
    ts = [eye - jnp.where((diff >> 1) == 0, m, 0.0) for m in ms]
    shift = 1
    while (1 << shift) < c:
        sel = (diff >> shift) == 1
        tbs = [t.astype(BF16) for t in ts]
        xs = [_dot(tb, jnp.where(sel, m, 0.0).astype(BF16)) for tb, m in zip(tbs, ms)]
        ts = [t - _dot(x.astype(BF16), tb) for t, x, tb in zip(ts, xs, tbs)]
        shift += 1
    return ts


def _gdn_prep_kernel(qkv_ref, sm_ref, convw_ref, dtb_ref, alog_ref, eg_ref, eb_ref,
                     u_ref, w_ref, qd_ref, aqk_ref, kdt_ref, gl_ref, halo_ref):
    c = GDN_CHUNK
    dh = GDN_HEAD_DIM
    t = GDN_PREP_ROWS
    ncb = t // c
    first = pl.program_id(1) == 0
    qkv = _silu(_causal_conv(qkv_ref[...].astype(F32), halo_ref, convw_ref[...], first))

    sm = sm_ref[...]
    lane1 = lax.broadcasted_iota(jnp.int32, (t, LANES), 1)
    is_a = (lane1 >= SM_A) & (lane1 < SM_A + GDN_HEADS)
    log_decay = jnp.where(is_a, -jnp.exp(alog_ref[...]) * _softplus(sm + dtb_ref[...]), 0.0)
    beta = _sigmoid(sm)
    row_t = lax.broadcasted_iota(jnp.int32, (t, t), 0)
    col_t = lax.broadcasted_iota(jnp.int32, (t, t), 1)
    same_chunk = (row_t >> GDN_CHUNK_SHIFT) == (col_t >> GDN_CHUNK_SHIFT)
    tri = jnp.where(same_chunk, jnp.where(_time_index(row_t) >= _time_index(col_t), 1.0, 0.0), 0.0).astype(BF16)
    gc = _dot_split_rhs(tri, log_decay)
    gc_t = gc.T
    g_last = jnp.concatenate(
        [jnp.broadcast_to(gc[(i + 1) * c - 1:(i + 1) * c, :], (c, LANES)) for i in range(ncb)], axis=0)
    ex = _dot_split_lhs(jnp.concatenate([jnp.exp(gc), jnp.exp(g_last - gc)], axis=0), eg_ref[...])
    egc_x, erev_x = ex[:t], ex[t:]
    beta_x = _dot_split_lhs(beta, eb_ref[...])

    row = _time_index(lax.broadcasted_iota(jnp.int32, (c, c), 0))
    col = _time_index(lax.broadcasted_iota(jnp.int32, (c, c), 1))
    incl = row >= col
    strict = row > col

    qs, ks, kbs, rhs = [], [], [], []
    for h in range(GDN_HEADS):
        hs = slice(h * dh, (h + 1) * dh)
        qh = qkv[:, h * dh:(h + 1) * dh]
        kh = qkv[:, GDN_DIM + h * dh:GDN_DIM + (h + 1) * dh]
        vh = qkv[:, 2 * GDN_DIM + h * dh:2 * GDN_DIM + (h + 1) * dh]
        qh = qh * lax.rsqrt(jnp.sum(qh * qh, axis=-1, keepdims=True) + EPS) * (dh ** -0.5)
        kh = kh * lax.rsqrt(jnp.sum(kh * kh, axis=-1, keepdims=True) + EPS)
        kb = kh * beta_x[:, hs]
        qs.append(qh)
        ks.append(kh)
        kbs.append(kb)
        rhs.append(jnp.concatenate([vh * beta_x[:, hs], kb * egc_x[:, hs]], axis=1).astype(BF16))

    items = [(i, h) for i in range(ncb) for h in range(GDN_HEADS)]
    ms, aqks = [], []
    for i, h in items:
        rs = slice(i * c, (i + 1) * c)
        la = SM_A + h
        decay = jnp.exp(jnp.where(incl, gc[rs, la:la + 1] - gc_t[la:la + 1, rs], -jnp.inf))
        nt = _dot_nt(jnp.concatenate([kbs[h][rs], qs[h][rs]], axis=0).astype(BF16), ks[h][rs].astype(BF16))
        ms.append(jnp.where(strict, nt[:c] * decay, 0.0))
        aqks.append((nt[c:] * decay).astype(BF16))
    t_invs = _unit_lower_inverse_many(ms, row, col)
    uws = [_dot(t_inv.astype(BF16), rhs[h][i * c:(i + 1) * c]) for (i, h), t_inv in zip(items, t_invs)]

    def by_chunk(pieces):
        return jnp.concatenate(
            [jnp.concatenate(pieces[i * GDN_HEADS:(i + 1) * GDN_HEADS], axis=1) for i in range(ncb)], axis=0)

    u_ref[...] = by_chunk([uw[:, :dh] for uw in uws]).astype(BF16)
    w_ref[...] = by_chunk([uw[:, dh:] for uw in uws]).astype(BF16)
    aqk_ref[...] = by_chunk(aqks)
    qd_ref[...] = jnp.concatenate(
        [qs[h] * egc_x[:, h * dh:(h + 1) * dh] for h in range(GDN_HEADS)], axis=1).astype(BF16)
    for i in range(ncb):
        rs = slice(i * c, (i + 1) * c)
        kdt_ref[i] = jnp.concatenate(
            [(ks[h][rs] * erev_x[rs, h * dh:(h + 1) * dh]).T for h in range(GDN_HEADS)], axis=1).astype(BF16)
        gl_ref[i] = egc_x[(i + 1) * c - 1:(i + 1) * c, :]


def _gdn_scan_kernel(u_ref, w_ref, qd_ref, aqk_ref, kdt_ref, gl_ref, zg_ref, normw_ref, o_ref, state_ref):
    c = GDN_CHUNK
    dh = GDN_HEAD_DIM

    @pl.when(pl.program_id(0) == 0)
    def _():
        state_ref[...] = jnp.zeros_like(state_ref)

    zero = jnp.zeros((c, dh), BF16)
    for b in range(u_ref.shape[0]):
        ss, ps = [], []
        for h in range(GDN_HEADS):
            hs = slice(h * dh, (h + 1) * dh)
            s = state_ref[b, h]
            ss.append(s)
            ps.append(_dot(jnp.concatenate([w_ref[b, :, hs], qd_ref[b, :, hs]], axis=0), s.astype(BF16)))
        v_news = [(u_ref[b, :, h * dh:(h + 1) * dh].astype(F32) - ps[h][:c]).astype(BF16)
                  for h in range(GDN_HEADS)]
        rs = []
        for pr in range(GDN_HEADS // 2):
            lhs = jnp.concatenate([aqk_ref[b, :, pr * dh:(pr + 1) * dh], kdt_ref[b, 0, :, pr * dh:(pr + 1) * dh]],
                                  axis=0)
            bd = jnp.concatenate([jnp.concatenate([v_news[2 * pr], zero], axis=1),
                                  jnp.concatenate([zero, v_news[2 * pr + 1]], axis=1)], axis=0)
            rs.append(_dot(lhs, bd))
        outs = []
        for h in range(GDN_HEADS):
            hs = slice(h * dh, (h + 1) * dh)
            r = rs[h // 2][:, (h % 2) * dh:(h % 2 + 1) * dh]
            state_ref[b, h] = ss[h] * gl_ref[b, 0, :, hs] + r[c:]
            o = ps[h][c:] + r[:c]
            o = o * lax.rsqrt(jnp.mean(o * o, axis=-1, keepdims=True) + EPS) * normw_ref[...]
            outs.append(o * _silu(zg_ref[b, :, hs].astype(F32)))
        o_ref[b] = jnp.concatenate(outs, axis=1).astype(BF16)


def _gdn(proj, small, convw, dtb, alog, normw, expand_a, expand_b, bsz, seqlen):
    c = GDN_CHUNK
    t = GDN_PREP_ROWS
    nc = seqlen // c
    nb = seqlen // t
    n = bsz * seqlen
    hh = GDN_HEADS * c
    rowmap = lambda b, i: b * nb + i
    const = lambda b, i: (0, 0)
    params = pltpu.CompilerParams(dimension_semantics=("arbitrary", "arbitrary"), vmem_limit_bytes=VMEM_LIMIT)
    u, w, qd, aqk, kdt, gl = pl.pallas_call(
        _gdn_prep_kernel,
        grid=(bsz, nb),
        in_specs=[
            pl.BlockSpec((t, 3 * GDN_DIM), lambda b, i: (rowmap(b, i), QKV_BLK)),
            pl.BlockSpec((t, LANES), lambda b, i: (rowmap(b, i), 0)),
            pl.BlockSpec((CONV_K, 3 * GDN_DIM), const),
            pl.BlockSpec((1, LANES), const),
            pl.BlockSpec((1, LANES), const),
            pl.BlockSpec((LANES, GDN_DIM), const),
            pl.BlockSpec((LANES, GDN_DIM), const),
        ],
        out_specs=[
            pl.BlockSpec((t, GDN_DIM), lambda b, i: (rowmap(b, i), 0)),
            pl.BlockSpec((t, GDN_DIM), lambda b, i: (rowmap(b, i), 0)),
            pl.BlockSpec((t, GDN_DIM), lambda b, i: (rowmap(b, i), 0)),
            pl.BlockSpec((t, hh), lambda b, i: (rowmap(b, i), 0)),
            pl.BlockSpec((t // c, GDN_HEAD_DIM, hh), lambda b, i: (rowmap(b, i), 0, 0)),
            pl.BlockSpec((t // c, 1, GDN_DIM), lambda b, i: (rowmap(b, i), 0, 0)),
        ],
        out_shape=[
            jax.ShapeDtypeStruct((n, GDN_DIM), BF16),
            jax.ShapeDtypeStruct((n, GDN_DIM), BF16),
            jax.ShapeDtypeStruct((n, GDN_DIM), BF16),
            jax.ShapeDtypeStruct((n, hh), BF16),
            jax.ShapeDtypeStruct((n // c, GDN_HEAD_DIM, hh), BF16),
            jax.ShapeDtypeStruct((n // c, 1, GDN_DIM), F32),
        ],
        scratch_shapes=[pltpu.VMEM((HALO, 3 * GDN_DIM), F32)],
        compiler_params=params,
        name="gdn_prep",
    )(proj, small, convw, dtb, alog, expand_a, expand_b)

    seq3 = lambda i: (0, i, 0)
    seq4 = lambda i: (0, i, 0, 0)
    o = pl.pallas_call(
        _gdn_scan_kernel,
        grid=(nc,),
        in_specs=[
            pl.BlockSpec((bsz, c, GDN_DIM), seq3),
            pl.BlockSpec((bsz, c, GDN_DIM), seq3),
            pl.BlockSpec((bsz, c, GDN_DIM), seq3),
            pl.BlockSpec((bsz, c, hh), seq3),
            pl.BlockSpec((bsz, 1, GDN_HEAD_DIM, hh), seq4),
            pl.BlockSpec((bsz, 1, 1, GDN_DIM), seq4),
            pl.BlockSpec((bsz, c, GDN_DIM), lambda i: (0, i, ZG_BLK)),
            pl.BlockSpec((1, GDN_HEAD_DIM), lambda i: (0, 0)),
        ],
        out_specs=pl.BlockSpec((bsz, c, GDN_DIM), seq3),
        out_shape=jax.ShapeDtypeStruct((bsz, seqlen, GDN_DIM), BF16),
        scratch_shapes=[pltpu.VMEM((bsz, GDN_HEADS, GDN_HEAD_DIM, GDN_HEAD_DIM), F32)],
        compiler_params=pltpu.CompilerParams(dimension_semantics=("arbitrary",), vmem_limit_bytes=VMEM_LIMIT),
        name="gdn_scan",
    )(u.reshape(bsz, seqlen, GDN_DIM), w.reshape(bsz, seqlen, GDN_DIM), qd.reshape(bsz, seqlen, GDN_DIM),
      aqk.reshape(bsz, seqlen, hh), kdt.reshape(bsz, nc, GDN_HEAD_DIM, hh), gl.reshape(bsz, nc, 1, GDN_DIM),
      proj.reshape(bsz, seqlen, MAIN_COLS), normw)
    return o.reshape(n, GDN_DIM)


def _merge_kernel(y_ref, o_ref, gate_ref, h_ref, bm_ref, wsp_ref, wgp_ref, wout_ref, out_ref):
    d = h_ref.shape[1]
    y_ssd = _dot(y_ref[...], wsp_ref[...])
    y_gdn = _dot(o_ref[...], wgp_ref[...])
    gates = _sigmoid(gate_ref[...].astype(F32) + bm_ref[...])
    merged = gates[:, :d] * y_ssd + gates[:, d:] * y_gdn
    out_ref[...] = h_ref[...] + _dot(merged.astype(BF16), wout_ref[...])


def _merge(y, o, proj, h, b_merge, w_sp, w_gp, w_out, layer, tm):
    n, d = h.shape
    const = lambda i: (0, 0)
    return pl.pallas_call(
        _merge_kernel,
        grid=(n // tm,),
        in_specs=[
            pl.BlockSpec((tm, SSD_D_INNER), lambda i: (i, 0)),
            pl.BlockSpec((tm, GDN_DIM), lambda i: (i, 0)),
            pl.BlockSpec((tm, 2 * d), lambda i: (i, GATE_BLK)),
            pl.BlockSpec((tm, d), lambda i: (i, 0)),
            pl.BlockSpec((1, 2 * d), const),
            pl.BlockSpec((pl.Squeezed(), SSD_D_INNER, d), lambda i: (layer, 0, 0)),
            pl.BlockSpec((pl.Squeezed(), GDN_DIM, d), lambda i: (layer, 0, 0)),
            pl.BlockSpec((pl.Squeezed(), d, d), lambda i: (layer, 0, 0)),
        ],
        out_specs=pl.BlockSpec((tm, d), lambda i: (i, 0)),
        out_shape=jax.ShapeDtypeStruct((n, d), F32),
        compiler_params=pltpu.CompilerParams(
            dimension_semantics=("arbitrary",), vmem_limit_bytes=VMEM_LIMIT),
        name="merge",
    )(y, o, proj, h, b_merge, w_sp, w_gp, w_out)


MOE_PAIRS = MOE_EPG * (MOE_EPG - 1) // 2
MOE_BUCKETS = MOE_GROUPS * MOE_PAIRS
MOE_TILE = 256
ROW_DMA_UNROLL = 32
RT_BUCKET, RT_WLO, RT_WHI, RT_RANK = 0, 1, 2, 3


def _route(logits):
    t = logits.shape[1]
    neg = -jnp.inf
    big = jnp.int32(LANES)
    gl = logits[RT_G:RT_G + MOE_GROUPS, :]
    grow = lax.broadcasted_iota(jnp.int32, (MOE_GROUPS, t), 0)
    gmax = jnp.max(gl, axis=0, keepdims=True)
    gsel = jnp.min(jnp.where(gl == gmax, grow, big), axis=0, keepdims=True)
    group_w = 1.0 / jnp.sum(jnp.exp(gl - gmax), axis=0, keepdims=True)
    erow = lax.broadcasted_iota(jnp.int32, (MOE_EXPERTS, t), 0)
    el = jnp.where((erow >> (MOE_EPG.bit_length() - 1)) == gsel, logits[RT_E:RT_E + MOE_EXPERTS, :], neg)
    m1 = jnp.max(el, axis=0, keepdims=True)
    i1 = jnp.min(jnp.where(el == m1, erow, big), axis=0, keepdims=True)
    el2 = jnp.where(erow == i1, neg, el)
    m2 = jnp.max(el2, axis=0, keepdims=True)
    i2 = jnp.min(jnp.where(el2 == m2, erow, big), axis=0, keepdims=True)
    z = jnp.sum(jnp.exp(el - m1), axis=0, keepdims=True)
    p1 = 1.0 / z
    p2 = jnp.exp(m2 - m1) / z
    tot = p1 + p2
    w1 = group_w * (p1 / tot)
    w2 = group_w * (p2 / tot)
    l1 = i1 - gsel * MOE_EPG
    l2 = i2 - gsel * MOE_EPG
    lo = jnp.minimum(l1, l2)
    hi = jnp.maximum(l1, l2)
    pair = ((lo * (2 * MOE_EPG - 1 - lo)) >> 1) + (hi - lo - 1)
    bucket = gsel * MOE_PAIRS + pair
    first_lo = l1 < l2
    return bucket, jnp.where(first_lo, w1, w2), jnp.where(first_lo, w2, w1)


def _moe_rmsnorm(x, lnw):
    return x * lax.rsqrt(jnp.mean(x * x, axis=-1, keepdims=True) + EPS) * lnw


def _router_kernel(h_ref, lnw_ref, wr_ref, br_ref, info_ref, counts_ref, run_ref):
    t = h_ref.shape[0]

    @pl.when(pl.program_id(0) == 0)
    def _():
        run_ref[...] = jnp.zeros_like(run_ref)

    hi, lo = _split(_moe_rmsnorm(h_ref[...], lnw_ref[...]))
    whi, wlo = wr_ref[0], wr_ref[1]
    logits = _dot_nt(whi, hi) + _dot_nt(whi, lo) + _dot_nt(wlo, hi) + br_ref[:, 0:1]
    bucket, w_lo, w_hi = _route(logits)
    brow = lax.broadcasted_iota(jnp.int32, (MOE_BUCKET_ROWS, t), 0)
    onehot = jnp.where(brow == bucket, 1.0, 0.0)
    row = lax.broadcasted_iota(jnp.int32, (t, t), 0)
    col = lax.broadcasted_iota(jnp.int32, (t, t), 1)
    earlier = _dot(onehot.astype(BF16), jnp.where(row < col, 1.0, 0.0).astype(BF16))
    rank = jnp.sum(onehot * (earlier + run_ref[:, 0:1]), axis=0, keepdims=True)
    run_ref[...] += jnp.sum(onehot, axis=1, keepdims=True)
    record = jnp.concatenate(
        [bucket.astype(F32), w_lo, w_hi, rank, jnp.zeros((LANES - 4, t), F32)], axis=0)
    info_ref[...] = record.T
    counts_ref[...] = run_ref[...]


def _dispatch_kernel(pos_ref, h_ref, info_ref, xs_in_hbm, xs_hbm, buf_ref, sem):
    del xs_in_hbm
    tm, d = h_ref.shape
    i = pl.program_id(0)
    last = pl.num_programs(0) - 1
    base = i * tm
    slot = i % 2

    def wait_rows(s):
        pltpu.make_async_copy(buf_ref.at[s], xs_hbm.at[pl.ds(0, tm)], sem.at[s]).wait()

    @pl.when(i >= 2)
    def _():
        wait_rows(slot)

    buf_ref[slot, :, :d] = h_ref[...]
    buf_ref[slot, :, d:] = info_ref[...]

    def issue(g, carry):
        for k in range(ROW_DMA_UNROLL):
            r = g * ROW_DMA_UNROLL + k
            pltpu.make_async_copy(buf_ref.at[slot, pl.ds(r, 1)], xs_hbm.at[pl.ds(pos_ref[base + r], 1)],
                                  sem.at[slot]).start(priority=k % 2)
        return carry

    lax.fori_loop(0, tm // ROW_DMA_UNROLL, issue, 0)

    @pl.when(i == last)
    def _():
        wait_rows(slot)

    @pl.when((i == last) & (i >= 1))
    def _():
        wait_rows(1 - slot)


def _experts_kernel(lo_ref, hi_ref, nt_ref, x_ref, lnw_ref, wg_lo, wu_lo, wd_lo, wg_hi, wu_hi, wd_hi, y_ref):
    live = pl.program_id(0) < nt_ref[0]
    d = y_ref.shape[1]

    @pl.when(live)
    def _():
        xn = _moe_rmsnorm(x_ref[:, :d], lnw_ref[...]).astype(BF16)
        wts = x_ref[:, d + RT_WLO:d + RT_WHI + 1]
        y = None
        for j, (wg, wu, wd) in enumerate(((wg_lo, wu_lo, wd_lo), (wg_hi, wu_hi, wd_hi))):
            hid = _silu(_dot(xn, wg[0])) * _dot(xn, wu[0]) * wts[:, j:j + 1]
            part = _dot(hid.astype(BF16), wd[0])
            y = part if y is None else y + part
        y_ref[...] = x_ref[:, :d] + y

    @pl.when(jnp.logical_not(live))
    def _():
        y_ref[...] = jnp.zeros_like(y_ref)


def _combine_kernel(pos_ref, y_hbm, fnw_ref, out_ref, buf_ref, sem, *, final_norm):
    tm = out_ref.shape[0]
    i = pl.program_id(0)
    slot = i % 2

    def issue_tile(tile, s):
        base = tile * tm

        def issue(g, carry):
            for k in range(ROW_DMA_UNROLL):
                r = g * ROW_DMA_UNROLL + k
                pltpu.make_async_copy(y_hbm.at[pl.ds(pos_ref[base + r], 1)], buf_ref.at[s, pl.ds(r, 1)],
                                      sem.at[s]).start(priority=k % 2)
            return carry

        lax.fori_loop(0, tm // ROW_DMA_UNROLL, issue, 0)

    @pl.when(i == 0)
    def _():
        issue_tile(0, 0)

    @pl.when(i + 1 < pl.num_programs(0))
    def _():
        issue_tile(i + 1, 1 - slot)

    pltpu.make_async_copy(y_hbm.at[pl.ds(0, tm)], buf_ref.at[slot], sem.at[slot]).wait()
    y = buf_ref[slot]
    if final_norm:
        y = _moe_rmsnorm(y, fnw_ref[...])
    out_ref[...] = y


def _moe(h, lnw, w_router, b_router, w_gate, w_up, w_down, fnw, layer, tm, final_norm, out_order, slots):
    n, d = h.shape
    dff = w_gate.shape[-1]
    tile = MOE_TILE
    n_tiles = n // tile + MOE_BUCKETS
    p = n_tiles * tile
    const = lambda i: (0, 0)
    info, counts = pl.pallas_call(
        _router_kernel,
        grid=(n // tm,),
        in_specs=[
            pl.BlockSpec((tm, d), lambda i: (i, 0)),
            pl.BlockSpec((1, d), const),
            pl.BlockSpec((2, LANES, d), lambda i: (0, 0, 0)),
            pl.BlockSpec((LANES, LANES), const),
        ],
        out_specs=[pl.BlockSpec((tm, LANES), lambda i: (i, 0)), pl.BlockSpec((MOE_BUCKET_ROWS, LANES), const)],
        out_shape=[jax.ShapeDtypeStruct((n, LANES), F32), jax.ShapeDtypeStruct((MOE_BUCKET_ROWS, LANES), F32)],
        scratch_shapes=[pltpu.VMEM((MOE_BUCKET_ROWS, LANES), F32)],
        compiler_params=pltpu.CompilerParams(dimension_semantics=("arbitrary",), vmem_limit_bytes=VMEM_LIMIT),
        name="moe_router",
    )(h, lnw, w_router, b_router)

    tiles_per = (counts[:MOE_BUCKETS, 0].astype(jnp.int32) + tile - 1) // tile
    tile_end = jnp.cumsum(tiles_per)
    first_slot = ((tile_end - tiles_per) * tile).astype(F32)
    bucket_is = info[:, RT_BUCKET:RT_BUCKET + 1] == jnp.arange(MOE_BUCKETS, dtype=F32)[None, :]
    pos = (jnp.sum(jnp.where(bucket_is, first_slot[None, :], 0.0), axis=1) + info[:, RT_RANK]).astype(jnp.int32)
    n_live = tile_end[-1]
    tile_bucket = jnp.searchsorted(tile_end, jnp.minimum(jnp.arange(n_tiles, dtype=jnp.int32), n_live - 1),
                                   side="right").astype(jnp.int32)
    pairs = [(a, b) for a in range(MOE_EPG) for b in range(a + 1, MOE_EPG)]
    grp = tile_bucket // MOE_PAIRS
    tile_lo = grp * MOE_EPG + jnp.array([a for a, _ in pairs], jnp.int32)[tile_bucket % MOE_PAIRS]
    tile_hi = grp * MOE_EPG + jnp.array([b for _, b in pairs], jnp.int32)[tile_bucket % MOE_PAIRS]

    x_sorted = pl.pallas_call(
        _dispatch_kernel,
        grid_spec=pltpu.PrefetchScalarGridSpec(
            num_scalar_prefetch=1, grid=(n // tile,),
            in_specs=[
                pl.BlockSpec((tile, d), lambda i, pos: (i, 0)),
                pl.BlockSpec((tile, LANES), lambda i, pos: (i, 0)),
                pl.BlockSpec(memory_space=pl.ANY),
            ],
            out_specs=pl.BlockSpec(memory_space=pl.ANY),
            scratch_shapes=[pltpu.VMEM((2, tile, d + LANES), F32), pltpu.SemaphoreType.DMA((2,))]),
        out_shape=jax.ShapeDtypeStruct((p, d + LANES), F32),
        input_output_aliases={3: 0},
        compiler_params=pltpu.CompilerParams(dimension_semantics=("arbitrary",), vmem_limit_bytes=VMEM_LIMIT),
        name="moe_dispatch",
    )(pos, h, info, jnp.zeros((p, d + LANES), F32) if slots is None else slots)

    lo_map = lambda t, lo, hi, nt: (layer, lo[t], 0, 0)
    hi_map = lambda t, lo, hi, nt: (layer, hi[t], 0, 0)
    y_sorted = pl.pallas_call(
        _experts_kernel,
        grid_spec=pltpu.PrefetchScalarGridSpec(
            num_scalar_prefetch=3, grid=(n_tiles,),
            in_specs=[
                pl.BlockSpec((tile, d + LANES), lambda t, lo, hi, nt: (t, 0)),
                pl.BlockSpec((1, d), lambda t, lo, hi, nt: (0, 0)),
                pl.BlockSpec((pl.Squeezed(), 1, d, dff), lo_map),
                pl.BlockSpec((pl.Squeezed(), 1, d, dff), lo_map),
                pl.BlockSpec((pl.Squeezed(), 1, dff, d), lo_map),
                pl.BlockSpec((pl.Squeezed(), 1, d, dff), hi_map),
                pl.BlockSpec((pl.Squeezed(), 1, d, dff), hi_map),
                pl.BlockSpec((pl.Squeezed(), 1, dff, d), hi_map),
            ],
            out_specs=pl.BlockSpec((tile, d), lambda t, lo, hi, nt: (t, 0))),
        out_shape=jax.ShapeDtypeStruct((p, d), F32),
        compiler_params=pltpu.CompilerParams(dimension_semantics=("arbitrary",), vmem_limit_bytes=VMEM_LIMIT),
        name="moe_experts",
    )(tile_lo, tile_hi, n_live.reshape(1).astype(jnp.int32), x_sorted, lnw,
      w_gate, w_up, w_down, w_gate, w_up, w_down)

    out = pl.pallas_call(
        functools.partial(_combine_kernel, final_norm=final_norm),
        grid_spec=pltpu.PrefetchScalarGridSpec(
            num_scalar_prefetch=1, grid=(n // tile,),
            in_specs=[
                pl.BlockSpec(memory_space=pl.ANY),
                pl.BlockSpec((1, d), lambda i, pos: (0, 0)),
            ],
            out_specs=pl.BlockSpec((tile, d), lambda i, pos: (i, 0)),
            scratch_shapes=[pltpu.VMEM((2, tile, d), F32), pltpu.SemaphoreType.DMA((2,))]),
        out_shape=jax.ShapeDtypeStruct((n, d), F32),
        compiler_params=pltpu.CompilerParams(dimension_semantics=("arbitrary",), vmem_limit_bytes=VMEM_LIMIT),
        name="moe_combine",
    )(pos if out_order is None else out_order(pos), y_sorted, fnw)
    return out, x_sorted


def _lane_row(v, offset):
    return jnp.zeros((1, LANES), F32).at[0, offset:offset + v.shape[0]].set(v.astype(F32))


def _expansion(offset, heads, width):
    r = jnp.arange(LANES)[:, None]
    c = jnp.arange(heads * width)[None, :]
    return (r == offset + c // width).astype(BF16)


def _row_tile(n, want):
    t = min(n, want)
    while n % t:
        t //= 2
    return t


def kernel(x, ln1_w, w_in, ssd_conv_w, ssd_conv_b, ssd_dt_bias, ssd_a_log, ssd_d, ssd_norm_w, w_ssd_proj,
           gdn_conv_w, gdn_dt_bias, gdn_a_log, gdn_norm_w, w_gdn_proj, b_merge, w_out, ln2_w,
           moe_w_group, moe_b_group, moe_w_expert, moe_b_expert, moe_w_gate, moe_w_up, moe_w_down,
           final_norm_w):
    bsz, seqlen, d = x.shape
    n = bsz * seqlen
    depth = w_in.shape[0]

    def swap_order(a):
        blocks = a.reshape((n // TIME_BLOCK, SUBLANES, SUBLANES) + a.shape[1:])
        return blocks.swapaxes(1, 2).reshape(a.shape)

    h = swap_order(x.reshape(n, d))

    o1 = SSD_D_INNER
    o2 = o1 + SSD_XBC
    o3 = o2 + SSD_HEADS
    o4 = o3 + 3 * GDN_DIM
    o5 = o4 + GDN_HEADS
    o6 = o5 + GDN_HEADS
    o7 = o6 + GDN_DIM

    e_ssd = _expansion(SM_DT, SSD_HEADS, SSD_HEAD_DIM)
    e_gdn_a = _expansion(SM_A, GDN_HEADS, GDN_HEAD_DIM)
    e_gdn_b = _expansion(SM_BETA, GDN_HEADS, GDN_HEAD_DIM)
    tm_proj = _row_tile(n, 1024)
    tm_merge = _row_tile(n, 512)
    tm_moe = _row_tile(n, 512)

    w_main = jnp.concatenate(
        [w_in[:, :, o1:o2], w_in[:, :, o3:o4], w_in[:, :, :o1], w_in[:, :, o7:], w_in[:, :, o6:o7]],
        axis=2).astype(BF16)
    w_small = jnp.concatenate(
        [w_in[:, :, o2:o3], w_in[:, :, o4:o5], w_in[:, :, o5:o6],
         jnp.zeros((depth, d, LANES - SSD_HEADS - 2 * GDN_HEADS), F32)], axis=2).astype(BF16)
    w_sp, w_gp, w_o = w_ssd_proj.astype(BF16), w_gdn_proj.astype(BF16), w_out.astype(BF16)
    w_gate, w_up, w_down = moe_w_gate.astype(BF16), moe_w_up.astype(BF16), moe_w_down.astype(BF16)

    slots = None
    for i in range(depth):
        proj, small = _in_proj(h, ln1_w[i][None, :], w_main, w_small, i, tm_proj, MAIN_COLS // 4)

        y = _ssd(proj, small, ssd_conv_w[i], ssd_conv_b[i][None, :],
                 _lane_row(ssd_dt_bias[i], SM_DT), _lane_row(ssd_a_log[i], SM_DT),
                 jnp.repeat(ssd_d[i], SSD_HEAD_DIM)[None, :], ssd_norm_w[i][None, :], e_ssd, bsz, seqlen)
        o = _gdn(proj, small, gdn_conv_w[i], _lane_row(gdn_dt_bias[i], SM_A), _lane_row(gdn_a_log[i], SM_A),
                 gdn_norm_w[i][None, :], e_gdn_a, e_gdn_b, bsz, seqlen)
        h = _merge(y, o, proj, h, b_merge[i][None, :], w_sp, w_gp, w_o, i, tm_merge)

        gap = jnp.zeros((d, RT_E - RT_G - MOE_GROUPS), F32)
        rest = jnp.zeros((d, LANES - RT_E - MOE_EXPERTS), F32)
        w_r = jnp.concatenate([moe_w_group[i], gap, moe_w_expert[i], rest], axis=1).T
        w_r_hi = w_r.astype(BF16)
        w_r_lo = (w_r - w_r_hi.astype(F32)).astype(BF16)
        b_r = jnp.broadcast_to(
            jnp.concatenate([moe_b_group[i], gap[0], moe_b_expert[i], rest[0]])[:, None], (LANES, LANES))
        h, slots = _moe(h, ln2_w[i][None, :], jnp.stack([w_r_hi, w_r_lo]), b_r, w_gate, w_up, w_down,
                        final_norm_w[None, :], i, tm_moe, final_norm=(i == depth - 1),
                        out_order=swap_order if i == depth - 1 else None, slots=slots)
    return h.reshape(bsz, seqlen, d)
```
